```python
import math
import jax, jax.numpy as jnp
from jax import lax
import numpy as np

D_MODEL = 1024
BATCH = 16
SEQ = 4096
DEPTH = 1

NORM_EPS = 1e-6
DA_QK_DIM = 64
DA_V_DIM = 2 * DA_QK_DIM
DA_HEADS = D_MODEL // DA_V_DIM
DA_QK_W = DA_HEADS * 2 * DA_QK_DIM
DA_V_W = DA_HEADS * DA_V_DIM
ROPE_THETA = 500000.0
ROPE_DIM = DA_QK_DIM // 4
Q_BLOCK = 128
ML_HEADS = 8
ML_QK_DIM = 64
ML_V_DIM = D_MODEL // ML_HEADS
ML_QK_W = ML_HEADS * ML_QK_DIM
ML_V_W = ML_HEADS * ML_V_DIM
ML_CHUNK = 64
CONV_WIDTH = 4
PEER_HEADS = 8
PEER_N_KEYS = 128
PEER_N_EXPERTS = PEER_N_KEYS * PEER_N_KEYS
PEER_QUERY_DIM = 256
PEER_HALF = PEER_QUERY_DIM // 2
PEER_TOPK = 16
PEER_TOKEN_BLOCK = 128
IN_SIZES = (DA_QK_W, DA_QK_W, DA_V_W, ML_QK_W, ML_QK_W, ML_V_W, ML_V_W, ML_HEADS, ML_HEADS, D_MODEL, D_MODEL)
IN_WIDTH = sum(IN_SIZES)

kernel_name = "hybrid_diffattn_mlstm_peer_block"


def rmsnorm(x, w):
    xf = x.astype(jnp.float32)
    y = xf * lax.rsqrt(jnp.mean(xf * xf, axis=-1, keepdims=True) + NORM_EPS)
    return (y * w.astype(jnp.float32)).astype(x.dtype)


def split_cols(t, sizes):
    idx = np.cumsum(np.array(sizes))[:-1].tolist()
    return jnp.split(t, idx, axis=-1)


def partial_rope(x, pos):
    half = ROPE_DIM // 2
    inv = ROPE_THETA ** (-jnp.arange(half, dtype=jnp.float32) * 2.0 / ROPE_DIM)
    ang = pos.astype(jnp.float32)[:, None] * inv[None, :]
    cos = jnp.cos(ang)[None, :, None, None, :]
    sin = jnp.sin(ang)[None, :, None, None, :]
    xr = x[..., :ROPE_DIM].astype(jnp.float32)
    x1, x2 = xr[..., :half], xr[..., half:]
    rot = jnp.concatenate([x1 * cos - x2 * sin, x2 * cos + x1 * sin], axis=-1)
    return jnp.concatenate([rot.astype(x.dtype), x[..., ROPE_DIM:]], axis=-1)


def diff_attention(q, k, v, lam, subln_w, lam_init):
    B, S, H, _, dk = q.shape
    scale = dk ** -0.5
    outs = []
    for s0 in range(0, S, Q_BLOCK):
        s1 = s0 + Q_BLOCK
        qb, kb, vb = q[:, s0:s1], k[:, :s1], v[:, :s1]
        sc = jnp.einsum('bqhcd,bkhcd->bhcqk', qb, kb).astype(jnp.float32) * scale
        mask = jnp.arange(s1)[None, :] <= jnp.arange(s0, s1)[:, None]
        p = jax.nn.softmax(jnp.where(mask, sc, -jnp.inf), axis=-1)
        a = p[:, :, 0] - lam * p[:, :, 1]
        outs.append(jnp.einsum('bhqk,bkhd->bqhd', a.astype(v.dtype), vb))
    o = jnp.concatenate(outs, axis=1)
    o = rmsnorm(o, subln_w) * (1.0 - lam_init)
    return o.reshape(B, S, H * v.shape[-1])


def causal_conv(x, w, b):
    K = w.shape[0]
    S = x.shape[1]
    xp = jnp.pad(x, ((0, 0), (K - 1, 0), (0, 0)))
    y = b
    for j in range(K):
        y = y + xp[:, j:j + S] * w[j]
    return y


def mlstm_chunkwise(q, k, v, i_pre, f_pre):
    B, H, S, dk = q.shape
    dv = v.shape[-1]
    L = ML_CHUNK
    NC = S // L
    f32 = jnp.float32
    qc = q.astype(f32).reshape(B, H, NC, L, dk)
    kc = k.astype(f32).reshape(B, H, NC, L, dk)
    vc = v.astype(f32).reshape(B, H, NC, L, dv)
    log_f = jax.nn.log_sigmoid(f_pre.astype(f32)).reshape(B, H, NC, L)
    ig = i_pre.astype(f32).reshape(B, H, NC, L)
    b = jnp.cumsum(log_f, axis=-1)
    g = b[..., -1]
    a = g[..., None] - b + ig
    m_loc = jnp.max(a, axis=-1)
    w_loc = jnp.exp(a - m_loc[..., None])
    wk = w_loc[..., None] * kc
    C_loc = jnp.einsum('bhcld,bhcle->bhcde', wk, vc)
    n_loc = jnp.sum(wk, axis=-2)

    def step(carry, inp):
        C, n, m = carry
        g_c, m_loc_c, C_loc_c, n_loc_c = inp
        m_new = jnp.maximum(g_c + m, m_loc_c)
        dec = jnp.exp(g_c + m - m_new)
        inc = jnp.exp(m_loc_c - m_new)
        C_new = dec[..., None, None] * C + inc[..., None, None] * C_loc_c
        n_new = dec[..., None] * n + inc[..., None] * n_loc_c
        return (C_new, n_new, m_new), (C, n, m)

    init = (jnp.zeros((B, H, dk, dv), f32), jnp.zeros((B, H, dk), f32), jnp.zeros((B, H), f32))
    xs = (jnp.moveaxis(g, 2, 0), jnp.moveaxis(m_loc, 2, 0), jnp.moveaxis(C_loc, 2, 0), jnp.moveaxis(n_loc, 2, 0))
    _, (C_prev, n_prev, m_prev) = lax.scan(step, init, xs)
    C_prev = jnp.moveaxis(C_prev, 0, 2)
    n_prev = jnp.moveaxis(n_prev, 0, 2)
    m_prev = jnp.moveaxis(m_prev, 0, 2)
    Dm = b[..., :, None] - b[..., None, :] + ig[..., None, :]
    causal = jnp.tril(jnp.ones((L, L), dtype=bool))
    Dm = jnp.where(causal, Dm, -jnp.inf)
    m_inter = b + m_prev[..., None]
    m_j = jnp.maximum(jnp.max(Dm, axis=-1), m_inter)
    W = jnp.exp(Dm - m_j[..., None])
    qk = jnp.einsum('bhcjd,bhckd->bhcjk', qc, kc) * W
    inter_w = jnp.exp(m_inter - m_j)
    num = jnp.einsum('bhcjk,bhcke->bhcje', qk, vc) + inter_w[..., None] * jnp.einsum('bhcjd,bhcde->bhcje', qc, C_prev)
    den = jnp.sum(qk, axis=-1) + inter_w * jnp.einsum('bhcjd,bhcd->bhcj', qc, n_prev)
    h = num / jnp.maximum(jnp.abs(den), jnp.exp(-m_j))[..., None]
    return h.reshape(B, H, S, dv).astype(v.dtype)


def peer_ffn(h, w_q, sub_keys, U, V):
    B, S, D = h.shape
    T = B * S
    hf = h.reshape(T, D)
    q = (hf @ w_q).reshape(T, PEER_HEADS, 2, PEER_HALF)
    s = jnp.einsum('thpd,hpnd->thpn', q, sub_keys).astype(jnp.float32)
    v_top, i_top = lax.top_k(s, PEER_TOPK)
    cand = v_top[:, :, 0, :, None] + v_top[:, :, 1, None, :]
    cand_idx = i_top[:, :, 0, :, None] * PEER_N_KEYS + i_top[:, :, 1, None, :]
    cand = cand.reshape(T, PEER_HEADS, PEER_TOPK * PEER_TOPK)
    cand_idx = cand_idx.reshape(T, PEER_HEADS, PEER_TOPK * PEER_TOPK)
    sc, sel = lax.top_k(cand, PEER_TOPK)
    idx = jnp.take_along_axis(cand_idx, sel, axis=-1)
    gate = jax.nn.softmax(sc, axis=-1).astype(h.dtype)
    nb = T // PEER_TOKEN_BLOCK

    def block(args):
        hb, ib, gb = args
        act = jax.nn.gelu(jnp.einsum('thed,td->the', U[ib], hb), approximate=False)
        return jnp.einsum('the,thed->td', gb * act, V[ib])

    y = lax.map(block, (hf.reshape(nb, PEER_TOKEN_BLOCK, D),
                        idx.reshape(nb, PEER_TOKEN_BLOCK, PEER_HEADS, PEER_TOPK),
                        gate.reshape(nb, PEER_TOKEN_BLOCK, PEER_HEADS, PEER_TOPK)))
    return y.reshape(B, S, D)


def setup_inputs(seed: int = 0) -> dict:
    key = jax.random.key(seed)
    ks = jax.random.split(key, 24)
    nrm = jax.random.normal
    D = D_MODEL
    f32 = jnp.float32
    return {
        "x": nrm(ks[0], (BATCH, SEQ, D), f32),
        "c": nrm(ks[1], (BATCH, D), f32),
        "ada_w": nrm(ks[2], (DEPTH, D, 6 * D), f32) * D ** -0.5,
        "ada_b": nrm(ks[3], (DEPTH, 6 * D), f32) * 0.01,
        "norm1_w": 1.0 + 0.02 * nrm(ks[4], (DEPTH, D), f32),
        "norm2_w": 1.0 + 0.02 * nrm(ks[5], (DEPTH, D), f32),
        "w_in": nrm(ks[6], (DEPTH, D, IN_WIDTH), f32) * D ** -0.5,
        "conv_w": nrm(ks[7], (DEPTH, CONV_WIDTH, 2 * ML_QK_W), f32) * CONV_WIDTH ** -0.5,
        "conv_b": nrm(ks[8], (DEPTH, 2 * ML_QK_W), f32) * 0.01,
        "ml_i_bias": nrm(ks[9], (DEPTH, ML_HEADS), f32) * 0.1,
        "ml_f_bias": jnp.linspace(3.0, 6.0, ML_HEADS, dtype=f32)[None, :] + 0.1 * nrm(ks[10], (DEPTH, ML_HEADS), f32),
        "ml_norm_w": 1.0 + 0.02 * nrm(ks[11], (DEPTH, ML_V_W), f32),
        "lam_q1": 0.1 * nrm(ks[12], (DEPTH, DA_QK_DIM), f32),
        "lam_k1": 0.1 * nrm(ks[13], (DEPTH, DA_QK_DIM), f32),
        "lam_q2": 0.1 * nrm(ks[14], (DEPTH, DA_QK_DIM), f32),
        "lam_k2": 0.1 * nrm(ks[15], (DEPTH, DA_QK_DIM), f32),
        "subln_w": 1.0 + 0.02 * nrm(ks[16], (DEPTH, DA_V_DIM), f32),
        "w_out": nrm(ks[17], (DEPTH, D, D), f32) * D ** -0.5,
        "peer_wq": nrm(ks[18], (DEPTH, D, PEER_HEADS * PEER_QUERY_DIM), f32) * D ** -0.5,
        "peer_keys": nrm(ks[19], (DEPTH, PEER_HEADS, 2, PEER_N_KEYS, PEER_HALF), f32) * PEER_HALF ** -0.5,
        "peer_u": nrm(ks[20], (DEPTH, PEER_N_EXPERTS, D), f32) * D ** -0.5,
        "peer_v": nrm(ks[21], (DEPTH, PEER_N_EXPERTS, D), f32) * 0.5,
        "final_norm_w": 1.0 + 0.02 * nrm(ks[22], (D,), f32),
    }


def _to_heads(t, B, S, d):
    return t.reshape(B, S, ML_HEADS, d).transpose(0, 2, 1, 3)


def reference(x, c, ada_w, ada_b, norm1_w, norm2_w, w_in, conv_w, conv_b, ml_i_bias, ml_f_bias, ml_norm_w,
              lam_q1, lam_k1, lam_q2, lam_k2, subln_w, w_out, peer_wq, peer_keys, peer_u, peer_v, final_norm_w):
    B, S, D = x.shape
    pos = jnp.arange(S, dtype=jnp.int32)
    cond = jax.nn.silu(c)
    for l in range(DEPTH):
        mod = cond @ ada_w[l] + ada_b[l]
        sh1, sc1, gt1, sh2, sc2, gt2 = jnp.split(mod[:, None, :], 6, axis=-1)
        h = rmsnorm(x, norm1_w[l]) * (1.0 + sc1) + sh1
        qa, ka, va, qm, km, vm, om, ip, fp, ga, gm = split_cols(h @ w_in[l], IN_SIZES)
        qa = partial_rope(qa.reshape(B, S, DA_HEADS, 2, DA_QK_DIM), pos)
        ka = partial_rope(ka.reshape(B, S, DA_HEADS, 2, DA_QK_DIM), pos)
        va = va.reshape(B, S, DA_HEADS, DA_V_DIM)
        lam_init = 0.8 - 0.6 * math.exp(-0.3 * l)
        lam = (jnp.exp(jnp.sum(lam_q1[l].astype(jnp.float32) * lam_k1[l].astype(jnp.float32)))
               - jnp.exp(jnp.sum(lam_q2[l].astype(jnp.float32) * lam_k2[l].astype(jnp.float32))) + lam_init)
        y_a = diff_attention(qa, ka, va, lam, subln_w[l], lam_init)
        qk = jax.nn.silu(causal_conv(jnp.concatenate([qm, km], axis=-1), conv_w[l], conv_b[l]))
        qm, km = qk[..., :ML_QK_W], qk[..., ML_QK_W:]
        hm = mlstm_chunkwise(_to_heads(qm, B, S, ML_QK_DIM),
                             _to_heads(km, B, S, ML_QK_DIM) * (ML_QK_DIM ** -0.5),
                             _to_heads(vm, B, S, ML_V_DIM),
                             (ip + ml_i_bias[l]).transpose(0, 2, 1),
                             (fp + ml_f_bias[l]).transpose(0, 2, 1))
        hm = rmsnorm(hm.transpose(0, 2, 1, 3), ml_norm_w[l].reshape(ML_HEADS, ML_V_DIM))
        y_m = hm.reshape(B, S, ML_V_W) * jax.nn.sigmoid(om)
        merged = jax.nn.sigmoid(ga) * y_a + jax.nn.sigmoid(gm) * y_m
        x = x + gt1 * (merged @ w_out[l])
        h2 = rmsnorm(x, norm2_w[l]) * (1.0 + sc2) + sh2
        x = x + gt2 * peer_ffn(h2, peer_wq[l], peer_keys[l], peer_u[l], peer_v[l])
    return rmsnorm(x, final_norm_w)
```

```python
import functools
import math

import jax
import jax.numpy as jnp
from jax import lax
from jax.experimental import pallas as pl
from jax.experimental.pallas import tpu as pltpu

F32 = jnp.float32
BF16 = jnp.bfloat16
HIGHEST = lax.Precision.HIGHEST

NORM_EPS = 1e-6
D_MODEL = 1024
HEADS = 8
HEAD_V = 128
DA_QK = 64
ROPE_DIM = 16
ROPE_HALF = 8
ROPE_THETA = 500000.0
ML_QK = 64
CONV_WIDTH = 4
ML_CHUNK = 128
N_KEYS = 128
TOPK = 16
LANES = 128
SUBLANES = 8
VMEM_LIMIT = 48 * 1024 * 1024

_CAND = [(r1, r2) for r1 in range(TOPK) for r2 in range(TOPK) if (r1 + 1) * (r2 + 1) <= TOPK]
_CAND_ROWS = -(-len(_CAND) // SUBLANES) * SUBLANES


def _cparams(sem):
    return pltpu.CompilerParams(dimension_semantics=sem, vmem_limit_bytes=VMEM_LIMIT)


def _rms(x):
    return x * lax.rsqrt(jnp.mean(x * x, axis=-1, keepdims=True) + NORM_EPS)


def _sigmoid(x):
    return 1.0 / (1.0 + jnp.exp(-x))


def _log_sigmoid(x):
    return jnp.minimum(x, 0.0) - jnp.log(1.0 + jnp.exp(-jnp.abs(x)))


def _mod_kernel(c_ref, w_ref, b_ref, o_ref):
    c = c_ref[...]
    cond = c * _sigmoid(c)
    o_ref[0] = jnp.dot(cond, w_ref[...], precision=HIGHEST, preferred_element_type=F32) + b_ref[...]


def _mod(c, ada_w, ada_b):
    B, D = c.shape
    out = pl.pallas_call(
        _mod_kernel,
        grid=(6,),
        in_specs=[pl.BlockSpec((B, D), lambda j: (0, 0)),
                  pl.BlockSpec((D, D), lambda j: (0, j)),
                  pl.BlockSpec((1, D), lambda j: (0, j))],
        out_specs=pl.BlockSpec((1, B, D), lambda j: (j, 0, 0)),
        out_shape=jax.ShapeDtypeStruct((6, B, D), F32),
        compiler_params=_cparams(("parallel",)),
        name="mod",
    )(c, ada_w, ada_b.reshape(1, 6 * D))
    return out.reshape(6 * B, 1, D)


def _norm_kernel(x_ref, w_ref, sc_ref, sh_ref, o_ref):
    y = _rms(x_ref[...]) * w_ref[...]
    o_ref[...] = (y * (1.0 + sc_ref[0]) + sh_ref[0]).astype(o_ref.dtype)


def _norm(x2, w, mod3, B, tps, tm):
    T, D = x2.shape
    return pl.pallas_call(
        _norm_kernel,
        grid=(T // tm,),
        in_specs=[pl.BlockSpec((tm, D), lambda i: (i, 0)),
                  pl.BlockSpec((1, D), lambda i: (0, 0)),
                  pl.BlockSpec((1, 1, D), lambda i: (1 * B + i // tps, 0, 0)),
                  pl.BlockSpec((1, 1, D), lambda i: (0 * B + i // tps, 0, 0))],
        out_specs=pl.BlockSpec((tm, D), lambda i: (i, 0)),
        out_shape=jax.ShapeDtypeStruct((T, D), BF16),
        compiler_params=_cparams(("parallel",)),
        name="norm1",
    )(x2, w.reshape(1, D), mod3, mod3)


def _proj_kernel(h_ref, w_ref, o_ref):
    o_ref[...] = jnp.dot(h_ref[...], w_ref[...], preferred_element_type=F32).astype(o_ref.dtype)


def _proj(h, w, tm, tn):
    T, D = h.shape
    N = w.shape[1]
    return pl.pallas_call(
        _proj_kernel,
        grid=(N // tn, T // tm),
        in_specs=[pl.BlockSpec((tm, D), lambda j, i: (i, 0)),
                  pl.BlockSpec((D, tn), lambda j, i: (0, j))],
        out_specs=pl.BlockSpec((tm, tn), lambda j, i: (i, j)),
        out_shape=jax.ShapeDtypeStruct((T, N), BF16),
        compiler_params=_cparams(("parallel", "parallel")),
        name="proj_plain",
    )(h, w)


def _rope_proj_kernel(h_ref, w_ref, cos_ref, sa_ref, sb_ref, o_ref):
    acc = jnp.dot(h_ref[...], w_ref[...], preferred_element_type=F32)
    scale = jnp.where(pl.program_id(0) == 0, DA_QK ** -0.5, 1.0).astype(F32)
    cosf, sa, sb = cos_ref[...], sa_ref[...], sb_ref[...]
    for hh in range(HEADS):
        blk = acc[:, hh * LANES:(hh + 1) * LANES]
        rot = (blk * cosf + pltpu.roll(blk, LANES - ROPE_HALF, 1) * sa
               + pltpu.roll(blk, ROPE_HALF, 1) * sb)
        o_ref[:, hh * LANES:(hh + 1) * LANES] = (rot * scale).astype(o_ref.dtype)


def _rope_proj(h, w, tables, tps, tm):
    T, D = h.shape
    cosf, sa, sb = tables
    tab_spec = pl.BlockSpec((tm, LANES), lambda j, i: (i % tps, 0))
    return pl.pallas_call(
        _rope_proj_kernel,
        grid=(2, T // tm),
        in_specs=[pl.BlockSpec((tm, D), lambda j, i: (i, 0)),
                  pl.BlockSpec((D, D), lambda j, i: (0, j)),
                  tab_spec, tab_spec, tab_spec],
        out_specs=pl.BlockSpec((tm, D), lambda j, i: (i, j)),
        out_shape=jax.ShapeDtypeStruct((T, 2 * D), BF16),
        compiler_params=_cparams(("parallel", "parallel")),
        name="proj_rope",
    )(h, w, cosf, sa, sb)


def _conv_proj_kernel(tps, h_ref, w_ref, wgi_ref, wgf_ref, wgit_ref, wgft_ref, cw_ref, cb_ref,
                      bi_ref, bf_ref, bit_ref, bft_ref,
                      o_ref, gi_ref, gf_ref, git_ref, gft_ref, buf):
    i = pl.program_id(0)
    tm = h_ref.shape[0]
    h = h_ref[...]
    acc = jnp.dot(h, w_ref[...], preferred_element_type=F32)

    @pl.when(i % tps == 0)
    def _():
        buf[0:SUBLANES, :] = jnp.zeros((SUBLANES, acc.shape[1]), F32)

    buf[SUBLANES:SUBLANES + tm, :] = acc
    cw = cw_ref[...]
    y = cb_ref[...]
    for j in range(CONV_WIDTH):
        off = SUBLANES - (CONV_WIDTH - 1) + j
        y = y + buf[off:off + tm, :] * cw[j:j + 1, :]
    buf[0:SUBLANES, :] = buf[tm:tm + SUBLANES, :]
    y = y * _sigmoid(y)
    lane = lax.broadcasted_iota(jnp.int32, (1, y.shape[1]), 1)
    kscale = jnp.where(lane >= HEADS * ML_QK, ML_QK ** -0.5, 1.0).astype(F32)
    o_ref[...] = (y * kscale).astype(o_ref.dtype)

    gi_ref[...] = jnp.dot(h, wgi_ref[...], preferred_element_type=F32)[:, :HEADS] + bi_ref[...]
    gf_ref[...] = jnp.dot(h, wgf_ref[...], preferred_element_type=F32)[:, :HEADS] + bf_ref[...]
    nt = (((1,), (1,)), ((), ()))
    git_ref[...] = lax.dot_general(wgit_ref[...], h, nt, preferred_element_type=F32) + bit_ref[...]
    gft_ref[...] = lax.dot_general(wgft_ref[...], h, nt, preferred_element_type=F32) + bft_ref[...]


def _conv_proj(h, w, wgi, wgf, conv_w, conv_b, bi, bf, tps, tm):
    T, D = h.shape
    N = w.shape[1]
    pad = lambda a: jnp.pad(a, ((0, 0), (0, LANES - a.shape[1])))
    full = lambda shape: pl.BlockSpec(shape, lambda i: tuple(0 for _ in shape))
    return pl.pallas_call(
        functools.partial(_conv_proj_kernel, tps),
        grid=(T // tm,),
        in_specs=[pl.BlockSpec((tm, D), lambda i: (i, 0)),
                  full((D, N)), full((D, LANES)), full((D, LANES)),
                  full((HEADS, D)), full((HEADS, D)),
                  full((CONV_WIDTH, N)), full((1, N)),
                  full((1, HEADS)), full((1, HEADS)), full((HEADS, 1)), full((HEADS, 1))],
        out_specs=[pl.BlockSpec((tm, N), lambda i: (i, 0)),
                   pl.BlockSpec((tm, HEADS), lambda i: (i, 0)),
                   pl.BlockSpec((tm, HEADS), lambda i: (i, 0)),
                   pl.BlockSpec((HEADS, tm), lambda i: (0, i)),
                   pl.BlockSpec((HEADS, tm), lambda i: (0, i))],
        out_shape=[jax.ShapeDtypeStruct((T, N), BF16),
                   jax.ShapeDtypeStruct((T, HEADS), F32),
                   jax.ShapeDtypeStruct((T, HEADS), F32),
                   jax.ShapeDtypeStruct((HEADS, T), F32),
                   jax.ShapeDtypeStruct((HEADS, T), F32)],
        scratch_shapes=[pltpu.VMEM((tm + 2 * SUBLANES, N), F32)],
        compiler_params=_cparams(("arbitrary",)),
        name="proj_conv",
    )(h, w, pad(wgi), pad(wgf), wgi.T, wgf.T, conv_w, conv_b.reshape(1, N),
      bi.reshape(1, HEADS), bf.reshape(1, HEADS), bi.reshape(HEADS, 1), bf.reshape(HEADS, 1))


def _attn_kernel(lam_init, q_ref, k_ref, v_ref, ga_ref, lq1_ref, lk1_ref, lq2_ref, lk2_ref, sw_ref,
                 o_ref, qs, m_s, l_s, acc_s):
    qi = pl.program_id(2)
    tq = q_ref.shape[0]
    q = q_ref[...].astype(F32)
    lane = lax.broadcasted_iota(jnp.int32, q.shape, 1)
    qs[0] = jnp.where(lane < DA_QK, q, 0.0).astype(BF16)
    qs[1] = jnp.where(lane >= DA_QK, q, 0.0).astype(BF16)
    m_s[...] = jnp.full(m_s.shape, -jnp.inf, F32)
    l_s[...] = jnp.zeros(l_s.shape, F32)
    acc_s[...] = jnp.zeros(acc_s.shape, F32)
    nt = (((1,), (1,)), ((), ()))

    def step(kv_start, diagonal):
        k = k_ref[pl.ds(kv_start, tq), :]
        v = v_ref[pl.ds(kv_start, tq), :]
        for c in range(2):
            s = lax.dot_general(qs[c], k, nt, preferred_element_type=F32)
            if diagonal:
                row = lax.broadcasted_iota(jnp.int32, s.shape, 0)
                col = lax.broadcasted_iota(jnp.int32, s.shape, 1)
                s = jnp.where(col <= row, s, -jnp.inf)
            m_prev = m_s[c]
            m_new = jnp.maximum(m_prev, jnp.max(s, axis=-1, keepdims=True))
            alpha = jnp.exp(m_prev - m_new)
            p = jnp.exp(s - m_new)
            l_s[c] = alpha * l_s[c] + jnp.sum(p, axis=-1, keepdims=True)
            acc_s[c] = alpha * acc_s[c] + jnp.dot(p.astype(BF16), v, preferred_element_type=F32)
            m_s[c] = m_new

    def body(kk, carry):
        step(pl.multiple_of(kk * tq, tq), False)
        return carry

    lax.fori_loop(0, qi, body, 0)
    step(pl.multiple_of(qi * tq, tq), True)

    lam = (jnp.exp(jnp.sum(lq1_ref[...] * lk1_ref[...], axis=-1, keepdims=True))
           - jnp.exp(jnp.sum(lq2_ref[...] * lk2_ref[...], axis=-1, keepdims=True)) + lam_init)
    o = acc_s[0] / l_s[0] - lam * (acc_s[1] / l_s[1])
    o = _rms(o) * sw_ref[...] * (1.0 - lam_init)
    o_ref[...] = (o * _sigmoid(ga_ref[...].astype(F32))).astype(o_ref.dtype)


def _attention(qk, plain, lam_vecs, subln_w, lam_init, B, S, tq):
    T = qk.shape[0]
    nq = S // tq
    vec = pl.BlockSpec((1, DA_QK), lambda b, h, i: (0, 0))
    return pl.pallas_call(
        functools.partial(_attn_kernel, lam_init),
        grid=(B, HEADS, nq),
        in_specs=[pl.BlockSpec((tq, LANES), lambda b, h, i: (b * nq + i, h)),
                  pl.BlockSpec((S, LANES), lambda b, h, i: (b, HEADS + h)),
                  pl.BlockSpec((S, LANES), lambda b, h, i: (b, h)),
                  pl.BlockSpec((tq, LANES), lambda b, h, i: (b * nq + i, 3 * HEADS + h)),
                  vec, vec, vec, vec,
                  pl.BlockSpec((1, HEAD_V), lambda b, h, i: (0, 0))],
        out_specs=pl.BlockSpec((tq, LANES), lambda b, h, i: (b * nq + i, h)),
        out_shape=jax.ShapeDtypeStruct((T, D_MODEL), BF16),
        scratch_shapes=[pltpu.VMEM((2, tq, LANES), BF16),
                        pltpu.VMEM((2, tq, 1), F32),
                        pltpu.VMEM((2, tq, 1), F32),
                        pltpu.VMEM((2, tq, HEAD_V), F32)],
        compiler_params=_cparams(("parallel", "parallel", "parallel")),
        name="attn",
    )(qk, qk, plain, plain, *[v.reshape(1, DA_QK) for v in lam_vecs], subln_w.reshape(1, HEAD_V))


def _mlstm_kernel(qk_ref, v_ref, om_ref, gm_ref, gi_ref, gf_ref, git_ref, gft_ref, nw_ref,
                  o_ref, c_s, n_s, m_s):
    L = qk_ref.shape[0]

    @pl.when(pl.program_id(1) == 0)
    def _():
        c_s[...] = jnp.zeros(c_s.shape, F32)
        n_s[...] = jnp.zeros(n_s.shape, F32)
        m_s[...] = jnp.zeros(m_s.shape, F32)

    row = lax.broadcasted_iota(jnp.int32, (L, L), 0)
    col = lax.broadcasted_iota(jnp.int32, (L, L), 1)
    causal = col <= row
    tri = causal.astype(F32)
    tri_t = (row <= col).astype(F32)
    bcols = jnp.dot(tri, _log_sigmoid(gf_ref[...]), precision=HIGHEST, preferred_element_type=F32)
    brows = jnp.dot(_log_sigmoid(gft_ref[...]), tri_t, precision=HIGHEST, preferred_element_type=F32)
    gi = gi_ref[...]
    git = git_ref[...]
    lane = lax.broadcasted_iota(jnp.int32, (1, LANES), 1)
    nt = (((1,), (1,)), ((), ()))
    tn = (((0,), (0,)), ((), ()))

    for h in range(HEADS):
        p = h // 2
        hmask = ((lane >= (h % 2) * ML_QK) & (lane < (h % 2 + 1) * ML_QK)).astype(F32)
        qh = (qk_ref[:, p * LANES:(p + 1) * LANES].astype(F32) * hmask).astype(BF16)
        kp = qk_ref[:, (HEADS // 2 + p) * LANES:(HEADS // 2 + p + 1) * LANES]
        vh = v_ref[:, h * HEAD_V:(h + 1) * HEAD_V]
        bc = bcols[:, h:h + 1]
        igc = gi[:, h:h + 1]
        br = brows[h:h + 1, :]
        igr = git[h:h + 1, :]
        g_tot = br[:, L - 1:L]
        m_prev = m_s[h]
        n_prev = n_s[h]

        a = g_tot - bc + igc
        m_loc = jnp.max(a, axis=0, keepdims=True)
        w_loc = jnp.exp(a - m_loc)

        dm = jnp.where(causal, bc - br + igr, -jnp.inf)
        m_inter = bc + m_prev
        m_j = jnp.maximum(jnp.max(dm, axis=-1, keepdims=True), m_inter)
        s = lax.dot_general(qh, kp, nt, preferred_element_type=F32)
        qkw = s * jnp.exp(dm - m_j)
        inter_w = jnp.exp(m_inter - m_j)
        num = (jnp.dot(qkw.astype(BF16), vh, preferred_element_type=F32)
               + inter_w * jnp.dot(qh, c_s[h].astype(BF16), preferred_element_type=F32))
        den = (jnp.sum(qkw, axis=-1, keepdims=True)
               + inter_w * jnp.sum(qh.astype(F32) * n_prev, axis=-1, keepdims=True))
        hh = num / jnp.maximum(jnp.abs(den), jnp.exp(-m_j))

        m_new = jnp.maximum(g_tot + m_prev, m_loc)
        dec = jnp.exp(g_tot + m_prev - m_new)
        inc = jnp.exp(m_loc - m_new)
        wk = w_loc * kp.astype(F32)
        c_loc = lax.dot_general(wk.astype(BF16), vh, tn, preferred_element_type=F32)
        c_s[h] = dec * c_s[h] + inc * c_loc
        n_s[h] = dec * n_prev + inc * jnp.sum(wk, axis=0, keepdims=True)
        m_s[h] = m_new

        sl = slice(h * HEAD_V, (h + 1) * HEAD_V)
        y = _rms(hh) * nw_ref[:, sl]
        y = y * _sigmoid(om_ref[:, sl].astype(F32)) * _sigmoid(gm_ref[:, sl].astype(F32))
        o_ref[:, sl] = y.astype(o_ref.dtype)


def _mlstm(qkc, plain, gi, gf, git, gft, ml_norm_w, B, S):
    T, D = qkc.shape
    L = ML_CHUNK
    nc = S // L
    wide = lambda cb: pl.BlockSpec((L, D), lambda b, c: (b * nc + c, cb))
    return pl.pallas_call(
        _mlstm_kernel,
        grid=(B, nc),
        in_specs=[wide(0), wide(1), wide(2), wide(4),
                  pl.BlockSpec((L, HEADS), lambda b, c: (b * nc + c, 0)),
                  pl.BlockSpec((L, HEADS), lambda b, c: (b * nc + c, 0)),
                  pl.BlockSpec((HEADS, L), lambda b, c: (0, b * nc + c)),
                  pl.BlockSpec((HEADS, L), lambda b, c: (0, b * nc + c)),
                  pl.BlockSpec((1, D), lambda b, c: (0, 0))],
        out_specs=pl.BlockSpec((L, D), lambda b, c: (b * nc + c, 0)),
        out_shape=jax.ShapeDtypeStruct((T, D), BF16),
        scratch_shapes=[pltpu.VMEM((HEADS, LANES, HEAD_V), F32),
                        pltpu.VMEM((HEADS, 1, LANES), F32),
                        pltpu.VMEM((HEADS, 1, 1), F32)],
        compiler_params=_cparams(("parallel", "arbitrary")),
        name="mlstm",
    )(qkc, plain, plain, plain, gi, gf, git, gft, ml_norm_w.reshape(1, D))


def _outproj_kernel(ya_ref, ym_ref, x_ref, w_ref, gt_ref, nw_ref, sc_ref, sh_ref, x1_ref, h2t_ref):
    merged = (ya_ref[...].astype(F32) + ym_ref[...].astype(F32)).astype(BF16)
    x1 = x_ref[...] + gt_ref[0] * jnp.dot(merged, w_ref[...], preferred_element_type=F32)
    x1_ref[...] = x1
    h2 = _rms(x1) * nw_ref[...] * (1.0 + sc_ref[0]) + sh_ref[0]
    h2t_ref[...] = h2.T.astype(h2t_ref.dtype)


def _outproj(ya, ym, x2, w_out, norm2_w, mod3, B, tps, tm):
    T, D = x2.shape
    row = lambda k: pl.BlockSpec((1, 1, D), lambda i: (k * B + i // tps, 0, 0))
    tile = pl.BlockSpec((tm, D), lambda i: (i, 0))
    return pl.pallas_call(
        _outproj_kernel,
        grid=(T // tm,),
        in_specs=[tile, tile, tile, pl.BlockSpec((D, D), lambda i: (0, 0)), row(2),
                  pl.BlockSpec((1, D), lambda i: (0, 0)), row(4), row(3)],
        out_specs=[tile, pl.BlockSpec((D, tm), lambda i: (0, i))],
        out_shape=[jax.ShapeDtypeStruct((T, D), F32), jax.ShapeDtypeStruct((D, T), BF16)],
        compiler_params=_cparams(("parallel",)),
        name="outproj",
    )(ya, ym, x2, w_out, mod3, norm2_w.reshape(1, D), mod3, mod3)


def _topk_ranks(s):
    n = s.shape[0]
    it = lax.broadcasted_iota(jnp.int32, s.shape, 0)
    rank = jnp.full(s.shape, float(TOPK), F32)
    x = s
    vals = []
    for r in range(TOPK):
        m = jnp.max(x, axis=0, keepdims=True)
        first = jnp.min(jnp.where(x == m, it, n), axis=0, keepdims=True)
        hit = it == first
        x = jnp.where(hit, -jnp.inf, x)
        rank = jnp.where(hit, float(r), rank)
        vals.append(m)
    return rank, vals


def _peer_sel_kernel(h2t_ref, wqt_ref, keys_ref, r2_ref, e2_ref, a_ref, c_ref):
    ht = h2t_ref[...]
    scores = []
    for p in range(2):
        qt = jnp.dot(wqt_ref[p * LANES:(p + 1) * LANES, :], ht, preferred_element_type=F32)
        scores.append(jnp.dot(keys_ref[p], qt.astype(BF16), preferred_element_type=F32))
    s1, s2 = scores
    rank1, v1 = _topk_ranks(s1)
    rank2, v2 = _topk_ranks(s2)

    tt = s1.shape[1]
    rows = [v1[r1] + v2[r2] for (r1, r2) in _CAND]
    rows += [jnp.full((1, tt), -jnp.inf, F32)] * (_CAND_ROWS - len(_CAND))
    cand = jnp.concatenate(rows, axis=0)
    it = lax.broadcasted_iota(jnp.int32, cand.shape, 0)
    sel = jnp.zeros(cand.shape, F32)
    top = v1[0] + v2[0]
    z = jnp.zeros((1, tt), F32)
    x = cand
    for _ in range(TOPK):
        m = jnp.max(x, axis=0, keepdims=True)
        first = jnp.min(jnp.where(x == m, it, _CAND_ROWS), axis=0, keepdims=True)
        hit = it == first
        x = jnp.where(hit, -jnp.inf, x)
        sel = jnp.where(hit, 1.0, sel)
        z = z + jnp.exp(m - top)
    a = jnp.zeros(s1.shape, F32)
    start = 0
    for r1 in range(TOPK):
        width = sum(1 for c in _CAND if c[0] == r1)
        cnt = jnp.sum(sel[start:start + width, :], axis=0, keepdims=True)
        a = jnp.where(rank1 == float(r1), cnt, a)
        start += width

    r2_ref[...] = rank2
    a_ref[...] = a
    e2_ref[...] = jnp.exp(s2 - v2[0])
    c_ref[...] = jnp.exp(s1 - v1[0]) / z


def _peer_sel(h2t, wqt, keys, tt):
    D, T = h2t.shape
    out = pl.BlockSpec((N_KEYS, tt), lambda i, h: (h, i))
    shp = jax.ShapeDtypeStruct((HEADS * N_KEYS, T), F32)
    return pl.pallas_call(
        _peer_sel_kernel,
        grid=(T // tt, HEADS),
        in_specs=[pl.BlockSpec((D, tt), lambda i, h: (0, i)),
                  pl.BlockSpec((2 * LANES, D), lambda i, h: (h, 0)),
                  pl.BlockSpec((2, N_KEYS, LANES), lambda i, h: (h, 0, 0))],
        out_specs=[out, out, out, out],
        out_shape=[shp, shp, shp, shp],
        compiler_params=_cparams(("parallel", "parallel")),
        name="peer_sel",
    )(h2t, wqt, keys)


def _peer_main_kernel(final, h2t_ref, u_ref, vt_ref, r2_ref, e2_ref, a_ref, c_ref, x1_ref, gt_ref, fw_ref,
                      o_ref, acc_ref):
    e = pl.program_id(1)
    eb = u_ref.shape[0]

    @pl.when(e == 0)
    def _():
        acc_ref[...] = jnp.zeros(acc_ref.shape, F32)

    act = jnp.dot(u_ref[...], h2t_ref[...], preferred_element_type=F32)
    parts = []
    for j in range(eb // N_KEYS):
        a_idx = e * (eb // N_KEYS) + j
        g = None
        for h in range(HEADS):
            arow = a_ref[pl.ds(h * N_KEYS + a_idx, 1), :]
            crow = c_ref[pl.ds(h * N_KEYS + a_idx, 1), :]
            hs = slice(h * N_KEYS, (h + 1) * N_KEYS)
            w = jnp.where(r2_ref[hs, :] < arow, e2_ref[hs, :], 0.0) * crow
            g = w if g is None else g + w
        aj = act[j * N_KEYS:(j + 1) * N_KEYS, :]
        gelu = 0.5 * aj * (1.0 + lax.erf(aj * (2.0 ** -0.5)))
        parts.append((g * gelu).astype(BF16))
    pmat = jnp.concatenate(parts, axis=0) if len(parts) > 1 else parts[0]
    acc_ref[...] += jnp.dot(vt_ref[...], pmat, preferred_element_type=F32)

    @pl.when(e == pl.num_programs(1) - 1)
    def _():
        x2 = x1_ref[...] + gt_ref[0] * acc_ref[...].T
        o_ref[...] = _rms(x2) * fw_ref[...] if final else x2


def _peer_main(h2t, u, vt, sel, x1, mod3, fw, final, B, tps, tt, eb):
    D, T = h2t.shape
    ne = u.shape[0]
    wide = pl.BlockSpec((HEADS * N_KEYS, tt), lambda i, e: (0, i))
    return pl.pallas_call(
        functools.partial(_peer_main_kernel, final),
        grid=(T // tt, ne // eb),
        in_specs=[pl.BlockSpec((D, tt), lambda i, e: (0, i)),
                  pl.BlockSpec((eb, D), lambda i, e: (e, 0)),
                  pl.BlockSpec((D, eb), lambda i, e: (0, e)),
                  wide, wide, wide, wide,
                  pl.BlockSpec((tt, D), lambda i, e: (i, 0)),
                  pl.BlockSpec((1, 1, D), lambda i, e: (5 * B + i // tps, 0, 0)),
                  pl.BlockSpec((1, D), lambda i, e: (0, 0))],
        out_specs=pl.BlockSpec((tt, D), lambda i, e: (i, 0)),
        out_shape=jax.ShapeDtypeStruct((T, D), F32),
        scratch_shapes=[pltpu.VMEM((D, tt), F32)],
        compiler_params=_cparams(("parallel", "arbitrary")),
        name="peer_main",
    )(h2t, u, vt, *sel, x1, mod3, fw.reshape(1, D))


def _rope_tables(S):
    inv = ROPE_THETA ** (-jnp.arange(ROPE_HALF, dtype=F32) * 2.0 / ROPE_DIM)
    ang = jnp.arange(S, dtype=jnp.int32).astype(F32)[:, None] * inv[None, :]
    cos, sin = jnp.cos(ang), jnp.sin(ang)
    zeros = jnp.zeros((S, DA_QK - ROPE_DIM), F32)
    z8 = jnp.zeros((S, ROPE_HALF), F32)
    cosf = jnp.concatenate([cos, cos, zeros + 1.0] * 2, axis=1)
    sa = jnp.concatenate([-sin, z8, zeros] * 2, axis=1)
    sb = jnp.concatenate([z8, sin, zeros] * 2, axis=1)
    return cosf, sa, sb


def kernel(x, c, ada_w, ada_b, norm1_w, norm2_w, w_in, conv_w, conv_b, ml_i_bias, ml_f_bias, ml_norm_w, lam_q1, lam_k1, lam_q2, lam_k2, subln_w, w_out, peer_wq, peer_keys, peer_u, peer_v, final_norm_w):
    B, S, D = x.shape
    assert D == D_MODEL and S % 512 == 0
    T = B * S
    depth = w_in.shape[0]
    tm = 512
    tps = S // tm
    tables = _rope_tables(S)
    x2 = x.reshape(T, D)
    for l in range(depth):
        mod3 = _mod(c, ada_w[l], ada_b[l])
        h = _norm(x2, norm1_w[l], mod3, B, tps, tm)

        wl = w_in[l]
        o = 0
        cols = {}
        for name, width in (("qa", D), ("ka", D), ("va", D), ("qm", D // 2), ("km", D // 2), ("vm", D),
                            ("om", D), ("ip", HEADS), ("fp", HEADS), ("ga", D), ("gm", D)):
            cols[name] = wl[:, o:o + width]
            o += width
        w_rope = jnp.concatenate([cols["qa"], cols["ka"]], axis=1).astype(BF16)
        w_conv = jnp.concatenate([cols["qm"], cols["km"]], axis=1).astype(BF16)
        w_plain = jnp.concatenate([cols[n] for n in ("va", "vm", "om", "ga", "gm")], axis=1).astype(BF16)

        qk = _rope_proj(h, w_rope, tables, tps, tm)
        plain = _proj(h, w_plain, tm, D)
        qkc, gi, gf, git, gft = _conv_proj(h, w_conv, cols["ip"].astype(BF16), cols["fp"].astype(BF16),
                                           conv_w[l], conv_b[l], ml_i_bias[l], ml_f_bias[l], tps, tm)

        lam_init = 0.8 - 0.6 * math.exp(-0.3 * l)
        ya = _attention(qk, plain, (lam_q1[l], lam_k1[l], lam_q2[l], lam_k2[l]), subln_w[l], lam_init, B, S, tm)
        ym = _mlstm(qkc, plain, gi, gf, git, gft, ml_norm_w[l], B, S)
        x1, h2t = _outproj(ya, ym, x2, w_out[l].astype(BF16), norm2_w[l], mod3, B, tps, tm)

        wqt = peer_wq[l].T.astype(BF16)
        keys = peer_keys[l].reshape(2 * HEADS, N_KEYS, LANES).astype(BF16)
        sel = _peer_sel(h2t, wqt, keys, tm)
        x2 = _peer_main(h2t, peer_u[l].astype(BF16), peer_v[l].T.astype(BF16), sel, x1, mod3,
                        final_norm_w, l == depth - 1, B, tps, tm, 2 * N_KEYS)
    if depth == 0:
        raise ValueError("depth must be positive")
    return x2.reshape(B, S, D)
```

```python
import functools
import math

import jax
import jax.numpy as jnp
from jax import lax
from jax.experimental import pallas as pl
from jax.experimental.pallas import tpu as pltpu

F32 = jnp.float32
BF16 = jnp.bfloat16
HIGHEST = lax.Precision.HIGHEST

NORM_EPS = 1e-6
D_MODEL = 1024
HEADS = 8
HEAD_V = 128
DA_QK = 64
ROPE_DIM = 16
ROPE_HALF = 8
ROPE_THETA = 500000.0
ML_QK = 64
CONV_WIDTH = 4
ML_CHUNK = 128
N_KEYS = 128
TOPK = 16
LANES = 128
SUBLANES = 8
VMEM_LIMIT = 48 * 1024 * 1024

_CAND = [(r1, r2) for r1 in range(TOPK) for r2 in range(TOPK) if (r1 + 1) * (r2 + 1) <= TOPK]
_CAND_ROWS = -(-len(_CAND) // SUBLANES) * SUBLANES


def _cparams(sem):
    return pltpu.CompilerParams(dimension_semantics=sem, vmem_limit_bytes=VMEM_LIMIT)


def _rms(x):
    return x * lax.rsqrt(jnp.mean(x * x, axis=-1, keepdims=True) + NORM_EPS)


def _sigmoid(x):
    return 1.0 / (1.0 + jnp.exp(-x))


def _log_sigmoid(x):
    return jnp.minimum(x, 0.0) - jnp.log(1.0 + jnp.exp(-jnp.abs(x)))


def _mod_kernel(c_ref, w_ref, b_ref, o_ref):
    c = c_ref[...]
    cond = c * _sigmoid(c)
    o_ref[0] = jnp.dot(cond, w_ref[...], precision=HIGHEST, preferred_element_type=F32) + b_ref[...]


def _mod(c, ada_w, ada_b):
    B, D = c.shape
    out = pl.pallas_call(
        _mod_kernel,
        grid=(6,),
        in_specs=[pl.BlockSpec((B, D), lambda j: (0, 0)),
                  pl.BlockSpec((D, D), lambda j: (0, j)),
                  pl.BlockSpec((1, D), lambda j: (0, j))],
        out_specs=pl.BlockSpec((1, B, D), lambda j: (j, 0, 0)),
        out_shape=jax.ShapeDtypeStruct((6, B, D), F32),
        compiler_params=_cparams(("parallel",)),
        name="mod",
    )(c, ada_w, ada_b.reshape(1, 6 * D))
    return out.reshape(6 * B, 1, D)


def _norm_kernel(x_ref, w_ref, sc_ref, sh_ref, o_ref):
    y = _rms(x_ref[...]) * w_ref[...]
    o_ref[...] = (y * (1.0 + sc_ref[0]) + sh_ref[0]).astype(o_ref.dtype)


def _norm(x2, w, mod3, B, tps, tm):
    T, D = x2.shape
    return pl.pallas_call(
        _norm_kernel,
        grid=(T // tm,),
        in_specs=[pl.BlockSpec((tm, D), lambda i: (i, 0)),
                  pl.BlockSpec((1, D), lambda i: (0, 0)),
                  pl.BlockSpec((1, 1, D), lambda i: (1 * B + i // tps, 0, 0)),
                  pl.BlockSpec((1, 1, D), lambda i: (0 * B + i // tps, 0, 0))],
        out_specs=pl.BlockSpec((tm, D), lambda i: (i, 0)),
        out_shape=jax.ShapeDtypeStruct((T, D), BF16),
        compiler_params=_cparams(("parallel",)),
        name="norm1",
    )(x2, w.reshape(1, D), mod3, mod3)


def _proj_kernel(h_ref, w_ref, o_ref):
    o_ref[...] = jnp.dot(h_ref[...], w_ref[...], preferred_element_type=F32).astype(o_ref.dtype)


def _proj(h, w, tm, tn):
    T, D = h.shape
    N = w.shape[1]
    return pl.pallas_call(
        _proj_kernel,
        grid=(N // tn, T // tm),
        in_specs=[pl.BlockSpec((tm, D), lambda j, i: (i, 0)),
                  pl.BlockSpec((D, tn), lambda j, i: (0, j))],
        out_specs=pl.BlockSpec((tm, tn), lambda j, i: (i, j)),
        out_shape=jax.ShapeDtypeStruct((T, N), BF16),
        compiler_params=_cparams(("parallel", "parallel")),
        name="proj_plain",
    )(h, w)


def _rope_proj_kernel(h_ref, w_ref, cos_ref, sa_ref, sb_ref, o_ref):
    acc = jnp.dot(h_ref[...], w_ref[...], preferred_element_type=F32)
    scale = jnp.where(pl.program_id(0) == 0, DA_QK ** -0.5, 1.0).astype(F32)
    cosf, sa, sb = cos_ref[...], sa_ref[...], sb_ref[...]
    for hh in range(HEADS):
        blk = acc[:, hh * LANES:(hh + 1) * LANES]
        rot = (blk * cosf + pltpu.roll(blk, LANES - ROPE_HALF, 1) * sa
               + pltpu.roll(blk, ROPE_HALF, 1) * sb)
        o_ref[:, hh * LANES:(hh + 1) * LANES] = (rot * scale).astype(o_ref.dtype)


def _rope_proj(h, w, tables, tps, tm):
    T, D = h.shape
    cosf, sa, sb = tables
    tab_spec = pl.BlockSpec((tm, LANES), lambda j, i: (i % tps, 0))
    return pl.pallas_call(
        _rope_proj_kernel,
        grid=(2, T // tm),
        in_specs=[pl.BlockSpec((tm, D), lambda j, i: (i, 0)),
                  pl.BlockSpec((D, D), lambda j, i: (0, j)),
                  tab_spec, tab_spec, tab_spec],
        out_specs=pl.BlockSpec((tm, D), lambda j, i: (i, j)),
        out_shape=jax.ShapeDtypeStruct((T, 2 * D), BF16),
        compiler_params=_cparams(("parallel", "parallel")),
        name="proj_rope",
    )(h, w, cosf, sa, sb)


def _conv_proj_kernel(tps, h_ref, w_ref, wgi_ref, wgf_ref, wgit_ref, wgft_ref, cw_ref, cb_ref,
                      bi_ref, bf_ref, bit_ref, bft_ref,
                      o_ref, gi_ref, gf_ref, git_ref, gft_ref, buf):
    i = pl.program_id(0)
    tm = h_ref.shape[0]
    h = h_ref[...]
    acc = jnp.dot(h, w_ref[...], preferred_element_type=F32)

    @pl.when(i % tps == 0)
    def _():
        buf[0:SUBLANES, :] = jnp.zeros((SUBLANES, acc.shape[1]), F32)

    buf[SUBLANES:SUBLANES + tm, :] = acc
    cw = cw_ref[...]
    y = cb_ref[...]
    for j in range(CONV_WIDTH):
        off = SUBLANES - (CONV_WIDTH - 1) + j
        y = y + buf[off:off + tm, :] * cw[j:j + 1, :]
    buf[0:SUBLANES, :] = buf[tm:tm + SUBLANES, :]
    y = y * _sigmoid(y)
    lane = lax.broadcasted_iota(jnp.int32, (1, y.shape[1]), 1)
    kscale = jnp.where(lane >= HEADS * ML_QK, ML_QK ** -0.5, 1.0).astype(F32)
    o_ref[...] = (y * kscale).astype(o_ref.dtype)

    gi_ref[...] = jnp.dot(h, wgi_ref[...], preferred_element_type=F32)[:, :HEADS] + bi_ref[...]
    gf_ref[...] = jnp.dot(h, wgf_ref[...], preferred_element_type=F32)[:, :HEADS] + bf_ref[...]
    nt = (((1,), (1,)), ((), ()))
    git_ref[...] = lax.dot_general(wgit_ref[...], h, nt, preferred_element_type=F32) + bit_ref[...]
    gft_ref[...] = lax.dot_general(wgft_ref[...], h, nt, preferred_element_type=F32) + bft_ref[...]


def _conv_proj(h, w, wgi, wgf, conv_w, conv_b, bi, bf, tps, tm):
    T, D = h.shape
    N = w.shape[1]
    pad = lambda a: jnp.pad(a, ((0, 0), (0, LANES - a.shape[1])))
    full = lambda shape: pl.BlockSpec(shape, lambda i: tuple(0 for _ in shape))
    return pl.pallas_call(
        functools.partial(_conv_proj_kernel, tps),
        grid=(T // tm,),
        in_specs=[pl.BlockSpec((tm, D), lambda i: (i, 0)),
                  full((D, N)), full((D, LANES)), full((D, LANES)),
                  full((HEADS, D)), full((HEADS, D)),
                  full((CONV_WIDTH, N)), full((1, N)),
                  full((1, HEADS)), full((1, HEADS)), full((HEADS, 1)), full((HEADS, 1))],
        out_specs=[pl.BlockSpec((tm, N), lambda i: (i, 0)),
                   pl.BlockSpec((tm, HEADS), lambda i: (i, 0)),
                   pl.BlockSpec((tm, HEADS), lambda i: (i, 0)),
                   pl.BlockSpec((HEADS, tm), lambda i: (0, i)),
                   pl.BlockSpec((HEADS, tm), lambda i: (0, i))],
        out_shape=[jax.ShapeDtypeStruct((T, N), BF16),
                   jax.ShapeDtypeStruct((T, HEADS), F32),
                   jax.ShapeDtypeStruct((T, HEADS), F32),
                   jax.ShapeDtypeStruct((HEADS, T), F32),
                   jax.ShapeDtypeStruct((HEADS, T), F32)],
        scratch_shapes=[pltpu.VMEM((tm + 2 * SUBLANES, N), F32)],
        compiler_params=_cparams(("arbitrary",)),
        name="proj_conv",
    )(h, w, pad(wgi), pad(wgf), wgi.T, wgf.T, conv_w, conv_b.reshape(1, N),
      bi.reshape(1, HEADS), bf.reshape(1, HEADS), bi.reshape(HEADS, 1), bf.reshape(HEADS, 1))


def _attn_kernel(lam_init, q_ref, k_ref, v_ref, ga_ref, lq1_ref, lk1_ref, lq2_ref, lk2_ref, sw_ref,
                 o_ref, qt_s, vt_s, m_s, l_s, acc_s):
    qi = pl.program_id(2)
    tq = q_ref.shape[0]

    @pl.when(qi == 0)
    def _():
        for kk in range(vt_s.shape[0]):
            vt_s[kk] = v_ref[kk * tq:(kk + 1) * tq, :].T

    q = q_ref[...].astype(F32)
    lane = lax.broadcasted_iota(jnp.int32, q.shape, 1)
    qt_s[0] = jnp.where(lane < DA_QK, q, 0.0).T.astype(BF16)
    qt_s[1] = jnp.where(lane >= DA_QK, q, 0.0).T.astype(BF16)
    m_s[...] = jnp.full(m_s.shape, -jnp.inf, F32)
    l_s[...] = jnp.zeros(l_s.shape, F32)
    acc_s[...] = jnp.zeros(acc_s.shape, F32)

    def step(kk, diagonal):
        k = k_ref[pl.ds(pl.multiple_of(kk * tq, tq), tq), :]
        vt = vt_s[kk]
        for c in range(2):
            st = jnp.dot(k, qt_s[c], preferred_element_type=F32)
            if diagonal:
                key = lax.broadcasted_iota(jnp.int32, st.shape, 0)
                qry = lax.broadcasted_iota(jnp.int32, st.shape, 1)
                st = jnp.where(key <= qry, st, -jnp.inf)
            m_prev = m_s[c]
            m_new = jnp.maximum(m_prev, jnp.max(st, axis=0, keepdims=True))
            alpha = jnp.exp(m_prev - m_new)
            p = jnp.exp(st - m_new)
            l_s[c] = alpha * l_s[c] + jnp.sum(p, axis=0, keepdims=True)
            acc_s[c] = alpha * acc_s[c] + jnp.dot(vt, p.astype(BF16), preferred_element_type=F32)
            m_s[c] = m_new

    def body(kk, carry):
        step(kk, False)
        return carry

    lax.fori_loop(0, qi, body, 0)
    step(qi, True)

    lam = (jnp.exp(jnp.sum(lq1_ref[...] * lk1_ref[...], axis=-1, keepdims=True))
           - jnp.exp(jnp.sum(lq2_ref[...] * lk2_ref[...], axis=-1, keepdims=True)) + lam_init)
    o = (acc_s[0] / l_s[0] - lam * (acc_s[1] / l_s[1])).T
    o = _rms(o) * sw_ref[...] * (1.0 - lam_init)
    o_ref[...] = (o * _sigmoid(ga_ref[...].astype(F32))).astype(o_ref.dtype)


def _attention(qk, plain, lam_vecs, subln_w, lam_init, B, S, tq):
    T = qk.shape[0]
    nq = S // tq
    vec = pl.BlockSpec((1, DA_QK), lambda b, h, i: (0, 0))
    return pl.pallas_call(
        functools.partial(_attn_kernel, lam_init),
        grid=(B, HEADS, nq),
        in_specs=[pl.BlockSpec((tq, LANES), lambda b, h, i: (b * nq + i, h)),
                  pl.BlockSpec((S, LANES), lambda b, h, i: (b, HEADS + h)),
                  pl.BlockSpec((S, LANES), lambda b, h, i: (b, h)),
                  pl.BlockSpec((tq, LANES), lambda b, h, i: (b * nq + i, 3 * HEADS + h)),
                  vec, vec, vec, vec,
                  pl.BlockSpec((1, HEAD_V), lambda b, h, i: (0, 0))],
        out_specs=pl.BlockSpec((tq, LANES), lambda b, h, i: (b * nq + i, h)),
        out_shape=jax.ShapeDtypeStruct((T, D_MODEL), BF16),
        scratch_shapes=[pltpu.VMEM((2, LANES, tq), BF16),
                        pltpu.VMEM((nq, HEAD_V, tq), BF16),
                        pltpu.VMEM((2, 1, tq), F32),
                        pltpu.VMEM((2, 1, tq), F32),
                        pltpu.VMEM((2, HEAD_V, tq), F32)],
        compiler_params=_cparams(("parallel", "parallel", "arbitrary")),
        name="attn",
    )(qk, qk, plain, plain, *[v.reshape(1, DA_QK) for v in lam_vecs], subln_w.reshape(1, HEAD_V))


def _mlstm_kernel(qk_ref, v_ref, om_ref, gm_ref, gi_ref, gf_ref, git_ref, gft_ref, nw_ref,
                  o_ref, c_s, n_s, m_s):
    L = qk_ref.shape[0]

    @pl.when(pl.program_id(1) == 0)
    def _():
        c_s[...] = jnp.zeros(c_s.shape, F32)
        n_s[...] = jnp.zeros(n_s.shape, F32)
        m_s[...] = jnp.zeros(m_s.shape, F32)

    row = lax.broadcasted_iota(jnp.int32, (L, L), 0)
    col = lax.broadcasted_iota(jnp.int32, (L, L), 1)
    causal = col <= row
    tri = causal.astype(F32)
    tri_t = (row <= col).astype(F32)
    bcols = jnp.dot(tri, _log_sigmoid(gf_ref[...]), precision=HIGHEST, preferred_element_type=F32)
    brows = jnp.dot(_log_sigmoid(gft_ref[...]), tri_t, precision=HIGHEST, preferred_element_type=F32)
    gi = gi_ref[...]
    git = git_ref[...]
    lane = lax.broadcasted_iota(jnp.int32, (1, LANES), 1)
    nt = (((1,), (1,)), ((), ()))
    tn = (((0,), (0,)), ((), ()))

    for h in range(HEADS):
        p = h // 2
        hmask = ((lane >= (h % 2) * ML_QK) & (lane < (h % 2 + 1) * ML_QK)).astype(F32)
        qh = (qk_ref[:, p * LANES:(p + 1) * LANES].astype(F32) * hmask).astype(BF16)
        kp = qk_ref[:, (HEADS // 2 + p) * LANES:(HEADS // 2 + p + 1) * LANES]
        vh = v_ref[:, h * HEAD_V:(h + 1) * HEAD_V]
        bc = bcols[:, h:h + 1]
        igc = gi[:, h:h + 1]
        br = brows[h:h + 1, :]
        igr = git[h:h + 1, :]
        g_tot = br[:, L - 1:L]
        m_prev = m_s[h]
        n_prev = n_s[h]

        a = g_tot - bc + igc
        m_loc = jnp.max(a, axis=0, keepdims=True)
        w_loc = jnp.exp(a - m_loc)

        dm = jnp.where(causal, bc - br + igr, -jnp.inf)
        m_inter = bc + m_prev
        m_j = jnp.maximum(jnp.max(dm, axis=-1, keepdims=True), m_inter)
        s = lax.dot_general(qh, kp, nt, preferred_element_type=F32)
        qkw = s * jnp.exp(dm - m_j)
        inter_w = jnp.exp(m_inter - m_j)
        num = (jnp.dot(qkw.astype(BF16), vh, preferred_element_type=F32)
               + inter_w * jnp.dot(qh, c_s[h].astype(BF16), preferred_element_type=F32))
        den = (jnp.sum(qkw, axis=-1, keepdims=True)
               + inter_w * jnp.sum(qh.astype(F32) * n_prev, axis=-1, keepdims=True))
        hh = num / jnp.maximum(jnp.abs(den), jnp.exp(-m_j))

        m_new = jnp.maximum(g_tot + m_prev, m_loc)
        dec = jnp.exp(g_tot + m_prev - m_new)
        inc = jnp.exp(m_loc - m_new)
        wk = w_loc * kp.astype(F32)
        c_loc = lax.dot_general(wk.astype(BF16), vh, tn, preferred_element_type=F32)
        c_s[h] = dec * c_s[h] + inc * c_loc
        n_s[h] = dec * n_prev + inc * jnp.sum(wk, axis=0, keepdims=True)
        m_s[h] = m_new

        sl = slice(h * HEAD_V, (h + 1) * HEAD_V)
        y = _rms(hh) * nw_ref[:, sl]
        y = y * _sigmoid(om_ref[:, sl].astype(F32)) * _sigmoid(gm_ref[:, sl].astype(F32))
        o_ref[:, sl] = y.astype(o_ref.dtype)


def _mlstm(qkc, plain, gi, gf, git, gft, ml_norm_w, B, S):
    T, D = qkc.shape
    L = ML_CHUNK
    nc = S // L
    wide = lambda cb: pl.BlockSpec((L, D), lambda b, c: (b * nc + c, cb))
    return pl.pallas_call(
        _mlstm_kernel,
        grid=(B, nc),
        in_specs=[wide(0), wide(1), wide(2), wide(4),
                  pl.BlockSpec((L, HEADS), lambda b, c: (b * nc + c, 0)),
                  pl.BlockSpec((L, HEADS), lambda b, c: (b * nc + c, 0)),
                  pl.BlockSpec((HEADS, L), lambda b, c: (0, b * nc + c)),
                  pl.BlockSpec((HEADS, L), lambda b, c: (0, b * nc + c)),
                  pl.BlockSpec((1, D), lambda b, c: (0, 0))],
        out_specs=pl.BlockSpec((L, D), lambda b, c: (b * nc + c, 0)),
        out_shape=jax.ShapeDtypeStruct((T, D), BF16),
        scratch_shapes=[pltpu.VMEM((HEADS, LANES, HEAD_V), F32),
                        pltpu.VMEM((HEADS, 1, LANES), F32),
                        pltpu.VMEM((HEADS, 1, 1), F32)],
        compiler_params=_cparams(("parallel", "arbitrary")),
        name="mlstm",
    )(qkc, plain, plain, plain, gi, gf, git, gft, ml_norm_w.reshape(1, D))


def _outproj_kernel(ya_ref, ym_ref, x_ref, w_ref, gt_ref, nw_ref, sc_ref, sh_ref, x1_ref, h2t_ref):
    merged = (ya_ref[...].astype(F32) + ym_ref[...].astype(F32)).astype(BF16)
    x1 = x_ref[...] + gt_ref[0] * jnp.dot(merged, w_ref[...], preferred_element_type=F32)
    x1_ref[...] = x1
    h2 = _rms(x1) * nw_ref[...] * (1.0 + sc_ref[0]) + sh_ref[0]
    h2t_ref[...] = h2.T.astype(h2t_ref.dtype)


def _outproj(ya, ym, x2, w_out, norm2_w, mod3, B, tps, tm):
    T, D = x2.shape
    row = lambda k: pl.BlockSpec((1, 1, D), lambda i: (k * B + i // tps, 0, 0))
    tile = pl.BlockSpec((tm, D), lambda i: (i, 0))
    return pl.pallas_call(
        _outproj_kernel,
        grid=(T // tm,),
        in_specs=[tile, tile, tile, pl.BlockSpec((D, D), lambda i: (0, 0)), row(2),
                  pl.BlockSpec((1, D), lambda i: (0, 0)), row(4), row(3)],
        out_specs=[tile, pl.BlockSpec((D, tm), lambda i: (0, i))],
        out_shape=[jax.ShapeDtypeStruct((T, D), F32), jax.ShapeDtypeStruct((D, T), BF16)],
        compiler_params=_cparams(("parallel",)),
        name="outproj",
    )(ya, ym, x2, w_out, mod3, norm2_w.reshape(1, D), mod3, mod3)


def _topk_ranks(s):
    n = s.shape[0]
    it = lax.broadcasted_iota(jnp.int32, s.shape, 0)
    rank = jnp.full(s.shape, float(TOPK), F32)
    x = s
    vals = []
    for r in range(TOPK):
        m = jnp.max(x, axis=0, keepdims=True)
        first = jnp.min(jnp.where(x == m, it, n), axis=0, keepdims=True)
        hit = it == first
        x = jnp.where(hit, -jnp.inf, x)
        rank = jnp.where(hit, float(r), rank)
        vals.append(m)
    return rank, vals


def _peer_sel_kernel(h2t_ref, wqt_ref, keys_ref, r2_ref, e2_ref, a_ref, c_ref):
    ht = h2t_ref[...]
    scores = []
    for p in range(2):
        qt = jnp.dot(wqt_ref[p * LANES:(p + 1) * LANES, :], ht, preferred_element_type=F32)
        scores.append(jnp.dot(keys_ref[p], qt.astype(BF16), preferred_element_type=F32))
    s1, s2 = scores
    rank1, v1 = _topk_ranks(s1)
    rank2, v2 = _topk_ranks(s2)

    tt = s1.shape[1]
    rows = [v1[r1] + v2[r2] for (r1, r2) in _CAND]
    rows += [jnp.full((1, tt), -jnp.inf, F32)] * (_CAND_ROWS - len(_CAND))
    cand = jnp.concatenate(rows, axis=0)
    it = lax.broadcasted_iota(jnp.int32, cand.shape, 0)
    sel = jnp.zeros(cand.shape, F32)
    top = v1[0] + v2[0]
    z = jnp.zeros((1, tt), F32)
    x = cand
    for _ in range(TOPK):
        m = jnp.max(x, axis=0, keepdims=True)
        first = jnp.min(jnp.where(x == m, it, _CAND_ROWS), axis=0, keepdims=True)
        hit = it == first
        x = jnp.where(hit, -jnp.inf, x)
        sel = jnp.where(hit, 1.0, sel)
        z = z + jnp.exp(m - top)
    a = jnp.zeros(s1.shape, F32)
    start = 0
    for r1 in range(TOPK):
        width = sum(1 for c in _CAND if c[0] == r1)
        cnt = jnp.sum(sel[start:start + width, :], axis=0, keepdims=True)
        a = jnp.where(rank1 == float(r1), cnt, a)
        start += width

    for ref, val in ((r2_ref, rank2), (a_ref, a), (e2_ref, jnp.exp(s2 - v2[0])),
                     (c_ref, jnp.exp(s1 - v1[0]) / z)):
        for lt in range(tt // LANES):
            ref[lt] = val[:, lt * LANES:(lt + 1) * LANES].astype(ref.dtype)


def _peer_sel(h2t, wqt, keys, tt):
    D, T = h2t.shape
    out = pl.BlockSpec((tt // LANES, N_KEYS, LANES), lambda i, h: (i, h, 0))
    shp = lambda dt: jax.ShapeDtypeStruct((T // LANES, HEADS * N_KEYS, LANES), dt)
    return pl.pallas_call(
        _peer_sel_kernel,
        grid=(T // tt, HEADS),
        in_specs=[pl.BlockSpec((D, tt), lambda i, h: (0, i)),
                  pl.BlockSpec((2 * LANES, D), lambda i, h: (h, 0)),
                  pl.BlockSpec((2, N_KEYS, LANES), lambda i, h: (h, 0, 0))],
        out_specs=[out, out, out, out],
        out_shape=[shp(BF16), shp(BF16), shp(F32), shp(F32)],
        compiler_params=_cparams(("parallel", "parallel")),
        name="peer_sel",
    )(h2t, wqt, keys)


PEER_ROWS = 64
MXU_COLS = 256


def _peer_main_kernel(final, h2t_ref, u_ref, vt_ref, r2_ref, e2_ref, a_ref, c_ref, x1_ref, gt_ref, fw_ref,
                      o_ref, acc_ref, act_ref, p_ref):
    e = pl.program_id(1)
    eb, tt = act_ref.shape
    nj = eb // N_KEYS

    @pl.when(e == 0)
    def _():
        acc_ref[...] = jnp.zeros(acc_ref.shape, F32)

    n_half = tt // MXU_COLS
    tcols = lambda th: slice(th * MXU_COLS, (th + 1) * MXU_COLS)

    def act_piece(th, m):
        ms = slice(m * N_KEYS, (m + 1) * N_KEYS)
        act_ref[ms, tcols(th)] = jnp.dot(u_ref[ms, :], h2t_ref[:, tcols(th)], preferred_element_type=F32)

    def out_piece(th, r):
        rs = slice(r * MXU_COLS, (r + 1) * MXU_COLS)
        acc_ref[rs, tcols(th)] += jnp.dot(vt_ref[rs, :], p_ref[:, tcols(th)], preferred_element_type=F32)

    def gate_chunk(tl, bs):
        g = [None] * nj
        for h in range(HEADS):
            rs = slice(h * N_KEYS + bs * PEER_ROWS, h * N_KEYS + (bs + 1) * PEER_ROWS)
            r2c = r2_ref[tl, rs, :]
            e2c = e2_ref[tl, rs, :]
            for j in range(nj):
                idx = h * N_KEYS + e * nj + j
                arow = a_ref[tl, pl.ds(idx, 1), :].astype(r2c.dtype)
                crow = c_ref[tl, pl.ds(idx, 1), :].astype(e2c.dtype)
                w = jnp.where(r2c < arow, e2c, jnp.zeros_like(e2c)) * crow
                g[j] = w if g[j] is None else g[j] + w
        ls = slice(tl * LANES, (tl + 1) * LANES)
        for j in range(nj):
            es = slice(j * N_KEYS + bs * PEER_ROWS, j * N_KEYS + (bs + 1) * PEER_ROWS)
            aj = act_ref[es, ls]
            gelu = 0.5 * aj * (1.0 + lax.erf(aj * (2.0 ** -0.5)))
            p_ref[es, ls] = g[j] * gelu.astype(BF16)

    for m in range(nj):
        act_piece(0, m)
    for th in range(n_half):
        mxu_work = [functools.partial(act_piece, th + 1, m) for m in range(nj)] if th + 1 < n_half else []
        if th > 0:
            mxu_work += [functools.partial(out_piece, th - 1, r) for r in range(acc_ref.shape[0] // MXU_COLS)]
        chunks = [(th * (MXU_COLS // LANES) + lt, bs)
                  for lt in range(MXU_COLS // LANES) for bs in range(N_KEYS // PEER_ROWS)]
        per_chunk = -(-len(mxu_work) // len(chunks))
        for tl, bs in chunks:
            gate_chunk(tl, bs)
            for piece in mxu_work[:per_chunk]:
                piece()
            mxu_work = mxu_work[per_chunk:]
    for r in range(acc_ref.shape[0] // MXU_COLS):
        out_piece(n_half - 1, r)

    @pl.when(e == pl.num_programs(1) - 1)
    def _():
        x2 = x1_ref[...] + gt_ref[0] * acc_ref[...].T
        o_ref[...] = _rms(x2) * fw_ref[...] if final else x2


def _peer_main(h2t, u, vt, sel, x1, mod3, fw, final, B, tps, tt, eb):
    D, T = h2t.shape
    ne = u.shape[0]
    wide = pl.BlockSpec((tt // LANES, HEADS * N_KEYS, LANES), lambda i, e: (i, 0, 0))
    return pl.pallas_call(
        functools.partial(_peer_main_kernel, final),
        grid=(T // tt, ne // eb),
        in_specs=[pl.BlockSpec((D, tt), lambda i, e: (0, i)),
                  pl.BlockSpec((eb, D), lambda i, e: (e, 0)),
                  pl.BlockSpec((D, eb), lambda i, e: (0, e)),
                  wide, wide, wide, wide,
                  pl.BlockSpec((tt, D), lambda i, e: (i, 0)),
                  pl.BlockSpec((1, 1, D), lambda i, e: (5 * B + i // tps, 0, 0)),
                  pl.BlockSpec((1, D), lambda i, e: (0, 0))],
        out_specs=pl.BlockSpec((tt, D), lambda i, e: (i, 0)),
        out_shape=jax.ShapeDtypeStruct((T, D), F32),
        scratch_shapes=[pltpu.VMEM((D, tt), F32),
                        pltpu.VMEM((eb, tt), F32),
                        pltpu.VMEM((eb, tt), BF16)],
        compiler_params=_cparams(("parallel", "arbitrary")),
        name="peer_main",
    )(h2t, u, vt, *sel, x1, mod3, fw.reshape(1, D))


def _rope_tables(S):
    inv = ROPE_THETA ** (-jnp.arange(ROPE_HALF, dtype=F32) * 2.0 / ROPE_DIM)
    ang = jnp.arange(S, dtype=jnp.int32).astype(F32)[:, None] * inv[None, :]
    cos, sin = jnp.cos(ang), jnp.sin(ang)
    zeros = jnp.zeros((S, DA_QK - ROPE_DIM), F32)
    z8 = jnp.zeros((S, ROPE_HALF), F32)
    cosf = jnp.concatenate([cos, cos, zeros + 1.0] * 2, axis=1)
    sa = jnp.concatenate([-sin, z8, zeros] * 2, axis=1)
    sb = jnp.concatenate([z8, sin, zeros] * 2, axis=1)
    return cosf, sa, sb


def kernel(x, c, ada_w, ada_b, norm1_w, norm2_w, w_in, conv_w, conv_b, ml_i_bias, ml_f_bias, ml_norm_w, lam_q1, lam_k1, lam_q2, lam_k2, subln_w, w_out, peer_wq, peer_keys, peer_u, peer_v, final_norm_w):
    B, S, D = x.shape
    assert D == D_MODEL and S % 512 == 0
    T = B * S
    depth = w_in.shape[0]
    tm = 512
    tps = S // tm
    tables = _rope_tables(S)
    x2 = x.reshape(T, D)
    for l in range(depth):
        mod3 = _mod(c, ada_w[l], ada_b[l])
        h = _norm(x2, norm1_w[l], mod3, B, tps, tm)

        wl = w_in[l]
        o = 0
        cols = {}
        for name, width in (("qa", D), ("ka", D), ("va", D), ("qm", D // 2), ("km", D // 2), ("vm", D),
                            ("om", D), ("ip", HEADS), ("fp", HEADS), ("ga", D), ("gm", D)):
            cols[name] = wl[:, o:o + width]
            o += width
        w_rope = jnp.concatenate([cols["qa"], cols["ka"]], axis=1).astype(BF16)
        w_conv = jnp.concatenate([cols["qm"], cols["km"]], axis=1).astype(BF16)
        w_plain = jnp.concatenate([cols[n] for n in ("va", "vm", "om", "ga", "gm")], axis=1).astype(BF16)

        qk = _rope_proj(h, w_rope, tables, tps, tm)
        plain = _proj(h, w_plain, tm, D)
        qkc, gi, gf, git, gft = _conv_proj(h, w_conv, cols["ip"].astype(BF16), cols["fp"].astype(BF16),
                                           conv_w[l], conv_b[l], ml_i_bias[l], ml_f_bias[l], tps, tm)

        lam_init = 0.8 - 0.6 * math.exp(-0.3 * l)
        ya = _attention(qk, plain, (lam_q1[l], lam_k1[l], lam_q2[l], lam_k2[l]), subln_w[l], lam_init, B, S, tm)
        ym = _mlstm(qkc, plain, gi, gf, git, gft, ml_norm_w[l], B, S)
        x1, h2t = _outproj(ya, ym, x2, w_out[l].astype(BF16), norm2_w[l], mod3, B, tps, tm)

        wqt = peer_wq[l].T.astype(BF16)
        keys = peer_keys[l].reshape(2 * HEADS, N_KEYS, LANES).astype(BF16)
        sel = _peer_sel(h2t, wqt, keys, tm)
        x2 = _peer_main(h2t, peer_u[l].astype(BF16), peer_v[l].T.astype(BF16), sel, x1, mod3,
                        final_norm_w, l == depth - 1, B, tps, tm, 4 * N_KEYS)
    if depth == 0:
        raise ValueError("depth must be positive")
    return x2.reshape(B, S, D)
```

```python
import functools
import math

import jax
import jax.numpy as jnp
from jax import lax
from jax.experimental import pallas as pl
from jax.experimental.pallas import tpu as pltpu

F32 = jnp.float32
BF16 = jnp.bfloat16
HIGHEST = lax.Precision.HIGHEST

NORM_EPS = 1e-6
LOG2E = 1.4426950408889634
D_MODEL = 1024
HEADS = 8
HEAD_V = 128
DA_QK = 64
ROPE_DIM = 16
ROPE_HALF = 8
ROPE_THETA = 500000.0
ML_QK = 64
CONV_WIDTH = 4
ML_CHUNK = 128
ATTN_TQ = 512
ATTN_TK = 512
N_KEYS = 128
TOPK = 16
LANES = 128
SUBLANES = 8
VMEM_LIMIT = 48 * 1024 * 1024

_CAND = [(r1, r2) for r1 in range(TOPK) for r2 in range(TOPK) if (r1 + 1) * (r2 + 1) <= TOPK]
_CAND_ROWS = -(-len(_CAND) // SUBLANES) * SUBLANES


def _cparams(sem):
    return pltpu.CompilerParams(dimension_semantics=sem, vmem_limit_bytes=VMEM_LIMIT)


def _rms(x):
    return x * lax.rsqrt(jnp.mean(x * x, axis=-1, keepdims=True) + NORM_EPS)


def _sigmoid(x):
    return 1.0 / (1.0 + jnp.exp(-x))


def _log_sigmoid(x):
    return jnp.minimum(x, 0.0) - jnp.log(1.0 + jnp.exp(-jnp.abs(x)))


def _mod_kernel(c_ref, w_ref, b_ref, o_ref):
    c = c_ref[...]
    cond = c * _sigmoid(c)
    o_ref[0] = jnp.dot(cond, w_ref[...], precision=HIGHEST, preferred_element_type=F32) + b_ref[...]


def _mod(c, ada_w, ada_b):
    B, D = c.shape
    out = pl.pallas_call(
        _mod_kernel,
        grid=(6,),
        in_specs=[pl.BlockSpec((B, D), lambda j: (0, 0)),
                  pl.BlockSpec((D, D), lambda j: (0, j)),
                  pl.BlockSpec((1, D), lambda j: (0, j))],
        out_specs=pl.BlockSpec((1, B, D), lambda j: (j, 0, 0)),
        out_shape=jax.ShapeDtypeStruct((6, B, D), F32),
        compiler_params=_cparams(("parallel",)),
        name="mod",
    )(c, ada_w, ada_b.reshape(1, 6 * D))
    return out.reshape(6 * B, 1, D)


def _norm_kernel(x_ref, w_ref, sc_ref, sh_ref, o_ref):
    y = _rms(x_ref[...]) * w_ref[...]
    o_ref[...] = (y * (1.0 + sc_ref[0]) + sh_ref[0]).astype(o_ref.dtype)


def _norm(x2, w, mod3, B, tps, tm):
    T, D = x2.shape
    return pl.pallas_call(
        _norm_kernel,
        grid=(T // tm,),
        in_specs=[pl.BlockSpec((tm, D), lambda i: (i, 0)),
                  pl.BlockSpec((1, D), lambda i: (0, 0)),
                  pl.BlockSpec((1, 1, D), lambda i: (1 * B + i // tps, 0, 0)),
                  pl.BlockSpec((1, 1, D), lambda i: (0 * B + i // tps, 0, 0))],
        out_specs=pl.BlockSpec((tm, D), lambda i: (i, 0)),
        out_shape=jax.ShapeDtypeStruct((T, D), BF16),
        compiler_params=_cparams(("parallel",)),
        name="norm1",
    )(x2, w.reshape(1, D), mod3, mod3)


def _proj_kernel(h_ref, w_ref, o_ref):
    o_ref[...] = jnp.dot(h_ref[...], w_ref[...], preferred_element_type=F32).astype(o_ref.dtype)


def _proj(h, w, tm, tn):
    T, D = h.shape
    N = w.shape[1]
    return pl.pallas_call(
        _proj_kernel,
        grid=(N // tn, T // tm),
        in_specs=[pl.BlockSpec((tm, D), lambda j, i: (i, 0)),
                  pl.BlockSpec((D, tn), lambda j, i: (0, j))],
        out_specs=pl.BlockSpec((tm, tn), lambda j, i: (i, j)),
        out_shape=jax.ShapeDtypeStruct((T, N), BF16),
        compiler_params=_cparams(("parallel", "parallel")),
        name="proj_plain",
    )(h, w)


def _rope_proj_kernel(h_ref, w_ref, cos_ref, sa_ref, sb_ref, o_ref):
    acc = jnp.dot(h_ref[...], w_ref[...], preferred_element_type=F32)
    scale = jnp.where(pl.program_id(0) == 0, DA_QK ** -0.5 * LOG2E, 1.0).astype(F32)
    cosf, sa, sb = cos_ref[...], sa_ref[...], sb_ref[...]
    for hh in range(HEADS):
        blk = acc[:, hh * LANES:(hh + 1) * LANES]
        rot = (blk * cosf + pltpu.roll(blk, LANES - ROPE_HALF, 1) * sa
               + pltpu.roll(blk, ROPE_HALF, 1) * sb)
        o_ref[:, hh * LANES:(hh + 1) * LANES] = (rot * scale).astype(o_ref.dtype)


def _rope_proj(h, w, tables, tps, tm):
    T, D = h.shape
    cosf, sa, sb = tables
    tab_spec = pl.BlockSpec((tm, LANES), lambda j, i: (i % tps, 0))
    return pl.pallas_call(
        _rope_proj_kernel,
        grid=(2, T // tm),
        in_specs=[pl.BlockSpec((tm, D), lambda j, i: (i, 0)),
                  pl.BlockSpec((D, D), lambda j, i: (0, j)),
                  tab_spec, tab_spec, tab_spec],
        out_specs=pl.BlockSpec((tm, D), lambda j, i: (i, j)),
        out_shape=jax.ShapeDtypeStruct((T, 2 * D), BF16),
        compiler_params=_cparams(("parallel", "parallel")),
        name="proj_rope",
    )(h, w, cosf, sa, sb)


def _conv_proj_kernel(tps, h_ref, w_ref, wgi_ref, wgf_ref, wgit_ref, wgft_ref, cw_ref, cb_ref,
                      bi_ref, bf_ref, bit_ref, bft_ref,
                      o_ref, gi_ref, gf_ref, git_ref, gft_ref, buf):
    i = pl.program_id(0)
    tm = h_ref.shape[0]
    h = h_ref[...]
    acc = jnp.dot(h, w_ref[...], preferred_element_type=F32)

    @pl.when(i % tps == 0)
    def _():
        buf[0:SUBLANES, :] = jnp.zeros((SUBLANES, acc.shape[1]), F32)

    buf[SUBLANES:SUBLANES + tm, :] = acc
    cw = cw_ref[...]
    y = cb_ref[...]
    for j in range(CONV_WIDTH):
        off = SUBLANES - (CONV_WIDTH - 1) + j
        y = y + buf[off:off + tm, :] * cw[j:j + 1, :]
    buf[0:SUBLANES, :] = buf[tm:tm + SUBLANES, :]
    y = y * _sigmoid(y)
    lane = lax.broadcasted_iota(jnp.int32, (1, y.shape[1]), 1)
    kscale = jnp.where(lane >= HEADS * ML_QK, ML_QK ** -0.5, 1.0).astype(F32)
    o_ref[...] = (y * kscale).astype(o_ref.dtype)

    gi_ref[...] = jnp.dot(h, wgi_ref[...], preferred_element_type=F32)[:, :HEADS] + bi_ref[...]
    gf_ref[...] = jnp.dot(h, wgf_ref[...], preferred_element_type=F32)[:, :HEADS] + bf_ref[...]
    nt = (((1,), (1,)), ((), ()))
    git_ref[...] = lax.dot_general(wgit_ref[...], h, nt, preferred_element_type=F32) + bit_ref[...]
    gft_ref[...] = lax.dot_general(wgft_ref[...], h, nt, preferred_element_type=F32) + bft_ref[...]


def _conv_proj(h, w, wgi, wgf, conv_w, conv_b, bi, bf, tps, tm):
    T, D = h.shape
    N = w.shape[1]
    pad = lambda a: jnp.pad(a, ((0, 0), (0, LANES - a.shape[1])))
    full = lambda shape: pl.BlockSpec(shape, lambda i: tuple(0 for _ in shape))
    return pl.pallas_call(
        functools.partial(_conv_proj_kernel, tps),
        grid=(T // tm,),
        in_specs=[pl.BlockSpec((tm, D), lambda i: (i, 0)),
                  full((D, N)), full((D, LANES)), full((D, LANES)),
                  full((HEADS, D)), full((HEADS, D)),
                  full((CONV_WIDTH, N)), full((1, N)),
                  full((1, HEADS)), full((1, HEADS)), full((HEADS, 1)), full((HEADS, 1))],
        out_specs=[pl.BlockSpec((tm, N), lambda i: (i, 0)),
                   pl.BlockSpec((tm, HEADS), lambda i: (i, 0)),
                   pl.BlockSpec((tm, HEADS), lambda i: (i, 0)),
                   pl.BlockSpec((HEADS, tm), lambda i: (0, i)),
                   pl.BlockSpec((HEADS, tm), lambda i: (0, i))],
        out_shape=[jax.ShapeDtypeStruct((T, N), BF16),
                   jax.ShapeDtypeStruct((T, HEADS), F32),
                   jax.ShapeDtypeStruct((T, HEADS), F32),
                   jax.ShapeDtypeStruct((HEADS, T), F32),
                   jax.ShapeDtypeStruct((HEADS, T), F32)],
        scratch_shapes=[pltpu.VMEM((tm + 2 * SUBLANES, N), F32)],
        compiler_params=_cparams(("arbitrary",)),
        name="proj_conv",
    )(h, w, pad(wgi), pad(wgf), wgi.T, wgf.T, conv_w, conv_b.reshape(1, N),
      bi.reshape(1, HEADS), bf.reshape(1, HEADS), bi.reshape(HEADS, 1), bf.reshape(HEADS, 1))


def _attn_kernel(lam_init, q_ref, k_ref, v_ref, ga_ref, lq1_ref, lk1_ref, lq2_ref, lk2_ref, sw_ref,
                 o_ref, qt_s, vt_s, m_s, l_s, acc_s):
    qi = pl.program_id(2)
    tq = q_ref.shape[0]
    tk = vt_s.shape[2]
    ratio = tq // tk

    @pl.when(qi == 0)
    def _():
        for kk in range(vt_s.shape[0]):
            vt_s[kk] = v_ref[kk * tk:(kk + 1) * tk, :].T

    q = q_ref[...].astype(F32)
    lane = lax.broadcasted_iota(jnp.int32, q.shape, 1)
    qt_s[0] = jnp.where(lane < DA_QK, q, 0.0).T.astype(BF16)
    qt_s[1] = jnp.where(lane >= DA_QK, q, 0.0).T.astype(BF16)
    m_s[...] = jnp.full(m_s.shape, -jnp.inf, F32)
    l_s[...] = jnp.zeros(l_s.shape, F32)
    acc_s[...] = jnp.zeros(acc_s.shape, F32)

    def step(kk, diag_offset):
        k = k_ref[pl.ds(pl.multiple_of(kk * tk, tk), tk), :]
        vt = vt_s[kk]
        for c in range(2):
            st = jnp.dot(k, qt_s[c], preferred_element_type=F32)
            if diag_offset is not None:
                key = lax.broadcasted_iota(jnp.int32, st.shape, 0) + diag_offset
                qry = lax.broadcasted_iota(jnp.int32, st.shape, 1)
                st = jnp.where(key <= qry, st, -jnp.inf)
            m_prev = m_s[c]
            m_new = jnp.maximum(m_prev, jnp.max(st, axis=0, keepdims=True))
            alpha = jnp.exp2(m_prev - m_new)
            p = jnp.exp2(st - m_new)
            l_s[c] = alpha * l_s[c] + jnp.sum(p, axis=0, keepdims=True)
            acc_s[c] = alpha * acc_s[c] + jnp.dot(vt, p.astype(BF16), preferred_element_type=F32)
            m_s[c] = m_new

    def body(kk, carry):
        step(kk, None)
        return carry

    lax.fori_loop(0, qi * ratio, body, 0)
    for d in range(ratio):
        step(qi * ratio + d, d * tk)

    lam = (jnp.exp(jnp.sum(lq1_ref[...] * lk1_ref[...], axis=-1, keepdims=True))
           - jnp.exp(jnp.sum(lq2_ref[...] * lk2_ref[...], axis=-1, keepdims=True)) + lam_init)
    o = (acc_s[0] / l_s[0] - lam * (acc_s[1] / l_s[1])).T
    o = _rms(o) * sw_ref[...] * (1.0 - lam_init)
    o_ref[...] = (o * _sigmoid(ga_ref[...].astype(F32))).astype(o_ref.dtype)


def _attention(qk, plain, lam_vecs, subln_w, lam_init, B, S, tq, tk):
    T = qk.shape[0]
    nq = S // tq
    vec = pl.BlockSpec((1, DA_QK), lambda b, h, i: (0, 0))
    return pl.pallas_call(
        functools.partial(_attn_kernel, lam_init),
        grid=(B, HEADS, nq),
        in_specs=[pl.BlockSpec((tq, LANES), lambda b, h, i: (b * nq + i, h)),
                  pl.BlockSpec((S, LANES), lambda b, h, i: (b, HEADS + h)),
                  pl.BlockSpec((S, LANES), lambda b, h, i: (b, h)),
                  pl.BlockSpec((tq, LANES), lambda b, h, i: (b * nq + i, 3 * HEADS + h)),
                  vec, vec, vec, vec,
                  pl.BlockSpec((1, HEAD_V), lambda b, h, i: (0, 0))],
        out_specs=pl.BlockSpec((tq, LANES), lambda b, h, i: (b * nq + i, h)),
        out_shape=jax.ShapeDtypeStruct((T, D_MODEL), BF16),
        scratch_shapes=[pltpu.VMEM((2, LANES, tq), BF16),
                        pltpu.VMEM((S // tk, HEAD_V, tk), BF16),
                        pltpu.VMEM((2, 1, tq), F32),
                        pltpu.VMEM((2, 1, tq), F32),
                        pltpu.VMEM((2, HEAD_V, tq), F32)],
        compiler_params=_cparams(("parallel", "parallel", "arbitrary")),
        name="attn",
    )(qk, qk, plain, plain, *[v.reshape(1, DA_QK) for v in lam_vecs], subln_w.reshape(1, HEAD_V))


def _mlstm_kernel(qk_ref, v_ref, om_ref, gm_ref, gi_ref, gf_ref, git_ref, gft_ref, nw_ref,
                  o_ref, c_s, n_s, m_s):
    L = qk_ref.shape[0]

    @pl.when(pl.program_id(1) == 0)
    def _():
        c_s[...] = jnp.zeros(c_s.shape, F32)
        n_s[...] = jnp.zeros(n_s.shape, F32)
        m_s[...] = jnp.zeros(m_s.shape, F32)

    row = lax.broadcasted_iota(jnp.int32, (L, L), 0)
    col = lax.broadcasted_iota(jnp.int32, (L, L), 1)
    causal = col <= row
    tri = causal.astype(F32)
    tri_t = (row <= col).astype(F32)
    bcols = jnp.dot(tri, _log_sigmoid(gf_ref[...]), precision=HIGHEST, preferred_element_type=F32)
    brows = jnp.dot(_log_sigmoid(gft_ref[...]), tri_t, precision=HIGHEST, preferred_element_type=F32)
    gi = gi_ref[...]
    git = git_ref[...]
    lane = lax.broadcasted_iota(jnp.int32, (1, LANES), 1)
    nt = (((1,), (1,)), ((), ()))
    tn = (((0,), (0,)), ((), ()))

    for h in range(HEADS):
        p = h // 2
        hmask = ((lane >= (h % 2) * ML_QK) & (lane < (h % 2 + 1) * ML_QK)).astype(F32)
        qh = (qk_ref[:, p * LANES:(p + 1) * LANES].astype(F32) * hmask).astype(BF16)
        kp = qk_ref[:, (HEADS // 2 + p) * LANES:(HEADS // 2 + p + 1) * LANES]
        vh = v_ref[:, h * HEAD_V:(h + 1) * HEAD_V]
        bc = bcols[:, h:h + 1]
        igc = gi[:, h:h + 1]
        br = brows[h:h + 1, :]
        igr = git[h:h + 1, :]
        g_tot = br[:, L - 1:L]
        m_prev = m_s[h]
        n_prev = n_s[h]

        a = g_tot - bc + igc
        m_loc = jnp.max(a, axis=0, keepdims=True)
        w_loc = jnp.exp(a - m_loc)

        dm = jnp.where(causal, bc - br + igr, -jnp.inf)
        m_inter = bc + m_prev
        m_j = jnp.maximum(jnp.max(dm, axis=-1, keepdims=True), m_inter)
        s = lax.dot_general(qh, kp, nt, preferred_element_type=F32)
        qkw = s * jnp.exp(dm - m_j)
        inter_w = jnp.exp(m_inter - m_j)
        num = (jnp.dot(qkw.astype(BF16), vh, preferred_element_type=F32)
               + inter_w * jnp.dot(qh, c_s[h].astype(BF16), preferred_element_type=F32))
        den = (jnp.sum(qkw, axis=-1, keepdims=True)
               + inter_w * jnp.sum(qh.astype(F32) * n_prev, axis=-1, keepdims=True))
        hh = num / jnp.maximum(jnp.abs(den), jnp.exp(-m_j))

        m_new = jnp.maximum(g_tot + m_prev, m_loc)
        dec = jnp.exp(g_tot + m_prev - m_new)
        inc = jnp.exp(m_loc - m_new)
        wk = w_loc * kp.astype(F32)
        c_loc = lax.dot_general(wk.astype(BF16), vh, tn, preferred_element_type=F32)
        c_s[h] = dec * c_s[h] + inc * c_loc
        n_s[h] = dec * n_prev + inc * jnp.sum(wk, axis=0, keepdims=True)
        m_s[h] = m_new

        sl = slice(h * HEAD_V, (h + 1) * HEAD_V)
        y = _rms(hh) * nw_ref[:, sl]
        y = y * _sigmoid(om_ref[:, sl].astype(F32)) * _sigmoid(gm_ref[:, sl].astype(F32))
        o_ref[:, sl] = y.astype(o_ref.dtype)


def _mlstm(qkc, plain, gi, gf, git, gft, ml_norm_w, B, S):
    T, D = qkc.shape
    L = ML_CHUNK
    nc = S // L
    wide = lambda cb: pl.BlockSpec((L, D), lambda b, c: (b * nc + c, cb))
    return pl.pallas_call(
        _mlstm_kernel,
        grid=(B, nc),
        in_specs=[wide(0), wide(1), wide(2), wide(4),
                  pl.BlockSpec((L, HEADS), lambda b, c: (b * nc + c, 0)),
                  pl.BlockSpec((L, HEADS), lambda b, c: (b * nc + c, 0)),
                  pl.BlockSpec((HEADS, L), lambda b, c: (0, b * nc + c)),
                  pl.BlockSpec((HEADS, L), lambda b, c: (0, b * nc + c)),
                  pl.BlockSpec((1, D), lambda b, c: (0, 0))],
        out_specs=pl.BlockSpec((L, D), lambda b, c: (b * nc + c, 0)),
        out_shape=jax.ShapeDtypeStruct((T, D), BF16),
        scratch_shapes=[pltpu.VMEM((HEADS, LANES, HEAD_V), F32),
                        pltpu.VMEM((HEADS, 1, LANES), F32),
                        pltpu.VMEM((HEADS, 1, 1), F32)],
        compiler_params=_cparams(("parallel", "arbitrary")),
        name="mlstm",
    )(qkc, plain, plain, plain, gi, gf, git, gft, ml_norm_w.reshape(1, D))


def _outproj_kernel(ya_ref, ym_ref, x_ref, w_ref, gt_ref, nw_ref, sc_ref, sh_ref, x1_ref, h2t_ref):
    merged = (ya_ref[...].astype(F32) + ym_ref[...].astype(F32)).astype(BF16)
    x1 = x_ref[...] + gt_ref[0] * jnp.dot(merged, w_ref[...], preferred_element_type=F32)
    x1_ref[...] = x1
    h2 = _rms(x1) * nw_ref[...] * (1.0 + sc_ref[0]) + sh_ref[0]
    h2t_ref[...] = h2.T.astype(h2t_ref.dtype)


def _outproj(ya, ym, x2, w_out, norm2_w, mod3, B, tps, tm):
    T, D = x2.shape
    row = lambda k: pl.BlockSpec((1, 1, D), lambda i: (k * B + i // tps, 0, 0))
    tile = pl.BlockSpec((tm, D), lambda i: (i, 0))
    return pl.pallas_call(
        _outproj_kernel,
        grid=(T // tm,),
        in_specs=[tile, tile, tile, pl.BlockSpec((D, D), lambda i: (0, 0)), row(2),
                  pl.BlockSpec((1, D), lambda i: (0, 0)), row(4), row(3)],
        out_specs=[tile, pl.BlockSpec((D, tm), lambda i: (0, i))],
        out_shape=[jax.ShapeDtypeStruct((T, D), F32), jax.ShapeDtypeStruct((D, T), BF16)],
        compiler_params=_cparams(("parallel",)),
        name="outproj",
    )(ya, ym, x2, w_out, mod3, norm2_w.reshape(1, D), mod3, mod3)


def _topk_ranks(s):
    n = s.shape[0]
    it = lax.broadcasted_iota(jnp.int32, s.shape, 0)
    rank = jnp.full(s.shape, float(TOPK), F32)
    x = s
    vals = []
    for r in range(TOPK):
        m = jnp.max(x, axis=0, keepdims=True)
        first = jnp.min(jnp.where(x == m, it, n), axis=0, keepdims=True)
        hit = it == first
        x = jnp.where(hit, -jnp.inf, x)
        rank = jnp.where(hit, float(r), rank)
        vals.append(m)
    return rank, vals


def _topk_ranks_distinct(s):
    rank = jnp.full(s.shape, float(TOPK), F32)
    x = s
    vals = []
    for r in range(TOPK):
        m = jnp.max(x, axis=0, keepdims=True)
        hit = x == m
        x = jnp.where(hit, -jnp.inf, x)
        rank = jnp.where(hit, float(r), rank)
        vals.append(m)
    removed = jnp.sum(jnp.where(x == -jnp.inf, 1.0, 0.0), axis=0, keepdims=True)
    return rank, vals, removed == float(TOPK)


def _peer_sel_kernel(h2t_ref, wqt_ref, keys_ref, r2_ref, e2_ref, a_ref, c_ref, rank_s, vals_s):
    ht = h2t_ref[...]
    scores = []
    for p in range(2):
        qt = jnp.dot(wqt_ref[p * LANES:(p + 1) * LANES, :], ht, preferred_element_type=F32)
        s = jnp.dot(keys_ref[p], qt.astype(BF16), preferred_element_type=F32)
        scores.append(s)
        rank, vals, distinct = _topk_ranks_distinct(s)
        rank_s[p] = rank
        vals_s[p] = jnp.concatenate(vals, axis=0)

        @pl.when(jnp.sum(jnp.where(distinct, 0.0, 1.0)) > 0.0)
        def _():
            rank, vals = _topk_ranks(s)
            rank_s[p] = rank
            vals_s[p] = jnp.concatenate(vals, axis=0)

    s1, s2 = scores
    rank1, rank2 = rank_s[0], rank_s[1]
    v1 = [vals_s[0, r:r + 1, :] for r in range(TOPK)]
    v2 = [vals_s[1, r:r + 1, :] for r in range(TOPK)]

    tt = s1.shape[1]
    rows = [v1[r1] + v2[r2] for (r1, r2) in _CAND]
    rows += [jnp.full((1, tt), -jnp.inf, F32)] * (_CAND_ROWS - len(_CAND))
    cand = jnp.concatenate(rows, axis=0)
    it = lax.broadcasted_iota(jnp.int32, cand.shape, 0)
    sel = jnp.zeros(cand.shape, F32)
    top = v1[0] + v2[0]
    z = jnp.zeros((1, tt), F32)
    x = cand
    for _ in range(TOPK):
        m = jnp.max(x, axis=0, keepdims=True)
        first = jnp.min(jnp.where(x == m, it, _CAND_ROWS), axis=0, keepdims=True)
        hit = it == first
        x = jnp.where(hit, -jnp.inf, x)
        sel = jnp.where(hit, 1.0, sel)
        z = z + jnp.exp(m - top)
    a = jnp.zeros(s1.shape, F32)
    start = 0
    for r1 in range(TOPK):
        width = sum(1 for c in _CAND if c[0] == r1)
        cnt = jnp.sum(sel[start:start + width, :], axis=0, keepdims=True)
        a = jnp.where(rank1 == float(r1), cnt, a)
        start += width

    for ref, val in ((r2_ref, rank2), (e2_ref, jnp.exp(s2 - v2[0])), (a_ref, a), (c_ref, jnp.exp(s1 - v1[0]) / z)):
        for lt in range(tt // LANES):
            ref[lt] = val[:, lt * LANES:(lt + 1) * LANES].astype(ref.dtype)


def _peer_sel(h2t, wqt, keys, tt):
    D, T = h2t.shape
    out = pl.BlockSpec((tt // LANES, N_KEYS, LANES), lambda i, h: (i, h, 0))
    shp = lambda dt: jax.ShapeDtypeStruct((T // LANES, HEADS * N_KEYS, LANES), dt)
    return pl.pallas_call(
        _peer_sel_kernel,
        grid=(T // tt, HEADS),
        in_specs=[pl.BlockSpec((D, tt), lambda i, h: (0, i)),
                  pl.BlockSpec((2 * LANES, D), lambda i, h: (h, 0)),
                  pl.BlockSpec((2, N_KEYS, LANES), lambda i, h: (h, 0, 0))],
        out_specs=[out, out, out, out],
        out_shape=[shp(BF16), shp(BF16), shp(F32), shp(F32)],
        scratch_shapes=[pltpu.VMEM((2, N_KEYS, tt), F32), pltpu.VMEM((2, TOPK, tt), F32)],
        compiler_params=_cparams(("parallel", "parallel")),
        name="peer_sel",
    )(h2t, wqt, keys)


PEER_ROWS = 64
MXU_COLS = 256
ACT_ROWS = 256
OUT_ROWS = 512


def _gate_rows(row, rows):
    packed = jnp.broadcast_to(row, (2 * SUBLANES, LANES)).astype(BF16)
    return jnp.tile(packed, (rows // (2 * SUBLANES), 1))


def _peer_main_kernel(final, h2t_ref, u_ref, vt_ref, r2_ref, e2_ref, a_ref, c_ref, x1_ref, gt_ref, fw_ref,
                      o_ref, acc_ref, act_ref, p_ref):
    e = pl.program_id(1)
    eb, tt = act_ref.shape
    nj = eb // N_KEYS

    @pl.when(e == 0)
    def _():
        acc_ref[...] = jnp.zeros(acc_ref.shape, F32)

    n_half = tt // MXU_COLS
    tcols = lambda th: slice(th * MXU_COLS, (th + 1) * MXU_COLS)

    def act_piece(th, m):
        ms = slice(m * ACT_ROWS, (m + 1) * ACT_ROWS)
        act_ref[ms, tcols(th)] = jnp.dot(u_ref[ms, :], h2t_ref[:, tcols(th)], preferred_element_type=F32)

    def out_piece(th, r):
        rs = slice(r * OUT_ROWS, (r + 1) * OUT_ROWS)
        acc_ref[rs, tcols(th)] += jnp.dot(vt_ref[rs, :], p_ref[:, tcols(th)], preferred_element_type=F32)

    def gate_chunk(tl, bs):
        g = [None] * nj
        for h in range(HEADS):
            rs = slice(h * N_KEYS + bs * PEER_ROWS, h * N_KEYS + (bs + 1) * PEER_ROWS)
            r2c = r2_ref[tl, rs, :]
            e2c = e2_ref[tl, rs, :]
            for j in range(nj):
                idx = h * N_KEYS + e * nj + j
                arow = _gate_rows(a_ref[tl, pl.ds(idx, 1), :], PEER_ROWS)
                crow = _gate_rows(c_ref[tl, pl.ds(idx, 1), :], PEER_ROWS)
                w = jnp.where(r2c < arow, e2c, jnp.zeros_like(e2c)) * crow
                g[j] = w if g[j] is None else g[j] + w
        ls = slice(tl * LANES, (tl + 1) * LANES)
        for j in range(nj):
            es = slice(j * N_KEYS + bs * PEER_ROWS, j * N_KEYS + (bs + 1) * PEER_ROWS)
            aj = act_ref[es, ls]
            gelu = 0.5 * aj * (1.0 + lax.erf(aj * (2.0 ** -0.5)))
            p_ref[es, ls] = g[j] * gelu.astype(BF16)

    n_act = eb // ACT_ROWS
    n_out = acc_ref.shape[0] // OUT_ROWS
    for m in range(n_act):
        act_piece(0, m)
    for th in range(n_half):
        mxu_work = [functools.partial(act_piece, th + 1, m) for m in range(n_act)] if th + 1 < n_half else []
        if th > 0:
            mxu_work += [functools.partial(out_piece, th - 1, r) for r in range(n_out)]
        chunks = [(th * (MXU_COLS // LANES) + lt, bs)
                  for lt in range(MXU_COLS // LANES) for bs in range(N_KEYS // PEER_ROWS)]
        per_chunk = -(-len(mxu_work) // len(chunks))
        for tl, bs in chunks:
            gate_chunk(tl, bs)
            for piece in mxu_work[:per_chunk]:
                piece()
            mxu_work = mxu_work[per_chunk:]
    for r in range(n_out):
        out_piece(n_half - 1, r)

    @pl.when(e == pl.num_programs(1) - 1)
    def _():
        x2 = x1_ref[...] + gt_ref[0] * acc_ref[...].T
        o_ref[...] = _rms(x2) * fw_ref[...] if final else x2


def _peer_main(h2t, u, vt, sel, x1, mod3, fw, final, B, tps, tt, eb):
    D, T = h2t.shape
    ne = u.shape[0]
    wide = pl.BlockSpec((tt // LANES, HEADS * N_KEYS, LANES), lambda i, e: (i, 0, 0))
    return pl.pallas_call(
        functools.partial(_peer_main_kernel, final),
        grid=(T // tt, ne // eb),
        in_specs=[pl.BlockSpec((D, tt), lambda i, e: (0, i)),
                  pl.BlockSpec((eb, D), lambda i, e: (e, 0)),
                  pl.BlockSpec((D, eb), lambda i, e: (0, e)),
                  wide, wide, wide, wide,
                  pl.BlockSpec((tt, D), lambda i, e: (i, 0)),
                  pl.BlockSpec((1, 1, D), lambda i, e: (5 * B + i // tps, 0, 0)),
                  pl.BlockSpec((1, D), lambda i, e: (0, 0))],
        out_specs=pl.BlockSpec((tt, D), lambda i, e: (i, 0)),
        out_shape=jax.ShapeDtypeStruct((T, D), F32),
        scratch_shapes=[pltpu.VMEM((D, tt), F32),
                        pltpu.VMEM((eb, tt), F32),
                        pltpu.VMEM((eb, tt), BF16)],
        compiler_params=_cparams(("parallel", "arbitrary")),
        name="peer_main",
    )(h2t, u, vt, *sel, x1, mod3, fw.reshape(1, D))


def _rope_tables(S):
    inv = ROPE_THETA ** (-jnp.arange(ROPE_HALF, dtype=F32) * 2.0 / ROPE_DIM)
    ang = jnp.arange(S, dtype=jnp.int32).astype(F32)[:, None] * inv[None, :]
    cos, sin = jnp.cos(ang), jnp.sin(ang)
    zeros = jnp.zeros((S, DA_QK - ROPE_DIM), F32)
    z8 = jnp.zeros((S, ROPE_HALF), F32)
    cosf = jnp.concatenate([cos, cos, zeros + 1.0] * 2, axis=1)
    sa = jnp.concatenate([-sin, z8, zeros] * 2, axis=1)
    sb = jnp.concatenate([z8, sin, zeros] * 2, axis=1)
    return cosf, sa, sb


def kernel(x, c, ada_w, ada_b, norm1_w, norm2_w, w_in, conv_w, conv_b, ml_i_bias, ml_f_bias, ml_norm_w, lam_q1, lam_k1, lam_q2, lam_k2, subln_w, w_out, peer_wq, peer_keys, peer_u, peer_v, final_norm_w):
    B, S, D = x.shape
    assert D == D_MODEL and S % 512 == 0
    T = B * S
    depth = w_in.shape[0]
    tm = 512
    tps = S // tm
    tables = _rope_tables(S)
    x2 = x.reshape(T, D)
    for l in range(depth):
        mod3 = _mod(c, ada_w[l], ada_b[l])
        h = _norm(x2, norm1_w[l], mod3, B, tps, tm)

        wl = w_in[l]
        o = 0
        cols = {}
        for name, width in (("qa", D), ("ka", D), ("va", D), ("qm", D // 2), ("km", D // 2), ("vm", D),
                            ("om", D), ("ip", HEADS), ("fp", HEADS), ("ga", D), ("gm", D)):
            cols[name] = wl[:, o:o + width]
            o += width
        w_rope = jnp.concatenate([cols["qa"], cols["ka"]], axis=1).astype(BF16)
        w_conv = jnp.concatenate([cols["qm"], cols["km"]], axis=1).astype(BF16)
        w_plain = jnp.concatenate([cols[n] for n in ("va", "vm", "om", "ga", "gm")], axis=1).astype(BF16)

        qk = _rope_proj(h, w_rope, tables, tps, tm)
        plain = _proj(h, w_plain, tm, D)
        qkc, gi, gf, git, gft = _conv_proj(h, w_conv, cols["ip"].astype(BF16), cols["fp"].astype(BF16),
                                           conv_w[l], conv_b[l], ml_i_bias[l], ml_f_bias[l], tps, tm)

        lam_init = 0.8 - 0.6 * math.exp(-0.3 * l)
        ya = _attention(qk, plain, (lam_q1[l], lam_k1[l], lam_q2[l], lam_k2[l]), subln_w[l], lam_init, B, S, ATTN_TQ, ATTN_TK)
        ym = _mlstm(qkc, plain, gi, gf, git, gft, ml_norm_w[l], B, S)
        x1, h2t = _outproj(ya, ym, x2, w_out[l].astype(BF16), norm2_w[l], mod3, B, tps, tm)

        wqt = peer_wq[l].T.astype(BF16)
        keys = peer_keys[l].reshape(2 * HEADS, N_KEYS, LANES).astype(BF16)
        sel = _peer_sel(h2t, wqt, keys, tm)
        x2 = _peer_main(h2t, peer_u[l].astype(BF16), peer_v[l].T.astype(BF16), sel, x1, mod3,
                        final_norm_w, l == depth - 1, B, tps, tm, 8 * N_KEYS)
    if depth == 0:
        raise ValueError("depth must be positive")
    return x2.reshape(B, S, D)
```

```python
import functools
import math

import jax
import jax.numpy as jnp
from jax import lax
from jax.experimental import pallas as pl
from jax.experimental.pallas import tpu as pltpu

F32 = jnp.float32
BF16 = jnp.bfloat16
HIGHEST = lax.Precision.HIGHEST

NORM_EPS = 1e-6
LOG2E = 1.4426950408889634
D_MODEL = 1024
HEADS = 8
HEAD_V = 128
DA_QK = 64
ROPE_DIM = 16
ROPE_HALF = 8
ROPE_THETA = 500000.0
ML_QK = 64
CONV_WIDTH = 4
ML_CHUNK = 128
ATTN_TQ = 512
ATTN_TK = 512
N_KEYS = 128
TOPK = 16
LANES = 128
SUBLANES = 8
VMEM_LIMIT = 48 * 1024 * 1024

_CAND = [(r1, r2) for r1 in range(TOPK) for r2 in range(TOPK) if (r1 + 1) * (r2 + 1) <= TOPK]
_CAND_ROWS = -(-len(_CAND) // SUBLANES) * SUBLANES


def _cparams(sem):
    return pltpu.CompilerParams(dimension_semantics=sem, vmem_limit_bytes=VMEM_LIMIT)


def _rms(x):
    return x * lax.rsqrt(jnp.mean(x * x, axis=-1, keepdims=True) + NORM_EPS)


def _sigmoid(x):
    return 1.0 / (1.0 + jnp.exp(-x))


def _log_sigmoid(x):
    return jnp.minimum(x, 0.0) - jnp.log(1.0 + jnp.exp(-jnp.abs(x)))


def _mod_kernel(c_ref, w_ref, b_ref, o_ref):
    c = c_ref[...]
    cond = c * _sigmoid(c)
    o_ref[0] = jnp.dot(cond, w_ref[...], precision=HIGHEST, preferred_element_type=F32) + b_ref[...]


def _mod(c, ada_w, ada_b):
    B, D = c.shape
    out = pl.pallas_call(
        _mod_kernel,
        grid=(6,),
        in_specs=[pl.BlockSpec((B, D), lambda j: (0, 0)),
                  pl.BlockSpec((D, D), lambda j: (0, j)),
                  pl.BlockSpec((1, D), lambda j: (0, j))],
        out_specs=pl.BlockSpec((1, B, D), lambda j: (j, 0, 0)),
        out_shape=jax.ShapeDtypeStruct((6, B, D), F32),
        compiler_params=_cparams(("parallel",)),
        name="mod",
    )(c, ada_w, ada_b.reshape(1, 6 * D))
    return out.reshape(6 * B, 1, D)


def _norm_kernel(x_ref, w_ref, sc_ref, sh_ref, o_ref):
    y = _rms(x_ref[...]) * w_ref[...]
    o_ref[...] = (y * (1.0 + sc_ref[0]) + sh_ref[0]).astype(o_ref.dtype)


def _norm(x2, w, mod3, B, tps, tm):
    T, D = x2.shape
    return pl.pallas_call(
        _norm_kernel,
        grid=(T // tm,),
        in_specs=[pl.BlockSpec((tm, D), lambda i: (i, 0)),
                  pl.BlockSpec((1, D), lambda i: (0, 0)),
                  pl.BlockSpec((1, 1, D), lambda i: (1 * B + i // tps, 0, 0)),
                  pl.BlockSpec((1, 1, D), lambda i: (0 * B + i // tps, 0, 0))],
        out_specs=pl.BlockSpec((tm, D), lambda i: (i, 0)),
        out_shape=jax.ShapeDtypeStruct((T, D), BF16),
        compiler_params=_cparams(("parallel",)),
        name="norm1",
    )(x2, w.reshape(1, D), mod3, mod3)


def _proj_kernel(h_ref, w_ref, o_ref):
    o_ref[...] = jnp.dot(h_ref[...], w_ref[...], preferred_element_type=F32).astype(o_ref.dtype)


def _proj(h, w, tm, tn):
    T, D = h.shape
    N = w.shape[1]
    return pl.pallas_call(
        _proj_kernel,
        grid=(N // tn, T // tm),
        in_specs=[pl.BlockSpec((tm, D), lambda j, i: (i, 0)),
                  pl.BlockSpec((D, tn), lambda j, i: (0, j))],
        out_specs=pl.BlockSpec((tm, tn), lambda j, i: (i, j)),
        out_shape=jax.ShapeDtypeStruct((T, N), BF16),
        compiler_params=_cparams(("parallel", "parallel")),
        name="proj_plain",
    )(h, w)


def _rope_proj_kernel(h_ref, w_ref, cos_ref, sa_ref, sb_ref, o_ref):
    acc = jnp.dot(h_ref[...], w_ref[...], preferred_element_type=F32)
    scale = jnp.where(pl.program_id(0) == 0, DA_QK ** -0.5 * LOG2E, 1.0).astype(F32)
    cosf, sa, sb = cos_ref[...], sa_ref[...], sb_ref[...]
    for hh in range(HEADS):
        blk = acc[:, hh * LANES:(hh + 1) * LANES]
        rot = (blk * cosf + pltpu.roll(blk, LANES - ROPE_HALF, 1) * sa
               + pltpu.roll(blk, ROPE_HALF, 1) * sb)
        o_ref[:, hh * LANES:(hh + 1) * LANES] = (rot * scale).astype(o_ref.dtype)


def _rope_proj(h, w, tables, tps, tm):
    T, D = h.shape
    cosf, sa, sb = tables
    tab_spec = pl.BlockSpec((tm, LANES), lambda j, i: (i % tps, 0))
    return pl.pallas_call(
        _rope_proj_kernel,
        grid=(2, T // tm),
        in_specs=[pl.BlockSpec((tm, D), lambda j, i: (i, 0)),
                  pl.BlockSpec((D, D), lambda j, i: (0, j)),
                  tab_spec, tab_spec, tab_spec],
        out_specs=pl.BlockSpec((tm, D), lambda j, i: (i, j)),
        out_shape=jax.ShapeDtypeStruct((T, 2 * D), BF16),
        compiler_params=_cparams(("parallel", "parallel")),
        name="proj_rope",
    )(h, w, cosf, sa, sb)


def _conv_proj_kernel(tps, h_ref, w_ref, wgi_ref, wgf_ref, wgit_ref, wgft_ref, cw_ref, cb_ref,
                      bi_ref, bf_ref, bit_ref, bft_ref,
                      o_ref, gi_ref, gf_ref, git_ref, gft_ref, buf):
    i = pl.program_id(0)
    tm = h_ref.shape[0]
    h = h_ref[...]
    acc = jnp.dot(h, w_ref[...], preferred_element_type=F32)

    @pl.when(i % tps == 0)
    def _():
        buf[0:SUBLANES, :] = jnp.zeros((SUBLANES, acc.shape[1]), F32)

    buf[SUBLANES:SUBLANES + tm, :] = acc
    cw = cw_ref[...]
    y = cb_ref[...]
    for j in range(CONV_WIDTH):
        off = SUBLANES - (CONV_WIDTH - 1) + j
        y = y + buf[off:off + tm, :] * cw[j:j + 1, :]
    buf[0:SUBLANES, :] = buf[tm:tm + SUBLANES, :]
    y = y * _sigmoid(y)
    lane = lax.broadcasted_iota(jnp.int32, (1, y.shape[1]), 1)
    kscale = jnp.where(lane >= HEADS * ML_QK, ML_QK ** -0.5, 1.0).astype(F32)
    o_ref[...] = (y * kscale).astype(o_ref.dtype)

    gi_ref[...] = jnp.dot(h, wgi_ref[...], preferred_element_type=F32)[:, :HEADS] + bi_ref[...]
    gf_ref[...] = jnp.dot(h, wgf_ref[...], preferred_element_type=F32)[:, :HEADS] + bf_ref[...]
    nt = (((1,), (1,)), ((), ()))
    git_ref[...] = lax.dot_general(wgit_ref[...], h, nt, preferred_element_type=F32) + bit_ref[...]
    gft_ref[...] = lax.dot_general(wgft_ref[...], h, nt, preferred_element_type=F32) + bft_ref[...]


def _conv_proj(h, w, wgi, wgf, conv_w, conv_b, bi, bf, tps, tm):
    T, D = h.shape
    N = w.shape[1]
    pad = lambda a: jnp.pad(a, ((0, 0), (0, LANES - a.shape[1])))
    full = lambda shape: pl.BlockSpec(shape, lambda i: tuple(0 for _ in shape))
    return pl.pallas_call(
        functools.partial(_conv_proj_kernel, tps),
        grid=(T // tm,),
        in_specs=[pl.BlockSpec((tm, D), lambda i: (i, 0)),
                  full((D, N)), full((D, LANES)), full((D, LANES)),
                  full((HEADS, D)), full((HEADS, D)),
                  full((CONV_WIDTH, N)), full((1, N)),
                  full((1, HEADS)), full((1, HEADS)), full((HEADS, 1)), full((HEADS, 1))],
        out_specs=[pl.BlockSpec((tm, N), lambda i: (i, 0)),
                   pl.BlockSpec((tm, HEADS), lambda i: (i, 0)),
                   pl.BlockSpec((tm, HEADS), lambda i: (i, 0)),
                   pl.BlockSpec((HEADS, tm), lambda i: (0, i)),
                   pl.BlockSpec((HEADS, tm), lambda i: (0, i))],
        out_shape=[jax.ShapeDtypeStruct((T, N), BF16),
                   jax.ShapeDtypeStruct((T, HEADS), F32),
                   jax.ShapeDtypeStruct((T, HEADS), F32),
                   jax.ShapeDtypeStruct((HEADS, T), F32),
                   jax.ShapeDtypeStruct((HEADS, T), F32)],
        scratch_shapes=[pltpu.VMEM((tm + 2 * SUBLANES, N), F32)],
        compiler_params=_cparams(("arbitrary",)),
        name="proj_conv",
    )(h, w, pad(wgi), pad(wgf), wgi.T, wgf.T, conv_w, conv_b.reshape(1, N),
      bi.reshape(1, HEADS), bf.reshape(1, HEADS), bi.reshape(HEADS, 1), bf.reshape(HEADS, 1))


def _attn_kernel(lam_init, q_ref, k_ref, v_ref, ga_ref, lq1_ref, lk1_ref, lq2_ref, lk2_ref, sw_ref,
                 o_ref, qt_s, vt_s, sa_s, sb_s, m_s, l_s, acc_s):
    qi = pl.program_id(2)
    tq = q_ref.shape[0]
    tk = vt_s.shape[2]
    ratio = tq // tk

    @pl.when(qi == 0)
    def _():
        for kk in range(vt_s.shape[0]):
            vt_s[kk] = v_ref[kk * tk:(kk + 1) * tk, :].T

    q = q_ref[...].astype(F32)
    lane = lax.broadcasted_iota(jnp.int32, q.shape, 1)
    qt_s[0] = jnp.where(lane < DA_QK, q, 0.0).T.astype(BF16)
    qt_s[1] = jnp.where(lane >= DA_QK, q, 0.0).T.astype(BF16)
    m_s[...] = jnp.full(m_s.shape, -jnp.inf, F32)
    l_s[...] = jnp.zeros(l_s.shape, F32)
    acc_s[...] = jnp.zeros(acc_s.shape, F32)

    def scores(kk, st_ref):
        k = k_ref[pl.ds(pl.multiple_of(kk * tk, tk), tk), :]
        for c in range(2):
            st_ref[c] = jnp.dot(k, qt_s[c], preferred_element_type=F32)

    def process(kk, st_ref, diag_offset):
        vt = vt_s[kk]
        for c in range(2):
            st = st_ref[c]
            if diag_offset is not None:
                key = lax.broadcasted_iota(jnp.int32, st.shape, 0) + diag_offset
                qry = lax.broadcasted_iota(jnp.int32, st.shape, 1)
                st = jnp.where(key <= qry, st, -jnp.inf)
            m_prev = m_s[c]
            m_new = jnp.maximum(m_prev, jnp.max(st, axis=0, keepdims=True))
            alpha = jnp.exp2(m_prev - m_new)
            p = jnp.exp2(st - m_new)
            l_s[c] = alpha * l_s[c] + jnp.sum(p, axis=0, keepdims=True)
            acc_s[c] = alpha * acc_s[c] + jnp.dot(vt, p.astype(BF16), preferred_element_type=F32)
            m_s[c] = m_new

    def diagonal(first, cur, nxt):
        for d in range(ratio):
            if d + 1 < ratio:
                scores(first + d + 1, nxt)
            process(first + d, cur, d * tk)
            cur, nxt = nxt, cur

    n_below = qi * ratio
    scores(0, sa_s)

    def pair(i, carry):
        kk = 2 * i
        scores(kk + 1, sb_s)
        process(kk, sa_s, None)
        scores(kk + 2, sa_s)
        process(kk + 1, sb_s, None)
        return carry

    lax.fori_loop(0, n_below // 2, pair, 0)
    odd = lax.rem(n_below, 2) == 1

    @pl.when(odd)
    def _():
        scores(n_below, sb_s)
        process(n_below - 1, sa_s, None)
        diagonal(n_below, sb_s, sa_s)

    @pl.when(jnp.logical_not(odd))
    def _():
        diagonal(n_below, sa_s, sb_s)

    lam = (jnp.exp(jnp.sum(lq1_ref[...] * lk1_ref[...], axis=-1, keepdims=True))
           - jnp.exp(jnp.sum(lq2_ref[...] * lk2_ref[...], axis=-1, keepdims=True)) + lam_init)
    o = (acc_s[0] / l_s[0] - lam * (acc_s[1] / l_s[1])).T
    o = _rms(o) * sw_ref[...] * (1.0 - lam_init)
    o_ref[...] = (o * _sigmoid(ga_ref[...].astype(F32))).astype(o_ref.dtype)


def _attention(qk, plain, lam_vecs, subln_w, lam_init, B, S, tq, tk):
    T = qk.shape[0]
    nq = S // tq
    vec = pl.BlockSpec((1, DA_QK), lambda b, h, i: (0, 0))
    return pl.pallas_call(
        functools.partial(_attn_kernel, lam_init),
        grid=(B, HEADS, nq),
        in_specs=[pl.BlockSpec((tq, LANES), lambda b, h, i: (b * nq + i, h)),
                  pl.BlockSpec((S, LANES), lambda b, h, i: (b, HEADS + h)),
                  pl.BlockSpec((S, LANES), lambda b, h, i: (b, h)),
                  pl.BlockSpec((tq, LANES), lambda b, h, i: (b * nq + i, 3 * HEADS + h)),
                  vec, vec, vec, vec,
                  pl.BlockSpec((1, HEAD_V), lambda b, h, i: (0, 0))],
        out_specs=pl.BlockSpec((tq, LANES), lambda b, h, i: (b * nq + i, h)),
        out_shape=jax.ShapeDtypeStruct((T, D_MODEL), BF16),
        scratch_shapes=[pltpu.VMEM((2, LANES, tq), BF16),
                        pltpu.VMEM((S // tk, HEAD_V, tk), BF16),
                        pltpu.VMEM((2, tk, tq), F32),
                        pltpu.VMEM((2, tk, tq), F32),
                        pltpu.VMEM((2, 1, tq), F32),
                        pltpu.VMEM((2, 1, tq), F32),
                        pltpu.VMEM((2, HEAD_V, tq), F32)],
        compiler_params=_cparams(("parallel", "parallel", "arbitrary")),
        name="attn",
    )(qk, qk, plain, plain, *[v.reshape(1, DA_QK) for v in lam_vecs], subln_w.reshape(1, HEAD_V))


def _mlstm_kernel(qk_ref, v_ref, om_ref, gm_ref, gi_ref, gf_ref, git_ref, gft_ref, nw_ref,
                  o_ref, c_s, n_s, m_s):
    L = qk_ref.shape[0]

    @pl.when(pl.program_id(1) == 0)
    def _():
        c_s[...] = jnp.zeros(c_s.shape, F32)
        n_s[...] = jnp.zeros(n_s.shape, F32)
        m_s[...] = jnp.zeros(m_s.shape, F32)

    row = lax.broadcasted_iota(jnp.int32, (L, L), 0)
    col = lax.broadcasted_iota(jnp.int32, (L, L), 1)
    causal = col <= row
    tri = causal.astype(F32)
    tri_t = (row <= col).astype(F32)
    bcols = jnp.dot(tri, _log_sigmoid(gf_ref[...]), precision=HIGHEST, preferred_element_type=F32)
    brows = jnp.dot(_log_sigmoid(gft_ref[...]), tri_t, precision=HIGHEST, preferred_element_type=F32)
    gi = gi_ref[...]
    git = git_ref[...]
    lane = lax.broadcasted_iota(jnp.int32, (1, LANES), 1)
    nt = (((1,), (1,)), ((), ()))
    tn = (((0,), (0,)), ((), ()))

    for h in range(HEADS):
        p = h // 2
        hmask = ((lane >= (h % 2) * ML_QK) & (lane < (h % 2 + 1) * ML_QK)).astype(F32)
        qh = (qk_ref[:, p * LANES:(p + 1) * LANES].astype(F32) * hmask).astype(BF16)
        kp = qk_ref[:, (HEADS // 2 + p) * LANES:(HEADS // 2 + p + 1) * LANES]
        vh = v_ref[:, h * HEAD_V:(h + 1) * HEAD_V]
        bc = bcols[:, h:h + 1]
        igc = gi[:, h:h + 1]
        br = brows[h:h + 1, :]
        igr = git[h:h + 1, :]
        g_tot = br[:, L - 1:L]
        m_prev = m_s[h]
        n_prev = n_s[h]

        a = g_tot - bc + igc
        m_loc = jnp.max(a, axis=0, keepdims=True)
        w_loc = jnp.exp(a - m_loc)

        dm = jnp.where(causal, bc - br + igr, -jnp.inf)
        m_inter = bc + m_prev
        m_j = jnp.maximum(jnp.max(dm, axis=-1, keepdims=True), m_inter)
        s = lax.dot_general(qh, kp, nt, preferred_element_type=F32)
        qkw = s * jnp.exp(dm - m_j)
        inter_w = jnp.exp(m_inter - m_j)
        num = (jnp.dot(qkw.astype(BF16), vh, preferred_element_type=F32)
               + inter_w * jnp.dot(qh, c_s[h].astype(BF16), preferred_element_type=F32))
        den = (jnp.sum(qkw, axis=-1, keepdims=True)
               + inter_w * jnp.sum(qh.astype(F32) * n_prev, axis=-1, keepdims=True))
        hh = num / jnp.maximum(jnp.abs(den), jnp.exp(-m_j))

        m_new = jnp.maximum(g_tot + m_prev, m_loc)
        dec = jnp.exp(g_tot + m_prev - m_new)
        inc = jnp.exp(m_loc - m_new)
        wk = w_loc * kp.astype(F32)
        c_loc = lax.dot_general(wk.astype(BF16), vh, tn, preferred_element_type=F32)
        c_s[h] = dec * c_s[h] + inc * c_loc
        n_s[h] = dec * n_prev + inc * jnp.sum(wk, axis=0, keepdims=True)
        m_s[h] = m_new

        sl = slice(h * HEAD_V, (h + 1) * HEAD_V)
        y = _rms(hh) * nw_ref[:, sl]
        y = y * _sigmoid(om_ref[:, sl].astype(F32)) * _sigmoid(gm_ref[:, sl].astype(F32))
        o_ref[:, sl] = y.astype(o_ref.dtype)


def _mlstm(qkc, plain, gi, gf, git, gft, ml_norm_w, B, S):
    T, D = qkc.shape
    L = ML_CHUNK
    nc = S // L
    wide = lambda cb: pl.BlockSpec((L, D), lambda b, c: (b * nc + c, cb))
    return pl.pallas_call(
        _mlstm_kernel,
        grid=(B, nc),
        in_specs=[wide(0), wide(1), wide(2), wide(4),
                  pl.BlockSpec((L, HEADS), lambda b, c: (b * nc + c, 0)),
                  pl.BlockSpec((L, HEADS), lambda b, c: (b * nc + c, 0)),
                  pl.BlockSpec((HEADS, L), lambda b, c: (0, b * nc + c)),
                  pl.BlockSpec((HEADS, L), lambda b, c: (0, b * nc + c)),
                  pl.BlockSpec((1, D), lambda b, c: (0, 0))],
        out_specs=pl.BlockSpec((L, D), lambda b, c: (b * nc + c, 0)),
        out_shape=jax.ShapeDtypeStruct((T, D), BF16),
        scratch_shapes=[pltpu.VMEM((HEADS, LANES, HEAD_V), F32),
                        pltpu.VMEM((HEADS, 1, LANES), F32),
                        pltpu.VMEM((HEADS, 1, 1), F32)],
        compiler_params=_cparams(("parallel", "arbitrary")),
        name="mlstm",
    )(qkc, plain, plain, plain, gi, gf, git, gft, ml_norm_w.reshape(1, D))


def _outproj_kernel(ya_ref, ym_ref, x_ref, w_ref, gt_ref, nw_ref, sc_ref, sh_ref, x1_ref, h2t_ref):
    merged = (ya_ref[...].astype(F32) + ym_ref[...].astype(F32)).astype(BF16)
    x1 = x_ref[...] + gt_ref[0] * jnp.dot(merged, w_ref[...], preferred_element_type=F32)
    x1_ref[...] = x1
    h2 = _rms(x1) * nw_ref[...] * (1.0 + sc_ref[0]) + sh_ref[0]
    h2t_ref[...] = h2.T.astype(h2t_ref.dtype)


def _outproj(ya, ym, x2, w_out, norm2_w, mod3, B, tps, tm):
    T, D = x2.shape
    row = lambda k: pl.BlockSpec((1, 1, D), lambda i: (k * B + i // tps, 0, 0))
    tile = pl.BlockSpec((tm, D), lambda i: (i, 0))
    return pl.pallas_call(
        _outproj_kernel,
        grid=(T // tm,),
        in_specs=[tile, tile, tile, pl.BlockSpec((D, D), lambda i: (0, 0)), row(2),
                  pl.BlockSpec((1, D), lambda i: (0, 0)), row(4), row(3)],
        out_specs=[tile, pl.BlockSpec((D, tm), lambda i: (0, i))],
        out_shape=[jax.ShapeDtypeStruct((T, D), F32), jax.ShapeDtypeStruct((D, T), BF16)],
        compiler_params=_cparams(("parallel",)),
        name="outproj",
    )(ya, ym, x2, w_out, mod3, norm2_w.reshape(1, D), mod3, mod3)


def _topk_ranks(s):
    n = s.shape[0]
    it = lax.broadcasted_iota(jnp.int32, s.shape, 0)
    rank = jnp.full(s.shape, float(TOPK), F32)
    x = s
    vals = []
    for r in range(TOPK):
        m = jnp.max(x, axis=0, keepdims=True)
        first = jnp.min(jnp.where(x == m, it, n), axis=0, keepdims=True)
        hit = it == first
        x = jnp.where(hit, -jnp.inf, x)
        rank = jnp.where(hit, float(r), rank)
        vals.append(m)
    return rank, vals


def _topk_ranks_distinct(s):
    rank = jnp.full(s.shape, float(TOPK), F32)
    x = s
    vals = []
    for r in range(TOPK):
        m = jnp.max(x, axis=0, keepdims=True)
        hit = x == m
        x = jnp.where(hit, -jnp.inf, x)
        rank = jnp.where(hit, float(r), rank)
        vals.append(m)
    removed = jnp.sum(jnp.where(x == -jnp.inf, 1.0, 0.0), axis=0, keepdims=True)
    return rank, vals, removed == float(TOPK)


def _peer_sel_kernel(h2t_ref, wqt_ref, keys_ref, r2_ref, e2_ref, a_ref, c_ref, rank_s, vals_s):
    ht = h2t_ref[...]
    scores = []
    for p in range(2):
        qt = jnp.dot(wqt_ref[p * LANES:(p + 1) * LANES, :], ht, preferred_element_type=F32)
        s = jnp.dot(keys_ref[p], qt.astype(BF16), preferred_element_type=F32)
        scores.append(s)
        rank, vals, distinct = _topk_ranks_distinct(s)
        rank_s[p] = rank
        vals_s[p] = jnp.concatenate(vals, axis=0)

        @pl.when(jnp.sum(jnp.where(distinct, 0.0, 1.0)) > 0.0)
        def _():
            rank, vals = _topk_ranks(s)
            rank_s[p] = rank
            vals_s[p] = jnp.concatenate(vals, axis=0)

    s1, s2 = scores
    rank1, rank2 = rank_s[0], rank_s[1]
    v1 = [vals_s[0, r:r + 1, :] for r in range(TOPK)]
    v2 = [vals_s[1, r:r + 1, :] for r in range(TOPK)]

    tt = s1.shape[1]
    rows = [v1[r1] + v2[r2] for (r1, r2) in _CAND]
    rows += [jnp.full((1, tt), -jnp.inf, F32)] * (_CAND_ROWS - len(_CAND))
    cand = jnp.concatenate(rows, axis=0)
    it = lax.broadcasted_iota(jnp.int32, cand.shape, 0)
    sel = jnp.zeros(cand.shape, F32)
    top = v1[0] + v2[0]
    z = jnp.zeros((1, tt), F32)
    x = cand
    for _ in range(TOPK):
        m = jnp.max(x, axis=0, keepdims=True)
        first = jnp.min(jnp.where(x == m, it, _CAND_ROWS), axis=0, keepdims=True)
        hit = it == first
        x = jnp.where(hit, -jnp.inf, x)
        sel = jnp.where(hit, 1.0, sel)
        z = z + jnp.exp(m - top)
    a = jnp.zeros(s1.shape, F32)
    start = 0
    for r1 in range(TOPK):
        width = sum(1 for c in _CAND if c[0] == r1)
        cnt = jnp.sum(sel[start:start + width, :], axis=0, keepdims=True)
        a = jnp.where(rank1 == float(r1), cnt, a)
        start += width

    for ref, val in ((r2_ref, rank2), (e2_ref, jnp.exp(s2 - v2[0])), (a_ref, a), (c_ref, jnp.exp(s1 - v1[0]) / z)):
        for lt in range(tt // LANES):
            ref[lt] = val[:, lt * LANES:(lt + 1) * LANES].astype(ref.dtype)


def _peer_sel(h2t, wqt, keys, tt):
    D, T = h2t.shape
    out = pl.BlockSpec((tt // LANES, N_KEYS, LANES), lambda i, h: (i, h, 0))
    shp = lambda dt: jax.ShapeDtypeStruct((T // LANES, HEADS * N_KEYS, LANES), dt)
    return pl.pallas_call(
        _peer_sel_kernel,
        grid=(T // tt, HEADS),
        in_specs=[pl.BlockSpec((D, tt), lambda i, h: (0, i)),
                  pl.BlockSpec((2 * LANES, D), lambda i, h: (h, 0)),
                  pl.BlockSpec((2, N_KEYS, LANES), lambda i, h: (h, 0, 0))],
        out_specs=[out, out, out, out],
        out_shape=[shp(BF16), shp(BF16), shp(F32), shp(F32)],
        scratch_shapes=[pltpu.VMEM((2, N_KEYS, tt), F32), pltpu.VMEM((2, TOPK, tt), F32)],
        compiler_params=_cparams(("parallel", "parallel")),
        name="peer_sel",
    )(h2t, wqt, keys)


PEER_ROWS = 64
MXU_COLS = 256
ACT_ROWS = 256
OUT_ROWS = 512


def _gate_rows(row, rows):
    packed = jnp.broadcast_to(row, (2 * SUBLANES, LANES)).astype(BF16)
    return jnp.tile(packed, (rows // (2 * SUBLANES), 1))


def _peer_main_kernel(final, h2t_ref, u_ref, vt_ref, r2_ref, e2_ref, a_ref, c_ref, x1_ref, gt_ref, fw_ref,
                      o_ref, acc_ref, act_ref, p_ref):
    e = pl.program_id(1)
    eb, tt = act_ref.shape
    nj = eb // N_KEYS

    @pl.when(e == 0)
    def _():
        acc_ref[...] = jnp.zeros(acc_ref.shape, F32)

    n_half = tt // MXU_COLS
    tcols = lambda th: slice(th * MXU_COLS, (th + 1) * MXU_COLS)

    def act_piece(th, m):
        ms = slice(m * ACT_ROWS, (m + 1) * ACT_ROWS)
        act_ref[ms, tcols(th)] = jnp.dot(u_ref[ms, :], h2t_ref[:, tcols(th)], preferred_element_type=F32)

    def out_piece(th, r):
        rs = slice(r * OUT_ROWS, (r + 1) * OUT_ROWS)
        acc_ref[rs, tcols(th)] += jnp.dot(vt_ref[rs, :], p_ref[:, tcols(th)], preferred_element_type=F32)

    def gate_chunk(tl, bs):
        g = [None] * nj
        for h in range(HEADS):
            rs = slice(h * N_KEYS + bs * PEER_ROWS, h * N_KEYS + (bs + 1) * PEER_ROWS)
            r2c = r2_ref[tl, rs, :]
            e2c = e2_ref[tl, rs, :]
            for j in range(nj):
                idx = h * N_KEYS + e * nj + j
                arow = _gate_rows(a_ref[tl, pl.ds(idx, 1), :], PEER_ROWS)
                crow = _gate_rows(c_ref[tl, pl.ds(idx, 1), :], PEER_ROWS)
                w = jnp.where(r2c < arow, e2c, jnp.zeros_like(e2c)) * crow
                g[j] = w if g[j] is None else g[j] + w
        ls = slice(tl * LANES, (tl + 1) * LANES)
        for j in range(nj):
            es = slice(j * N_KEYS + bs * PEER_ROWS, j * N_KEYS + (bs + 1) * PEER_ROWS)
            aj = act_ref[es, ls]
            gelu = 0.5 * aj * (1.0 + lax.erf(aj * (2.0 ** -0.5)))
            p_ref[es, ls] = g[j] * gelu.astype(BF16)

    n_act = eb // ACT_ROWS
    n_out = acc_ref.shape[0] // OUT_ROWS
    for m in range(n_act):
        act_piece(0, m)
    for th in range(n_half):
        mxu_work = [functools.partial(act_piece, th + 1, m) for m in range(n_act)] if th + 1 < n_half else []
        if th > 0:
            mxu_work += [functools.partial(out_piece, th - 1, r) for r in range(n_out)]
        chunks = [(th * (MXU_COLS // LANES) + lt, bs)
                  for lt in range(MXU_COLS // LANES) for bs in range(N_KEYS // PEER_ROWS)]
        per_chunk = -(-len(mxu_work) // len(chunks))
        for tl, bs in chunks:
            gate_chunk(tl, bs)
            for piece in mxu_work[:per_chunk]:
                piece()
            mxu_work = mxu_work[per_chunk:]
    for r in range(n_out):
        out_piece(n_half - 1, r)

    @pl.when(e == pl.num_programs(1) - 1)
    def _():
        x2 = x1_ref[...] + gt_ref[0] * acc_ref[...].T
        o_ref[...] = _rms(x2) * fw_ref[...] if final else x2


def _peer_main(h2t, u, vt, sel, x1, mod3, fw, final, B, tps, tt, eb):
    D, T = h2t.shape
    ne = u.shape[0]
    wide = pl.BlockSpec((tt // LANES, HEADS * N_KEYS, LANES), lambda i, e: (i, 0, 0))
    return pl.pallas_call(
        functools.partial(_peer_main_kernel, final),
        grid=(T // tt, ne // eb),
        in_specs=[pl.BlockSpec((D, tt), lambda i, e: (0, i)),
                  pl.BlockSpec((eb, D), lambda i, e: (e, 0)),
                  pl.BlockSpec((D, eb), lambda i, e: (0, e)),
                  wide, wide, wide, wide,
                  pl.BlockSpec((tt, D), lambda i, e: (i, 0)),
                  pl.BlockSpec((1, 1, D), lambda i, e: (5 * B + i // tps, 0, 0)),
                  pl.BlockSpec((1, D), lambda i, e: (0, 0))],
        out_specs=pl.BlockSpec((tt, D), lambda i, e: (i, 0)),
        out_shape=jax.ShapeDtypeStruct((T, D), F32),
        scratch_shapes=[pltpu.VMEM((D, tt), F32),
                        pltpu.VMEM((eb, tt), F32),
                        pltpu.VMEM((eb, tt), BF16)],
        compiler_params=_cparams(("parallel", "arbitrary")),
        name="peer_main",
    )(h2t, u, vt, *sel, x1, mod3, fw.reshape(1, D))


def _rope_tables(S):
    inv = ROPE_THETA ** (-jnp.arange(ROPE_HALF, dtype=F32) * 2.0 / ROPE_DIM)
    ang = jnp.arange(S, dtype=jnp.int32).astype(F32)[:, None] * inv[None, :]
    cos, sin = jnp.cos(ang), jnp.sin(ang)
    zeros = jnp.zeros((S, DA_QK - ROPE_DIM), F32)
    z8 = jnp.zeros((S, ROPE_HALF), F32)
    cosf = jnp.concatenate([cos, cos, zeros + 1.0] * 2, axis=1)
    sa = jnp.concatenate([-sin, z8, zeros] * 2, axis=1)
    sb = jnp.concatenate([z8, sin, zeros] * 2, axis=1)
    return cosf, sa, sb


def kernel(x, c, ada_w, ada_b, norm1_w, norm2_w, w_in, conv_w, conv_b, ml_i_bias, ml_f_bias, ml_norm_w, lam_q1, lam_k1, lam_q2, lam_k2, subln_w, w_out, peer_wq, peer_keys, peer_u, peer_v, final_norm_w):
    B, S, D = x.shape
    assert D == D_MODEL and S % 512 == 0
    T = B * S
    depth = w_in.shape[0]
    tm = 512
    tps = S // tm
    tables = _rope_tables(S)
    x2 = x.reshape(T, D)
    for l in range(depth):
        mod3 = _mod(c, ada_w[l], ada_b[l])
        h = _norm(x2, norm1_w[l], mod3, B, tps, tm)

        wl = w_in[l]
        o = 0
        cols = {}
        for name, width in (("qa", D), ("ka", D), ("va", D), ("qm", D // 2), ("km", D // 2), ("vm", D),
                            ("om", D), ("ip", HEADS), ("fp", HEADS), ("ga", D), ("gm", D)):
            cols[name] = wl[:, o:o + width]
            o += width
        w_rope = jnp.concatenate([cols["qa"], cols["ka"]], axis=1).astype(BF16)
        w_conv = jnp.concatenate([cols["qm"], cols["km"]], axis=1).astype(BF16)
        w_plain = jnp.concatenate([cols[n] for n in ("va", "vm", "om", "ga", "gm")], axis=1).astype(BF16)

        qk = _rope_proj(h, w_rope, tables, tps, tm)
        plain = _proj(h, w_plain, tm, D)
        qkc, gi, gf, git, gft = _conv_proj(h, w_conv, cols["ip"].astype(BF16), cols["fp"].astype(BF16),
                                           conv_w[l], conv_b[l], ml_i_bias[l], ml_f_bias[l], tps, tm)

        lam_init = 0.8 - 0.6 * math.exp(-0.3 * l)
        ya = _attention(qk, plain, (lam_q1[l], lam_k1[l], lam_q2[l], lam_k2[l]), subln_w[l], lam_init, B, S, ATTN_TQ, ATTN_TK)
        ym = _mlstm(qkc, plain, gi, gf, git, gft, ml_norm_w[l], B, S)
        x1, h2t = _outproj(ya, ym, x2, w_out[l].astype(BF16), norm2_w[l], mod3, B, tps, tm)

        wqt = peer_wq[l].T.astype(BF16)
        keys = peer_keys[l].reshape(2 * HEADS, N_KEYS, LANES).astype(BF16)
        sel = _peer_sel(h2t, wqt, keys, tm)
        x2 = _peer_main(h2t, peer_u[l].astype(BF16), peer_v[l].T.astype(BF16), sel, x1, mod3,
                        final_norm_w, l == depth - 1, B, tps, tm, 8 * N_KEYS)
    if depth == 0:
        raise ValueError("depth must be positive")
    return x2.reshape(B, S, D)
```

```python
import functools
import math

import jax
import jax.numpy as jnp
from jax import lax
from jax.experimental import pallas as pl
from jax.experimental.pallas import tpu as pltpu

F32 = jnp.float32
BF16 = jnp.bfloat16
HIGHEST = lax.Precision.HIGHEST

NORM_EPS = 1e-6
LOG2E = 1.4426950408889634
D_MODEL = 1024
HEADS = 8
HEAD_V = 128
DA_QK = 64
ROPE_DIM = 16
ROPE_HALF = 8
ROPE_THETA = 500000.0
ML_QK = 64
CONV_WIDTH = 4
ML_CHUNK = 128
ATTN_TQ = 512
ATTN_TK = 512
N_KEYS = 128
TOPK = 16
LANES = 128
SUBLANES = 8
VMEM_LIMIT = 48 * 1024 * 1024

_CAND = [(r1, r2) for r1 in range(TOPK) for r2 in range(TOPK) if (r1 + 1) * (r2 + 1) <= TOPK]
_CAND_ROWS = -(-len(_CAND) // SUBLANES) * SUBLANES


def _cparams(sem):
    return pltpu.CompilerParams(dimension_semantics=sem, vmem_limit_bytes=VMEM_LIMIT)


def _rms(x):
    return x * lax.rsqrt(jnp.mean(x * x, axis=-1, keepdims=True) + NORM_EPS)


def _sigmoid(x):
    return 1.0 / (1.0 + jnp.exp(-x))


def _log_sigmoid(x):
    return jnp.minimum(x, 0.0) - jnp.log(1.0 + jnp.exp(-jnp.abs(x)))


def _mod_kernel(c_ref, w_ref, b_ref, o_ref):
    c = c_ref[...]
    cond = c * _sigmoid(c)
    o_ref[0] = jnp.dot(cond, w_ref[...], precision=HIGHEST, preferred_element_type=F32) + b_ref[...]


def _mod(c, ada_w, ada_b):
    B, D = c.shape
    out = pl.pallas_call(
        _mod_kernel,
        grid=(6,),
        in_specs=[pl.BlockSpec((B, D), lambda j: (0, 0)),
                  pl.BlockSpec((D, D), lambda j: (0, j)),
                  pl.BlockSpec((1, D), lambda j: (0, j))],
        out_specs=pl.BlockSpec((1, B, D), lambda j: (j, 0, 0)),
        out_shape=jax.ShapeDtypeStruct((6, B, D), F32),
        compiler_params=_cparams(("parallel",)),
        name="mod",
    )(c, ada_w, ada_b.reshape(1, 6 * D))
    return out.reshape(6 * B, 1, D)


def _norm_kernel(x_ref, w_ref, sc_ref, sh_ref, o_ref):
    y = _rms(x_ref[...]) * w_ref[...]
    o_ref[...] = (y * (1.0 + sc_ref[0]) + sh_ref[0]).astype(o_ref.dtype)


def _norm(x2, w, mod3, B, tps, tm):
    T, D = x2.shape
    return pl.pallas_call(
        _norm_kernel,
        grid=(T // tm,),
        in_specs=[pl.BlockSpec((tm, D), lambda i: (i, 0)),
                  pl.BlockSpec((1, D), lambda i: (0, 0)),
                  pl.BlockSpec((1, 1, D), lambda i: (1 * B + i // tps, 0, 0)),
                  pl.BlockSpec((1, 1, D), lambda i: (0 * B + i // tps, 0, 0))],
        out_specs=pl.BlockSpec((tm, D), lambda i: (i, 0)),
        out_shape=jax.ShapeDtypeStruct((T, D), BF16),
        compiler_params=_cparams(("parallel",)),
        name="norm1",
    )(x2, w.reshape(1, D), mod3, mod3)


def _proj_kernel(h_ref, w_ref, o_ref):
    o_ref[...] = jnp.dot(h_ref[...], w_ref[...], preferred_element_type=F32).astype(o_ref.dtype)


def _proj(h, w, tm, tn):
    T, D = h.shape
    N = w.shape[1]
    return pl.pallas_call(
        _proj_kernel,
        grid=(N // tn, T // tm),
        in_specs=[pl.BlockSpec((tm, D), lambda j, i: (i, 0)),
                  pl.BlockSpec((D, tn), lambda j, i: (0, j))],
        out_specs=pl.BlockSpec((tm, tn), lambda j, i: (i, j)),
        out_shape=jax.ShapeDtypeStruct((T, N), BF16),
        compiler_params=_cparams(("parallel", "parallel")),
        name="proj_plain",
    )(h, w)


def _rope_proj_kernel(h_ref, w_ref, cos_ref, sa_ref, sb_ref, o_ref):
    acc = jnp.dot(h_ref[...], w_ref[...], preferred_element_type=F32)
    scale = jnp.where(pl.program_id(0) == 0, DA_QK ** -0.5 * LOG2E, 1.0).astype(F32)
    cosf, sa, sb = cos_ref[...], sa_ref[...], sb_ref[...]
    for hh in range(HEADS):
        blk = acc[:, hh * LANES:(hh + 1) * LANES]
        rot = (blk * cosf + pltpu.roll(blk, LANES - ROPE_HALF, 1) * sa
               + pltpu.roll(blk, ROPE_HALF, 1) * sb)
        o_ref[:, hh * LANES:(hh + 1) * LANES] = (rot * scale).astype(o_ref.dtype)


def _rope_proj(h, w, tables, tps, tm):
    T, D = h.shape
    cosf, sa, sb = tables
    tab_spec = pl.BlockSpec((tm, LANES), lambda j, i: (i % tps, 0))
    return pl.pallas_call(
        _rope_proj_kernel,
        grid=(2, T // tm),
        in_specs=[pl.BlockSpec((tm, D), lambda j, i: (i, 0)),
                  pl.BlockSpec((D, D), lambda j, i: (0, j)),
                  tab_spec, tab_spec, tab_spec],
        out_specs=pl.BlockSpec((tm, D), lambda j, i: (i, j)),
        out_shape=jax.ShapeDtypeStruct((T, 2 * D), BF16),
        compiler_params=_cparams(("parallel", "parallel")),
        name="proj_rope",
    )(h, w, cosf, sa, sb)


def _conv_proj_kernel(tps, h_ref, w_ref, wgi_ref, wgf_ref, wgit_ref, wgft_ref, cw_ref, cb_ref,
                      bi_ref, bf_ref, bit_ref, bft_ref,
                      o_ref, gi_ref, gf_ref, git_ref, gft_ref, buf):
    i = pl.program_id(0)
    tm = h_ref.shape[0]
    h = h_ref[...]
    acc = jnp.dot(h, w_ref[...], preferred_element_type=F32)

    @pl.when(i % tps == 0)
    def _():
        buf[0:SUBLANES, :] = jnp.zeros((SUBLANES, acc.shape[1]), F32)

    buf[SUBLANES:SUBLANES + tm, :] = acc
    cw = cw_ref[...]
    y = cb_ref[...]
    for j in range(CONV_WIDTH):
        off = SUBLANES - (CONV_WIDTH - 1) + j
        y = y + buf[off:off + tm, :] * cw[j:j + 1, :]
    buf[0:SUBLANES, :] = buf[tm:tm + SUBLANES, :]
    y = y * _sigmoid(y)
    lane = lax.broadcasted_iota(jnp.int32, (1, y.shape[1]), 1)
    kscale = jnp.where(lane >= HEADS * ML_QK, ML_QK ** -0.5, 1.0).astype(F32)
    o_ref[...] = (y * kscale).astype(o_ref.dtype)

    gi_ref[...] = jnp.dot(h, wgi_ref[...], preferred_element_type=F32)[:, :HEADS] + bi_ref[...]
    gf_ref[...] = jnp.dot(h, wgf_ref[...], preferred_element_type=F32)[:, :HEADS] + bf_ref[...]
    nt = (((1,), (1,)), ((), ()))
    git_ref[...] = lax.dot_general(wgit_ref[...], h, nt, preferred_element_type=F32) + bit_ref[...]
    gft_ref[...] = lax.dot_general(wgft_ref[...], h, nt, preferred_element_type=F32) + bft_ref[...]


def _conv_proj(h, w, wgi, wgf, conv_w, conv_b, bi, bf, tps, tm):
    T, D = h.shape
    N = w.shape[1]
    pad = lambda a: jnp.pad(a, ((0, 0), (0, LANES - a.shape[1])))
    full = lambda shape: pl.BlockSpec(shape, lambda i: tuple(0 for _ in shape))
    return pl.pallas_call(
        functools.partial(_conv_proj_kernel, tps),
        grid=(T // tm,),
        in_specs=[pl.BlockSpec((tm, D), lambda i: (i, 0)),
                  full((D, N)), full((D, LANES)), full((D, LANES)),
                  full((HEADS, D)), full((HEADS, D)),
                  full((CONV_WIDTH, N)), full((1, N)),
                  full((1, HEADS)), full((1, HEADS)), full((HEADS, 1)), full((HEADS, 1))],
        out_specs=[pl.BlockSpec((tm, N), lambda i: (i, 0)),
                   pl.BlockSpec((tm, HEADS), lambda i: (i, 0)),
                   pl.BlockSpec((tm, HEADS), lambda i: (i, 0)),
                   pl.BlockSpec((HEADS, tm), lambda i: (0, i)),
                   pl.BlockSpec((HEADS, tm), lambda i: (0, i))],
        out_shape=[jax.ShapeDtypeStruct((T, N), BF16),
                   jax.ShapeDtypeStruct((T, HEADS), F32),
                   jax.ShapeDtypeStruct((T, HEADS), F32),
                   jax.ShapeDtypeStruct((HEADS, T), F32),
                   jax.ShapeDtypeStruct((HEADS, T), F32)],
        scratch_shapes=[pltpu.VMEM((tm + 2 * SUBLANES, N), F32)],
        compiler_params=_cparams(("arbitrary",)),
        name="proj_conv",
    )(h, w, pad(wgi), pad(wgf), wgi.T, wgf.T, conv_w, conv_b.reshape(1, N),
      bi.reshape(1, HEADS), bf.reshape(1, HEADS), bi.reshape(HEADS, 1), bf.reshape(HEADS, 1))


def _attn_kernel(lam_init, q_ref, k_ref, v_ref, ga_ref, lq1_ref, lk1_ref, lq2_ref, lk2_ref, sw_ref,
                 o_ref, qt_s, vt_s, sa_s, sb_s, m_s, l_s, acc_s):
    qi = pl.program_id(2)
    tq = q_ref.shape[0]
    tk = vt_s.shape[2]
    ratio = tq // tk

    @pl.when(qi == 0)
    def _():
        for kk in range(vt_s.shape[0]):
            vt_s[kk] = v_ref[kk * tk:(kk + 1) * tk, :].T

    q = q_ref[...].astype(F32)
    lane = lax.broadcasted_iota(jnp.int32, q.shape, 1)
    qt_s[0] = jnp.where(lane < DA_QK, q, 0.0).T.astype(BF16)
    qt_s[1] = jnp.where(lane >= DA_QK, q, 0.0).T.astype(BF16)
    m_s[...] = jnp.full(m_s.shape, -jnp.inf, F32)
    l_s[...] = jnp.zeros(l_s.shape, F32)
    acc_s[...] = jnp.zeros(acc_s.shape, F32)

    def scores(kk, st_ref):
        k = k_ref[pl.ds(pl.multiple_of(kk * tk, tk), tk), :]
        for c in range(2):
            st_ref[c] = jnp.dot(k, qt_s[c], preferred_element_type=F32)

    def process(kk, st_ref, diag_offset):
        vt = vt_s[kk]
        for c in range(2):
            st = st_ref[c]
            if diag_offset is not None:
                key = lax.broadcasted_iota(jnp.int32, st.shape, 0) + diag_offset
                qry = lax.broadcasted_iota(jnp.int32, st.shape, 1)
                st = jnp.where(key <= qry, st, -jnp.inf)
            m_prev = m_s[c]
            m_new = jnp.maximum(m_prev, jnp.max(st, axis=0, keepdims=True))
            alpha = jnp.exp2(m_prev - m_new)
            p = jnp.exp2(st - m_new)
            l_s[c] = alpha * l_s[c] + jnp.sum(p, axis=0, keepdims=True)
            acc_s[c] = alpha * acc_s[c] + jnp.dot(vt, p.astype(BF16), preferred_element_type=F32)
            m_s[c] = m_new

    def diagonal(first, cur, nxt):
        for d in range(ratio):
            if d + 1 < ratio:
                scores(first + d + 1, nxt)
            process(first + d, cur, d * tk)
            cur, nxt = nxt, cur

    n_below = qi * ratio
    scores(0, sa_s)

    def pair(i, carry):
        kk = 2 * i
        scores(kk + 1, sb_s)
        process(kk, sa_s, None)
        scores(kk + 2, sa_s)
        process(kk + 1, sb_s, None)
        return carry

    lax.fori_loop(0, n_below // 2, pair, 0)
    odd = lax.rem(n_below, 2) == 1

    @pl.when(odd)
    def _():
        scores(n_below, sb_s)
        process(n_below - 1, sa_s, None)
        diagonal(n_below, sb_s, sa_s)

    @pl.when(jnp.logical_not(odd))
    def _():
        diagonal(n_below, sa_s, sb_s)

    lam = (jnp.exp(jnp.sum(lq1_ref[...] * lk1_ref[...], axis=-1, keepdims=True))
           - jnp.exp(jnp.sum(lq2_ref[...] * lk2_ref[...], axis=-1, keepdims=True)) + lam_init)
    o = (acc_s[0] / l_s[0] - lam * (acc_s[1] / l_s[1])).T
    o = _rms(o) * sw_ref[...] * (1.0 - lam_init)
    o_ref[...] = (o * _sigmoid(ga_ref[...].astype(F32))).astype(o_ref.dtype)


def _attention(qk, plain, lam_vecs, subln_w, lam_init, B, S, tq, tk):
    T = qk.shape[0]
    nq = S // tq
    vec = pl.BlockSpec((1, DA_QK), lambda b, h, i: (0, 0))
    return pl.pallas_call(
        functools.partial(_attn_kernel, lam_init),
        grid=(B, HEADS, nq),
        in_specs=[pl.BlockSpec((tq, LANES), lambda b, h, i: (b * nq + i, h)),
                  pl.BlockSpec((S, LANES), lambda b, h, i: (b, HEADS + h)),
                  pl.BlockSpec((S, LANES), lambda b, h, i: (b, h)),
                  pl.BlockSpec((tq, LANES), lambda b, h, i: (b * nq + i, 3 * HEADS + h)),
                  vec, vec, vec, vec,
                  pl.BlockSpec((1, HEAD_V), lambda b, h, i: (0, 0))],
        out_specs=pl.BlockSpec((tq, LANES), lambda b, h, i: (b * nq + i, h)),
        out_shape=jax.ShapeDtypeStruct((T, D_MODEL), BF16),
        scratch_shapes=[pltpu.VMEM((2, LANES, tq), BF16),
                        pltpu.VMEM((S // tk, HEAD_V, tk), BF16),
                        pltpu.VMEM((2, tk, tq), F32),
                        pltpu.VMEM((2, tk, tq), F32),
                        pltpu.VMEM((2, 1, tq), F32),
                        pltpu.VMEM((2, 1, tq), F32),
                        pltpu.VMEM((2, HEAD_V, tq), F32)],
        compiler_params=_cparams(("parallel", "parallel", "arbitrary")),
        name="attn",
    )(qk, qk, plain, plain, *[v.reshape(1, DA_QK) for v in lam_vecs], subln_w.reshape(1, HEAD_V))


def _mlstm_kernel(qk_ref, v_ref, om_ref, gm_ref, gi_ref, gf_ref, git_ref, gft_ref, nw_ref,
                  o_ref, c_s, n_s, m_s):
    L = qk_ref.shape[0]

    @pl.when(pl.program_id(1) == 0)
    def _():
        c_s[...] = jnp.zeros(c_s.shape, F32)
        n_s[...] = jnp.zeros(n_s.shape, F32)
        m_s[...] = jnp.zeros(m_s.shape, F32)

    row = lax.broadcasted_iota(jnp.int32, (L, L), 0)
    col = lax.broadcasted_iota(jnp.int32, (L, L), 1)
    tri = (col <= row).astype(F32)
    causal_t = row <= col
    tri_t = causal_t.astype(F32)
    bcols = jnp.dot(tri, _log_sigmoid(gf_ref[...]), precision=HIGHEST, preferred_element_type=F32)
    brows = jnp.dot(_log_sigmoid(gft_ref[...]), tri_t, precision=HIGHEST, preferred_element_type=F32)
    ucols = gi_ref[...] - bcols
    git = git_ref[...]
    lane = lax.broadcasted_iota(jnp.int32, (1, LANES), 1)
    first_row = lax.broadcasted_iota(jnp.int32, (SUBLANES, 1), 0) == 0
    nt = (((1,), (1,)), ((), ()))

    for h in range(HEADS):
        p = h // 2
        hmask = ((lane >= (h % 2) * ML_QK) & (lane < (h % 2 + 1) * ML_QK)).astype(F32)
        qh = (qk_ref[:, p * LANES:(p + 1) * LANES].astype(F32) * hmask).astype(BF16)
        kp = qk_ref[:, (HEADS // 2 + p) * LANES:(HEADS // 2 + p + 1) * LANES]
        vt = v_ref[:, h * HEAD_V:(h + 1) * HEAD_V].T
        br = brows[h:h + 1, :]
        igr = git[h:h + 1, :]
        g_tot = br[:, L - 1:L]
        m_prev = m_s[h]
        ct_prev = c_s[h]
        n_prev = n_s[h]

        dm = jnp.where(causal_t, br + ucols[:, h:h + 1], -jnp.inf)
        m_inter = br + m_prev
        m_j = jnp.maximum(jnp.max(dm, axis=0, keepdims=True), m_inter)
        st = lax.dot_general(kp, qh, nt, preferred_element_type=F32)
        qkw = st * jnp.exp(dm - m_j)
        inter_w = jnp.exp(m_inter - m_j)
        num = (jnp.dot(vt, qkw.astype(BF16), preferred_element_type=F32)
               + inter_w * lax.dot_general(ct_prev.astype(BF16), qh, nt, preferred_element_type=F32))
        qn = lax.dot_general(n_prev.astype(BF16), qh, nt, preferred_element_type=F32)[0:1, :]
        den = jnp.sum(qkw, axis=0, keepdims=True) + inter_w * qn
        ht = num / jnp.maximum(jnp.abs(den), jnp.exp(-m_j))

        a = g_tot - br + igr
        m_loc = jnp.max(a, axis=-1, keepdims=True)
        w_loc = jnp.exp(a - m_loc)
        m_new = jnp.maximum(g_tot + m_prev, m_loc)
        dec = jnp.exp(g_tot + m_prev - m_new)
        inc = jnp.exp(m_loc - m_new)
        c_loc = jnp.dot((vt.astype(F32) * w_loc).astype(BF16), kp, preferred_element_type=F32)
        w8 = jnp.where(first_row, w_loc, 0.0)
        n_loc = jnp.dot(w8, kp.astype(F32), precision=HIGHEST, preferred_element_type=F32)
        c_s[h] = dec * ct_prev + inc * c_loc
        n_s[h] = dec * n_prev + inc * n_loc
        m_s[h] = m_new

        sl = slice(h * HEAD_V, (h + 1) * HEAD_V)
        yt = ht * lax.rsqrt(jnp.mean(ht * ht, axis=0, keepdims=True) + NORM_EPS)
        y = yt.T * nw_ref[:, sl]
        y = y * _sigmoid(om_ref[:, sl].astype(F32)) * _sigmoid(gm_ref[:, sl].astype(F32))
        o_ref[:, sl] = y.astype(o_ref.dtype)


def _mlstm(qkc, plain, gi, gf, git, gft, ml_norm_w, B, S):
    T, D = qkc.shape
    L = ML_CHUNK
    nc = S // L
    wide = lambda cb: pl.BlockSpec((L, D), lambda b, c: (b * nc + c, cb))
    return pl.pallas_call(
        _mlstm_kernel,
        grid=(B, nc),
        in_specs=[wide(0), wide(1), wide(2), wide(4),
                  pl.BlockSpec((L, HEADS), lambda b, c: (b * nc + c, 0)),
                  pl.BlockSpec((L, HEADS), lambda b, c: (b * nc + c, 0)),
                  pl.BlockSpec((HEADS, L), lambda b, c: (0, b * nc + c)),
                  pl.BlockSpec((HEADS, L), lambda b, c: (0, b * nc + c)),
                  pl.BlockSpec((1, D), lambda b, c: (0, 0))],
        out_specs=pl.BlockSpec((L, D), lambda b, c: (b * nc + c, 0)),
        out_shape=jax.ShapeDtypeStruct((T, D), BF16),
        scratch_shapes=[pltpu.VMEM((HEADS, LANES, HEAD_V), F32),
                        pltpu.VMEM((HEADS, SUBLANES, LANES), F32),
                        pltpu.VMEM((HEADS, 1, 1), F32)],
        compiler_params=_cparams(("parallel", "arbitrary")),
        name="mlstm",
    )(qkc, plain, plain, plain, gi, gf, git, gft, ml_norm_w.reshape(1, D))


def _outproj_kernel(ya_ref, ym_ref, x_ref, w_ref, gt_ref, nw_ref, sc_ref, sh_ref, x1_ref, h2t_ref):
    merged = (ya_ref[...].astype(F32) + ym_ref[...].astype(F32)).astype(BF16)
    x1 = x_ref[...] + gt_ref[0] * jnp.dot(merged, w_ref[...], preferred_element_type=F32)
    x1_ref[...] = x1
    h2 = _rms(x1) * nw_ref[...] * (1.0 + sc_ref[0]) + sh_ref[0]
    h2t_ref[...] = h2.T.astype(h2t_ref.dtype)


def _outproj(ya, ym, x2, w_out, norm2_w, mod3, B, tps, tm):
    T, D = x2.shape
    row = lambda k: pl.BlockSpec((1, 1, D), lambda i: (k * B + i // tps, 0, 0))
    tile = pl.BlockSpec((tm, D), lambda i: (i, 0))
    return pl.pallas_call(
        _outproj_kernel,
        grid=(T // tm,),
        in_specs=[tile, tile, tile, pl.BlockSpec((D, D), lambda i: (0, 0)), row(2),
                  pl.BlockSpec((1, D), lambda i: (0, 0)), row(4), row(3)],
        out_specs=[tile, pl.BlockSpec((D, tm), lambda i: (0, i))],
        out_shape=[jax.ShapeDtypeStruct((T, D), F32), jax.ShapeDtypeStruct((D, T), BF16)],
        compiler_params=_cparams(("parallel",)),
        name="outproj",
    )(ya, ym, x2, w_out, mod3, norm2_w.reshape(1, D), mod3, mod3)


def _topk_ranks(s):
    n = s.shape[0]
    it = lax.broadcasted_iota(jnp.int32, s.shape, 0)
    rank = jnp.full(s.shape, float(TOPK), F32)
    x = s
    vals = []
    for r in range(TOPK):
        m = jnp.max(x, axis=0, keepdims=True)
        first = jnp.min(jnp.where(x == m, it, n), axis=0, keepdims=True)
        hit = it == first
        x = jnp.where(hit, -jnp.inf, x)
        rank = jnp.where(hit, float(r), rank)
        vals.append(m)
    return rank, vals


MARK_BASE = 3.0e38
MARK_STEP = 1.0e36


def _topk_ranks_distinct(s):
    x = s
    vals = []
    for r in range(TOPK):
        m = jnp.max(x, axis=0, keepdims=True)
        x = jnp.where(x == m, -(MARK_BASE + r * MARK_STEP), x)
        vals.append(m)
    taken = x <= -MARK_BASE
    rank = jnp.where(taken, jnp.round((-x - MARK_BASE) * (1.0 / MARK_STEP)), float(TOPK))
    removed = jnp.sum(jnp.where(taken, 1.0, 0.0), axis=0, keepdims=True)
    return rank, vals, removed == float(TOPK)


def _peer_sel_kernel(h2t_ref, wqt_ref, keys_ref, r2_ref, e2_ref, a_ref, c_ref, rank_s, vals_s):
    ht = h2t_ref[...]
    scores = []
    suspect = []
    for p in range(2):
        qt = jnp.dot(wqt_ref[p * LANES:(p + 1) * LANES, :], ht, preferred_element_type=F32)
        s = jnp.dot(keys_ref[p], qt.astype(BF16), preferred_element_type=F32)
        scores.append(s)
        rank, vals, distinct = _topk_ranks_distinct(s)
        rank_s[p] = rank
        vals_s[p] = jnp.concatenate(vals, axis=0)
        suspect.append(jnp.sum(jnp.where(distinct, 0.0, 1.0)) > 0.0)

    for p in range(2):
        @pl.when(suspect[p])
        def _():
            rank, vals = _topk_ranks(scores[p])
            rank_s[p] = rank
            vals_s[p] = jnp.concatenate(vals, axis=0)

    s1, s2 = scores
    rank1, rank2 = rank_s[0], rank_s[1]
    v1 = [vals_s[0, r:r + 1, :] for r in range(TOPK)]
    v2 = [vals_s[1, r:r + 1, :] for r in range(TOPK)]

    tt = s1.shape[1]
    rows = [v1[r1] + v2[r2] for (r1, r2) in _CAND]
    rows += [jnp.full((1, tt), -jnp.inf, F32)] * (_CAND_ROWS - len(_CAND))
    cand = jnp.concatenate(rows, axis=0)
    it = lax.broadcasted_iota(jnp.int32, cand.shape, 0)
    sel = jnp.zeros(cand.shape, F32)
    top = v1[0] + v2[0]
    z = jnp.zeros((1, tt), F32)
    x = cand
    for _ in range(TOPK):
        m = jnp.max(x, axis=0, keepdims=True)
        first = jnp.min(jnp.where(x == m, it, _CAND_ROWS), axis=0, keepdims=True)
        hit = it == first
        x = jnp.where(hit, -jnp.inf, x)
        sel = jnp.where(hit, 1.0, sel)
        z = z + jnp.exp(m - top)
    a = jnp.zeros(s1.shape, F32)
    start = 0
    for r1 in range(TOPK):
        width = sum(1 for c in _CAND if c[0] == r1)
        cnt = jnp.sum(sel[start:start + width, :], axis=0, keepdims=True)
        a = jnp.where(rank1 == float(r1), cnt, a)
        start += width

    for ref, val in ((r2_ref, rank2), (e2_ref, jnp.exp(s2 - v2[0])), (a_ref, a), (c_ref, jnp.exp(s1 - v1[0]) / z)):
        for lt in range(tt // LANES):
            ref[lt] = val[:, lt * LANES:(lt + 1) * LANES].astype(ref.dtype)


def _peer_sel(h2t, wqt, keys, tt):
    D, T = h2t.shape
    out = pl.BlockSpec((tt // LANES, N_KEYS, LANES), lambda i, h: (i, h, 0))
    shp = lambda dt: jax.ShapeDtypeStruct((T // LANES, HEADS * N_KEYS, LANES), dt)
    return pl.pallas_call(
        _peer_sel_kernel,
        grid=(T // tt, HEADS),
        in_specs=[pl.BlockSpec((D, tt), lambda i, h: (0, i)),
                  pl.BlockSpec((2 * LANES, D), lambda i, h: (h, 0)),
                  pl.BlockSpec((2, N_KEYS, LANES), lambda i, h: (h, 0, 0))],
        out_specs=[out, out, out, out],
        out_shape=[shp(BF16), shp(BF16), shp(F32), shp(F32)],
        scratch_shapes=[pltpu.VMEM((2, N_KEYS, tt), F32), pltpu.VMEM((2, TOPK, tt), F32)],
        compiler_params=_cparams(("parallel", "parallel")),
        name="peer_sel",
    )(h2t, wqt, keys)


PEER_ROWS = 64
MXU_COLS = 256
ACT_ROWS = 256
OUT_ROWS = 512


def _gate_rows(row, rows):
    packed = jnp.broadcast_to(row, (2 * SUBLANES, LANES)).astype(BF16)
    return jnp.tile(packed, (rows // (2 * SUBLANES), 1))


def _peer_main_kernel(final, h2t_ref, u_ref, vt_ref, r2_ref, e2_ref, a_ref, c_ref, x1_ref, gt_ref, fw_ref,
                      o_ref, acc_ref, act_ref, p_ref):
    e = pl.program_id(1)
    eb, tt = act_ref.shape
    nj = eb // N_KEYS

    @pl.when(e == 0)
    def _():
        acc_ref[...] = jnp.zeros(acc_ref.shape, F32)

    n_half = tt // MXU_COLS
    tcols = lambda th: slice(th * MXU_COLS, (th + 1) * MXU_COLS)

    def act_piece(th, m):
        ms = slice(m * ACT_ROWS, (m + 1) * ACT_ROWS)
        act_ref[ms, tcols(th)] = jnp.dot(u_ref[ms, :], h2t_ref[:, tcols(th)], preferred_element_type=F32)

    def out_piece(th, r):
        rs = slice(r * OUT_ROWS, (r + 1) * OUT_ROWS)
        acc_ref[rs, tcols(th)] += jnp.dot(vt_ref[rs, :], p_ref[:, tcols(th)], preferred_element_type=F32)

    def gate_chunk(tl, bs):
        g = [None] * nj
        for h in range(HEADS):
            rs = slice(h * N_KEYS + bs * PEER_ROWS, h * N_KEYS + (bs + 1) * PEER_ROWS)
            r2c = r2_ref[tl, rs, :]
            e2c = e2_ref[tl, rs, :]
            for j in range(nj):
                idx = h * N_KEYS + e * nj + j
                arow = _gate_rows(a_ref[tl, pl.ds(idx, 1), :], PEER_ROWS)
                crow = _gate_rows(c_ref[tl, pl.ds(idx, 1), :], PEER_ROWS)
                w = jnp.where(r2c < arow, e2c, jnp.zeros_like(e2c)) * crow
                g[j] = w if g[j] is None else g[j] + w
        ls = slice(tl * LANES, (tl + 1) * LANES)
        for j in range(nj):
            es = slice(j * N_KEYS + bs * PEER_ROWS, j * N_KEYS + (bs + 1) * PEER_ROWS)
            aj = act_ref[es, ls]
            gelu = 0.5 * aj * (1.0 + lax.erf(aj * (2.0 ** -0.5)))
            p_ref[es, ls] = g[j] * gelu.astype(BF16)

    n_act = eb // ACT_ROWS
    n_out = acc_ref.shape[0] // OUT_ROWS
    for m in range(n_act):
        act_piece(0, m)
    for th in range(n_half):
        mxu_work = [functools.partial(act_piece, th + 1, m) for m in range(n_act)] if th + 1 < n_half else []
        if th > 0:
            mxu_work += [functools.partial(out_piece, th - 1, r) for r in range(n_out)]
        chunks = [(th * (MXU_COLS // LANES) + lt, bs)
                  for lt in range(MXU_COLS // LANES) for bs in range(N_KEYS // PEER_ROWS)]
        per_chunk = -(-len(mxu_work) // len(chunks))
        for tl, bs in chunks:
            gate_chunk(tl, bs)
            for piece in mxu_work[:per_chunk]:
                piece()
            mxu_work = mxu_work[per_chunk:]
    for r in range(n_out):
        out_piece(n_half - 1, r)

    @pl.when(e == pl.num_programs(1) - 1)
    def _():
        x2 = x1_ref[...] + gt_ref[0] * acc_ref[...].T
        o_ref[...] = _rms(x2) * fw_ref[...] if final else x2


def _peer_main(h2t, u, vt, sel, x1, mod3, fw, final, B, tps, tt, eb):
    D, T = h2t.shape
    ne = u.shape[0]
    wide = pl.BlockSpec((tt // LANES, HEADS * N_KEYS, LANES), lambda i, e: (i, 0, 0))
    return pl.pallas_call(
        functools.partial(_peer_main_kernel, final),
        grid=(T // tt, ne // eb),
        in_specs=[pl.BlockSpec((D, tt), lambda i, e: (0, i)),
                  pl.BlockSpec((eb, D), lambda i, e: (e, 0)),
                  pl.BlockSpec((D, eb), lambda i, e: (0, e)),
                  wide, wide, wide, wide,
                  pl.BlockSpec((tt, D), lambda i, e: (i, 0)),
                  pl.BlockSpec((1, 1, D), lambda i, e: (5 * B + i // tps, 0, 0)),
                  pl.BlockSpec((1, D), lambda i, e: (0, 0))],
        out_specs=pl.BlockSpec((tt, D), lambda i, e: (i, 0)),
        out_shape=jax.ShapeDtypeStruct((T, D), F32),
        scratch_shapes=[pltpu.VMEM((D, tt), F32),
                        pltpu.VMEM((eb, tt), F32),
                        pltpu.VMEM((eb, tt), BF16)],
        compiler_params=_cparams(("parallel", "arbitrary")),
        name="peer_main",
    )(h2t, u, vt, *sel, x1, mod3, fw.reshape(1, D))


def _rope_tables(S):
    inv = ROPE_THETA ** (-jnp.arange(ROPE_HALF, dtype=F32) * 2.0 / ROPE_DIM)
    ang = jnp.arange(S, dtype=jnp.int32).astype(F32)[:, None] * inv[None, :]
    cos, sin = jnp.cos(ang), jnp.sin(ang)
    zeros = jnp.zeros((S, DA_QK - ROPE_DIM), F32)
    z8 = jnp.zeros((S, ROPE_HALF), F32)
    cosf = jnp.concatenate([cos, cos, zeros + 1.0] * 2, axis=1)
    sa = jnp.concatenate([-sin, z8, zeros] * 2, axis=1)
    sb = jnp.concatenate([z8, sin, zeros] * 2, axis=1)
    return cosf, sa, sb


def kernel(x, c, ada_w, ada_b, norm1_w, norm2_w, w_in, conv_w, conv_b, ml_i_bias, ml_f_bias, ml_norm_w, lam_q1, lam_k1, lam_q2, lam_k2, subln_w, w_out, peer_wq, peer_keys, peer_u, peer_v, final_norm_w):
    B, S, D = x.shape
    assert D == D_MODEL and S % 512 == 0
    T = B * S
    depth = w_in.shape[0]
    tm = 512
    tps = S // tm
    tables = _rope_tables(S)
    x2 = x.reshape(T, D)
    for l in range(depth):
        mod3 = _mod(c, ada_w[l], ada_b[l])
        h = _norm(x2, norm1_w[l], mod3, B, tps, tm)

        wl = w_in[l]
        o = 0
        cols = {}
        for name, width in (("qa", D), ("ka", D), ("va", D), ("qm", D // 2), ("km", D // 2), ("vm", D),
                            ("om", D), ("ip", HEADS), ("fp", HEADS), ("ga", D), ("gm", D)):
            cols[name] = wl[:, o:o + width]
            o += width
        w_rope = jnp.concatenate([cols["qa"], cols["ka"]], axis=1).astype(BF16)
        w_conv = jnp.concatenate([cols["qm"], cols["km"]], axis=1).astype(BF16)
        w_plain = jnp.concatenate([cols[n] for n in ("va", "vm", "om", "ga", "gm")], axis=1).astype(BF16)

        qk = _rope_proj(h, w_rope, tables, tps, tm)
        plain = _proj(h, w_plain, tm, D)
        qkc, gi, gf, git, gft = _conv_proj(h, w_conv, cols["ip"].astype(BF16), cols["fp"].astype(BF16),
                                           conv_w[l], conv_b[l], ml_i_bias[l], ml_f_bias[l], tps, tm)

        lam_init = 0.8 - 0.6 * math.exp(-0.3 * l)
        ya = _attention(qk, plain, (lam_q1[l], lam_k1[l], lam_q2[l], lam_k2[l]), subln_w[l], lam_init, B, S, ATTN_TQ, ATTN_TK)
        ym = _mlstm(qkc, plain, gi, gf, git, gft, ml_norm_w[l], B, S)
        x1, h2t = _outproj(ya, ym, x2, w_out[l].astype(BF16), norm2_w[l], mod3, B, tps, tm)

        wqt = peer_wq[l].T.astype(BF16)
        keys = peer_keys[l].reshape(2 * HEADS, N_KEYS, LANES).astype(BF16)
        sel = _peer_sel(h2t, wqt, keys, tm)
        x2 = _peer_main(h2t, peer_u[l].astype(BF16), peer_v[l].T.astype(BF16), sel, x1, mod3,
                        final_norm_w, l == depth - 1, B, tps, tm, 8 * N_KEYS)
    if depth == 0:
        raise ValueError("depth must be positive")
    return x2.reshape(B, S, D)
```

```python
import functools
import math

import jax
import jax.numpy as jnp
from jax import lax
from jax.experimental import pallas as pl
from jax.experimental.pallas import tpu as pltpu

F32 = jnp.float32
BF16 = jnp.bfloat16
HIGHEST = lax.Precision.HIGHEST

NORM_EPS = 1e-6
LOG2E = 1.4426950408889634
D_MODEL = 1024
HEADS = 8
HEAD_V = 128
DA_QK = 64
ROPE_DIM = 16
ROPE_HALF = 8
ROPE_THETA = 500000.0
ML_QK = 64
CONV_WIDTH = 4
ML_CHUNK = 128
PROJ_TM = 1024
ATTN_TQ = 512
ATTN_TK = 512
N_KEYS = 128
TOPK = 16
LANES = 128
SUBLANES = 8
VMEM_LIMIT = 48 * 1024 * 1024

_CAND = [(r1, r2) for r1 in range(TOPK) for r2 in range(TOPK) if (r1 + 1) * (r2 + 1) <= TOPK]
_CAND_ROWS = -(-len(_CAND) // SUBLANES) * SUBLANES


def _cparams(sem):
    return pltpu.CompilerParams(dimension_semantics=sem, vmem_limit_bytes=VMEM_LIMIT)


def _rms(x):
    return x * lax.rsqrt(jnp.mean(x * x, axis=-1, keepdims=True) + NORM_EPS)


def _sigmoid(x):
    return 1.0 / (1.0 + jnp.exp(-x))


def _log_sigmoid(x):
    return jnp.minimum(x, 0.0) - jnp.log(1.0 + jnp.exp(-jnp.abs(x)))


def _mod_kernel(c_ref, w_ref, b_ref, o_ref):
    c = c_ref[...]
    cond = c * _sigmoid(c)
    o_ref[0] = jnp.dot(cond, w_ref[...], precision=HIGHEST, preferred_element_type=F32) + b_ref[...]


def _mod(c, ada_w, ada_b):
    B, D = c.shape
    out = pl.pallas_call(
        _mod_kernel,
        grid=(6,),
        in_specs=[pl.BlockSpec((B, D), lambda j: (0, 0)),
                  pl.BlockSpec((D, D), lambda j: (0, j)),
                  pl.BlockSpec((1, D), lambda j: (0, j))],
        out_specs=pl.BlockSpec((1, B, D), lambda j: (j, 0, 0)),
        out_shape=jax.ShapeDtypeStruct((6, B, D), F32),
        compiler_params=_cparams(("parallel",)),
        name="mod",
    )(c, ada_w, ada_b.reshape(1, 6 * D))
    return out.reshape(6 * B, 1, D)


def _norm_kernel(x_ref, w_ref, sc_ref, sh_ref, o_ref):
    y = _rms(x_ref[...]) * w_ref[...]
    o_ref[...] = (y * (1.0 + sc_ref[0]) + sh_ref[0]).astype(o_ref.dtype)


def _norm(x2, w, mod3, B, tps, tm):
    T, D = x2.shape
    return pl.pallas_call(
        _norm_kernel,
        grid=(T // tm,),
        in_specs=[pl.BlockSpec((tm, D), lambda i: (i, 0)),
                  pl.BlockSpec((1, D), lambda i: (0, 0)),
                  pl.BlockSpec((1, 1, D), lambda i: (1 * B + i // tps, 0, 0)),
                  pl.BlockSpec((1, 1, D), lambda i: (0 * B + i // tps, 0, 0))],
        out_specs=pl.BlockSpec((tm, D), lambda i: (i, 0)),
        out_shape=jax.ShapeDtypeStruct((T, D), BF16),
        compiler_params=_cparams(("parallel",)),
        name="norm1",
    )(x2, w.reshape(1, D), mod3, mod3)


def _proj_kernel(h_ref, w_ref, o_ref):
    o_ref[...] = jnp.dot(h_ref[...], w_ref[...], preferred_element_type=F32).astype(o_ref.dtype)


def _proj(h, w, tm, tn):
    T, D = h.shape
    N = w.shape[1]
    return pl.pallas_call(
        _proj_kernel,
        grid=(N // tn, T // tm),
        in_specs=[pl.BlockSpec((tm, D), lambda j, i: (i, 0)),
                  pl.BlockSpec((D, tn), lambda j, i: (0, j))],
        out_specs=pl.BlockSpec((tm, tn), lambda j, i: (i, j)),
        out_shape=jax.ShapeDtypeStruct((T, N), BF16),
        compiler_params=_cparams(("parallel", "parallel")),
        name="proj_plain",
    )(h, w)


def _rope_proj_kernel(h_ref, w_ref, cos_ref, sa_ref, sb_ref, o_ref):
    acc = jnp.dot(h_ref[...], w_ref[...], preferred_element_type=F32)
    scale = jnp.where(pl.program_id(0) == 0, DA_QK ** -0.5 * LOG2E, 1.0).astype(F32)
    cosf, sa, sb = cos_ref[...], sa_ref[...], sb_ref[...]
    for hh in range(HEADS):
        blk = acc[:, hh * LANES:(hh + 1) * LANES]
        rot = (blk * cosf + pltpu.roll(blk, LANES - ROPE_HALF, 1) * sa
               + pltpu.roll(blk, ROPE_HALF, 1) * sb)
        o_ref[:, hh * LANES:(hh + 1) * LANES] = (rot * scale).astype(o_ref.dtype)


def _rope_proj(h, w, tables, tps, tm):
    T, D = h.shape
    cosf, sa, sb = tables
    tab_spec = pl.BlockSpec((tm, LANES), lambda j, i: (i % tps, 0))
    return pl.pallas_call(
        _rope_proj_kernel,
        grid=(2, T // tm),
        in_specs=[pl.BlockSpec((tm, D), lambda j, i: (i, 0)),
                  pl.BlockSpec((D, D), lambda j, i: (0, j)),
                  tab_spec, tab_spec, tab_spec],
        out_specs=pl.BlockSpec((tm, D), lambda j, i: (i, j)),
        out_shape=jax.ShapeDtypeStruct((T, 2 * D), BF16),
        compiler_params=_cparams(("parallel", "parallel")),
        name="proj_rope",
    )(h, w, cosf, sa, sb)


def _conv_proj_kernel(tps, h_ref, w_ref, wgi_ref, wgf_ref, wgit_ref, wgft_ref, cw_ref, cb_ref,
                      bi_ref, bf_ref, bit_ref, bft_ref,
                      o_ref, gi_ref, gf_ref, git_ref, gft_ref, buf):
    i = pl.program_id(0)
    tm = h_ref.shape[0]
    h = h_ref[...]
    acc = jnp.dot(h, w_ref[...], preferred_element_type=F32)

    @pl.when(i % tps == 0)
    def _():
        buf[0:SUBLANES, :] = jnp.zeros((SUBLANES, acc.shape[1]), F32)

    buf[SUBLANES:SUBLANES + tm, :] = acc
    cw = cw_ref[...]
    y = cb_ref[...]
    for j in range(CONV_WIDTH):
        off = SUBLANES - (CONV_WIDTH - 1) + j
        y = y + buf[off:off + tm, :] * cw[j:j + 1, :]
    buf[0:SUBLANES, :] = buf[tm:tm + SUBLANES, :]
    y = y * _sigmoid(y)
    lane = lax.broadcasted_iota(jnp.int32, (1, y.shape[1]), 1)
    kscale = jnp.where(lane >= HEADS * ML_QK, ML_QK ** -0.5, 1.0).astype(F32)
    o_ref[...] = (y * kscale).astype(o_ref.dtype)

    gi_ref[...] = jnp.dot(h, wgi_ref[...], preferred_element_type=F32)[:, :HEADS] + bi_ref[...]
    gf_ref[...] = jnp.dot(h, wgf_ref[...], preferred_element_type=F32)[:, :HEADS] + bf_ref[...]
    nt = (((1,), (1,)), ((), ()))
    git_ref[...] = lax.dot_general(wgit_ref[...], h, nt, preferred_element_type=F32) + bit_ref[...]
    gft_ref[...] = lax.dot_general(wgft_ref[...], h, nt, preferred_element_type=F32) + bft_ref[...]


def _conv_proj(h, w, wgi, wgf, conv_w, conv_b, bi, bf, tps, tm):
    T, D = h.shape
    N = w.shape[1]
    pad = lambda a: jnp.pad(a, ((0, 0), (0, LANES - a.shape[1])))
    full = lambda shape: pl.BlockSpec(shape, lambda i: tuple(0 for _ in shape))
    return pl.pallas_call(
        functools.partial(_conv_proj_kernel, tps),
        grid=(T // tm,),
        in_specs=[pl.BlockSpec((tm, D), lambda i: (i, 0)),
                  full((D, N)), full((D, LANES)), full((D, LANES)),
                  full((HEADS, D)), full((HEADS, D)),
                  full((CONV_WIDTH, N)), full((1, N)),
                  full((1, HEADS)), full((1, HEADS)), full((HEADS, 1)), full((HEADS, 1))],
        out_specs=[pl.BlockSpec((tm, N), lambda i: (i, 0)),
                   pl.BlockSpec((tm, HEADS), lambda i: (i, 0)),
                   pl.BlockSpec((tm, HEADS), lambda i: (i, 0)),
                   pl.BlockSpec((HEADS, tm), lambda i: (0, i)),
                   pl.BlockSpec((HEADS, tm), lambda i: (0, i))],
        out_shape=[jax.ShapeDtypeStruct((T, N), BF16),
                   jax.ShapeDtypeStruct((T, HEADS), F32),
                   jax.ShapeDtypeStruct((T, HEADS), F32),
                   jax.ShapeDtypeStruct((HEADS, T), F32),
                   jax.ShapeDtypeStruct((HEADS, T), F32)],
        scratch_shapes=[pltpu.VMEM((tm + 2 * SUBLANES, N), F32)],
        compiler_params=_cparams(("arbitrary",)),
        name="proj_conv",
    )(h, w, pad(wgi), pad(wgf), wgi.T, wgf.T, conv_w, conv_b.reshape(1, N),
      bi.reshape(1, HEADS), bf.reshape(1, HEADS), bi.reshape(HEADS, 1), bf.reshape(HEADS, 1))


def _attn_kernel(lam_init, q_ref, k_ref, v_ref, ga_ref, lq1_ref, lk1_ref, lq2_ref, lk2_ref, sw_ref,
                 o_ref, qt_s, vt_s, sa_s, sb_s, m_s, l_s, acc_s):
    qi = pl.program_id(2)
    tq = q_ref.shape[0]
    tk = vt_s.shape[2]
    ratio = tq // tk

    @pl.when(qi == 0)
    def _():
        for kk in range(vt_s.shape[0]):
            vt_s[kk] = v_ref[kk * tk:(kk + 1) * tk, :].T

    q = q_ref[...].astype(F32)
    lane = lax.broadcasted_iota(jnp.int32, q.shape, 1)
    qt_s[0] = jnp.where(lane < DA_QK, q, 0.0).T.astype(BF16)
    qt_s[1] = jnp.where(lane >= DA_QK, q, 0.0).T.astype(BF16)
    m_s[...] = jnp.full(m_s.shape, -jnp.inf, F32)
    l_s[...] = jnp.zeros(l_s.shape, F32)
    acc_s[...] = jnp.zeros(acc_s.shape, F32)

    def scores(kk, st_ref):
        k = k_ref[pl.ds(pl.multiple_of(kk * tk, tk), tk), :]
        for c in range(2):
            st_ref[c] = jnp.dot(k, qt_s[c], preferred_element_type=F32)

    def process(kk, st_ref, diag_offset):
        vt = vt_s[kk]
        for c in range(2):
            st = st_ref[c]
            if diag_offset is not None:
                key = lax.broadcasted_iota(jnp.int32, st.shape, 0) + diag_offset
                qry = lax.broadcasted_iota(jnp.int32, st.shape, 1)
                st = jnp.where(key <= qry, st, -jnp.inf)
            m_prev = m_s[c]
            m_new = jnp.maximum(m_prev, jnp.max(st, axis=0, keepdims=True))
            alpha = jnp.exp2(m_prev - m_new)
            p = jnp.exp2(st - m_new)
            l_s[c] = alpha * l_s[c] + jnp.sum(p, axis=0, keepdims=True)
            acc_s[c] = alpha * acc_s[c] + jnp.dot(vt, p.astype(BF16), preferred_element_type=F32)
            m_s[c] = m_new

    def diagonal(first, cur, nxt):
        for d in range(ratio):
            if d + 1 < ratio:
                scores(first + d + 1, nxt)
            process(first + d, cur, d * tk)
            cur, nxt = nxt, cur

    n_below = qi * ratio
    scores(0, sa_s)

    def pair(i, carry):
        kk = 2 * i
        scores(kk + 1, sb_s)
        process(kk, sa_s, None)
        scores(kk + 2, sa_s)
        process(kk + 1, sb_s, None)
        return carry

    lax.fori_loop(0, n_below // 2, pair, 0)
    odd = lax.rem(n_below, 2) == 1

    @pl.when(odd)
    def _():
        scores(n_below, sb_s)
        process(n_below - 1, sa_s, None)
        diagonal(n_below, sb_s, sa_s)

    @pl.when(jnp.logical_not(odd))
    def _():
        diagonal(n_below, sa_s, sb_s)

    lam = (jnp.exp(jnp.sum(lq1_ref[...] * lk1_ref[...], axis=-1, keepdims=True))
           - jnp.exp(jnp.sum(lq2_ref[...] * lk2_ref[...], axis=-1, keepdims=True)) + lam_init)
    o = (acc_s[0] / l_s[0] - lam * (acc_s[1] / l_s[1])).T
    o = _rms(o) * sw_ref[...] * (1.0 - lam_init)
    o_ref[...] = (o * _sigmoid(ga_ref[...].astype(F32))).astype(o_ref.dtype)


def _attention(qk, plain, lam_vecs, subln_w, lam_init, B, S, tq, tk):
    T = qk.shape[0]
    nq = S // tq
    vec = pl.BlockSpec((1, DA_QK), lambda b, h, i: (0, 0))
    return pl.pallas_call(
        functools.partial(_attn_kernel, lam_init),
        grid=(B, HEADS, nq),
        in_specs=[pl.BlockSpec((tq, LANES), lambda b, h, i: (b * nq + i, h)),
                  pl.BlockSpec((S, LANES), lambda b, h, i: (b, HEADS + h)),
                  pl.BlockSpec((S, LANES), lambda b, h, i: (b, h)),
                  pl.BlockSpec((tq, LANES), lambda b, h, i: (b * nq + i, 3 * HEADS + h)),
                  vec, vec, vec, vec,
                  pl.BlockSpec((1, HEAD_V), lambda b, h, i: (0, 0))],
        out_specs=pl.BlockSpec((tq, LANES), lambda b, h, i: (b * nq + i, h)),
        out_shape=jax.ShapeDtypeStruct((T, D_MODEL), BF16),
        scratch_shapes=[pltpu.VMEM((2, LANES, tq), BF16),
                        pltpu.VMEM((S // tk, HEAD_V, tk), BF16),
                        pltpu.VMEM((2, tk, tq), F32),
                        pltpu.VMEM((2, tk, tq), F32),
                        pltpu.VMEM((2, 1, tq), F32),
                        pltpu.VMEM((2, 1, tq), F32),
                        pltpu.VMEM((2, HEAD_V, tq), F32)],
        compiler_params=_cparams(("parallel", "parallel", "arbitrary")),
        name="attn",
    )(qk, qk, plain, plain, *[v.reshape(1, DA_QK) for v in lam_vecs], subln_w.reshape(1, HEAD_V))


def _mlstm_kernel(qk_ref, v_ref, om_ref, gm_ref, gi_ref, gf_ref, git_ref, gft_ref, nw_ref,
                  o_ref, c_s, n_s, m_s):
    L = qk_ref.shape[0]

    @pl.when(pl.program_id(1) == 0)
    def _():
        c_s[...] = jnp.zeros(c_s.shape, F32)
        n_s[...] = jnp.zeros(n_s.shape, F32)
        m_s[...] = jnp.zeros(m_s.shape, F32)

    row = lax.broadcasted_iota(jnp.int32, (L, L), 0)
    col = lax.broadcasted_iota(jnp.int32, (L, L), 1)
    tri = (col <= row).astype(F32)
    causal_t = row <= col
    tri_t = causal_t.astype(F32)
    bcols = jnp.dot(tri, _log_sigmoid(gf_ref[...]), precision=HIGHEST, preferred_element_type=F32)
    brows = jnp.dot(_log_sigmoid(gft_ref[...]), tri_t, precision=HIGHEST, preferred_element_type=F32)
    ucols = gi_ref[...] - bcols
    git = git_ref[...]
    lane = lax.broadcasted_iota(jnp.int32, (1, LANES), 1)
    first_row = lax.broadcasted_iota(jnp.int32, (SUBLANES, 1), 0) == 0
    nt = (((1,), (1,)), ((), ()))

    for h in range(HEADS):
        p = h // 2
        hmask = ((lane >= (h % 2) * ML_QK) & (lane < (h % 2 + 1) * ML_QK)).astype(F32)
        qh = (qk_ref[:, p * LANES:(p + 1) * LANES].astype(F32) * hmask).astype(BF16)
        kp = qk_ref[:, (HEADS // 2 + p) * LANES:(HEADS // 2 + p + 1) * LANES]
        vt = v_ref[:, h * HEAD_V:(h + 1) * HEAD_V].T
        br = brows[h:h + 1, :]
        igr = git[h:h + 1, :]
        g_tot = br[:, L - 1:L]
        m_prev = m_s[h]
        ct_prev = c_s[h]
        n_prev = n_s[h]

        dm = jnp.where(causal_t, br + ucols[:, h:h + 1], -jnp.inf)
        m_inter = br + m_prev
        m_j = jnp.maximum(jnp.max(dm, axis=0, keepdims=True), m_inter)
        st = lax.dot_general(kp, qh, nt, preferred_element_type=F32)
        qkw = st * jnp.exp(dm - m_j)
        inter_w = jnp.exp(m_inter - m_j)
        num = (jnp.dot(vt, qkw.astype(BF16), preferred_element_type=F32)
               + inter_w * lax.dot_general(ct_prev.astype(BF16), qh, nt, preferred_element_type=F32))
        qn = lax.dot_general(n_prev.astype(BF16), qh, nt, preferred_element_type=F32)[0:1, :]
        den = jnp.sum(qkw, axis=0, keepdims=True) + inter_w * qn
        ht = num / jnp.maximum(jnp.abs(den), jnp.exp(-m_j))

        a = g_tot - br + igr
        m_loc = jnp.max(a, axis=-1, keepdims=True)
        w_loc = jnp.exp(a - m_loc)
        m_new = jnp.maximum(g_tot + m_prev, m_loc)
        dec = jnp.exp(g_tot + m_prev - m_new)
        inc = jnp.exp(m_loc - m_new)
        c_loc = jnp.dot((vt.astype(F32) * w_loc).astype(BF16), kp, preferred_element_type=F32)
        w8 = jnp.where(first_row, w_loc, 0.0)
        n_loc = jnp.dot(w8, kp.astype(F32), precision=HIGHEST, preferred_element_type=F32)
        c_s[h] = dec * ct_prev + inc * c_loc
        n_s[h] = dec * n_prev + inc * n_loc
        m_s[h] = m_new

        sl = slice(h * HEAD_V, (h + 1) * HEAD_V)
        yt = ht * lax.rsqrt(jnp.mean(ht * ht, axis=0, keepdims=True) + NORM_EPS)
        y = yt.T * nw_ref[:, sl]
        y = y * _sigmoid(om_ref[:, sl].astype(F32)) * _sigmoid(gm_ref[:, sl].astype(F32))
        o_ref[:, sl] = y.astype(o_ref.dtype)


def _mlstm(qkc, plain, gi, gf, git, gft, ml_norm_w, B, S):
    T, D = qkc.shape
    L = ML_CHUNK
    nc = S // L
    wide = lambda cb: pl.BlockSpec((L, D), lambda b, c: (b * nc + c, cb))
    return pl.pallas_call(
        _mlstm_kernel,
        grid=(B, nc),
        in_specs=[wide(0), wide(1), wide(2), wide(4),
                  pl.BlockSpec((L, HEADS), lambda b, c: (b * nc + c, 0)),
                  pl.BlockSpec((L, HEADS), lambda b, c: (b * nc + c, 0)),
                  pl.BlockSpec((HEADS, L), lambda b, c: (0, b * nc + c)),
                  pl.BlockSpec((HEADS, L), lambda b, c: (0, b * nc + c)),
                  pl.BlockSpec((1, D), lambda b, c: (0, 0))],
        out_specs=pl.BlockSpec((L, D), lambda b, c: (b * nc + c, 0)),
        out_shape=jax.ShapeDtypeStruct((T, D), BF16),
        scratch_shapes=[pltpu.VMEM((HEADS, LANES, HEAD_V), F32),
                        pltpu.VMEM((HEADS, SUBLANES, LANES), F32),
                        pltpu.VMEM((HEADS, 1, 1), F32)],
        compiler_params=_cparams(("parallel", "arbitrary")),
        name="mlstm",
    )(qkc, plain, plain, plain, gi, gf, git, gft, ml_norm_w.reshape(1, D))


def _outproj_kernel(ya_ref, ym_ref, x_ref, w_ref, gt_ref, nw_ref, sc_ref, sh_ref, x1_ref, h2t_ref):
    merged = (ya_ref[...].astype(F32) + ym_ref[...].astype(F32)).astype(BF16)
    x1 = x_ref[...] + gt_ref[0] * jnp.dot(merged, w_ref[...], preferred_element_type=F32)
    x1_ref[...] = x1
    h2 = _rms(x1) * nw_ref[...] * (1.0 + sc_ref[0]) + sh_ref[0]
    h2t_ref[...] = h2.T.astype(h2t_ref.dtype)


def _outproj(ya, ym, x2, w_out, norm2_w, mod3, B, tps, tm):
    T, D = x2.shape
    row = lambda k: pl.BlockSpec((1, 1, D), lambda i: (k * B + i // tps, 0, 0))
    tile = pl.BlockSpec((tm, D), lambda i: (i, 0))
    return pl.pallas_call(
        _outproj_kernel,
        grid=(T // tm,),
        in_specs=[tile, tile, tile, pl.BlockSpec((D, D), lambda i: (0, 0)), row(2),
                  pl.BlockSpec((1, D), lambda i: (0, 0)), row(4), row(3)],
        out_specs=[tile, pl.BlockSpec((D, tm), lambda i: (0, i))],
        out_shape=[jax.ShapeDtypeStruct((T, D), F32), jax.ShapeDtypeStruct((D, T), BF16)],
        compiler_params=_cparams(("parallel",)),
        name="outproj",
    )(ya, ym, x2, w_out, mod3, norm2_w.reshape(1, D), mod3, mod3)


def _topk_ranks(s):
    n = s.shape[0]
    it = lax.broadcasted_iota(jnp.int32, s.shape, 0)
    rank = jnp.full(s.shape, float(TOPK), F32)
    x = s
    vals = []
    for r in range(TOPK):
        m = jnp.max(x, axis=0, keepdims=True)
        first = jnp.min(jnp.where(x == m, it, n), axis=0, keepdims=True)
        hit = it == first
        x = jnp.where(hit, -jnp.inf, x)
        rank = jnp.where(hit, float(r), rank)
        vals.append(m)
    return rank, vals


MARK_BASE = 3.0e38
MARK_STEP = 1.0e36


def _topk_ranks_distinct(s):
    x = s
    vals = []
    for r in range(TOPK):
        m = jnp.max(x, axis=0, keepdims=True)
        x = jnp.where(x == m, -(MARK_BASE + r * MARK_STEP), x)
        vals.append(m)
    taken = x <= -MARK_BASE
    rank = jnp.where(taken, jnp.round((-x - MARK_BASE) * (1.0 / MARK_STEP)), float(TOPK))
    removed = jnp.sum(jnp.where(taken, 1.0, 0.0), axis=0, keepdims=True)
    return rank, vals, removed == float(TOPK)


def _cand_select(cand, top):
    it = lax.broadcasted_iota(jnp.int32, cand.shape, 0)
    sel = jnp.zeros(cand.shape, F32)
    z = jnp.zeros(top.shape, F32)
    x = cand
    for _ in range(TOPK):
        m = jnp.max(x, axis=0, keepdims=True)
        first = jnp.min(jnp.where(x == m, it, cand.shape[0]), axis=0, keepdims=True)
        hit = it == first
        x = jnp.where(hit, -jnp.inf, x)
        sel = jnp.where(hit, 1.0, sel)
        z = z + jnp.exp(m - top)
    return sel, z


def _cand_select_distinct(cand, top):
    z = jnp.zeros(top.shape, F32)
    x = cand
    for _ in range(TOPK):
        m = jnp.max(x, axis=0, keepdims=True)
        x = jnp.where(x == m, -MARK_BASE, x)
        z = z + jnp.exp(m - top)
    sel = jnp.where(x == -MARK_BASE, 1.0, 0.0)
    return sel, z, jnp.sum(sel, axis=0, keepdims=True) == float(TOPK)


def _peer_sel_kernel(h2t_ref, wqt_ref, keys_ref, r2_ref, e2_ref, a_ref, c_ref, rank_s, vals_s, sel_s, z_s):
    ht = h2t_ref[...]
    scores = []
    suspect = []
    for p in range(2):
        qt = jnp.dot(wqt_ref[p * LANES:(p + 1) * LANES, :], ht, preferred_element_type=F32)
        s = jnp.dot(keys_ref[p], qt.astype(BF16), preferred_element_type=F32)
        scores.append(s)
        rank, vals, distinct = _topk_ranks_distinct(s)
        rank_s[p] = rank
        vals_s[p] = jnp.concatenate(vals, axis=0)
        suspect.append(jnp.sum(jnp.where(distinct, 0.0, 1.0)) > 0.0)

    for p in range(2):
        @pl.when(suspect[p])
        def _():
            rank, vals = _topk_ranks(scores[p])
            rank_s[p] = rank
            vals_s[p] = jnp.concatenate(vals, axis=0)

    s1, s2 = scores
    rank1, rank2 = rank_s[0], rank_s[1]
    v1 = [vals_s[0, r:r + 1, :] for r in range(TOPK)]
    v2 = [vals_s[1, r:r + 1, :] for r in range(TOPK)]

    tt = s1.shape[1]
    rows = [v1[r1] + v2[r2] for (r1, r2) in _CAND]
    rows += [jnp.full((1, tt), -jnp.inf, F32)] * (_CAND_ROWS - len(_CAND))
    cand = jnp.concatenate(rows, axis=0)
    top = v1[0] + v2[0]
    sel, z, distinct = _cand_select_distinct(cand, top)
    sel_s[...] = sel
    z_s[...] = z

    @pl.when(jnp.sum(jnp.where(distinct, 0.0, 1.0)) > 0.0)
    def _():
        sel, z = _cand_select(cand, top)
        sel_s[...] = sel
        z_s[...] = z

    sel = sel_s[...]
    z = z_s[...]
    a = jnp.zeros(s1.shape, F32)
    start = 0
    for r1 in range(TOPK):
        width = sum(1 for c in _CAND if c[0] == r1)
        cnt = jnp.sum(sel[start:start + width, :], axis=0, keepdims=True)
        a = jnp.where(rank1 == float(r1), cnt, a)
        start += width

    for ref, val in ((r2_ref, rank2), (e2_ref, jnp.exp(s2 - v2[0])), (a_ref, a), (c_ref, jnp.exp(s1 - v1[0]) / z)):
        for lt in range(tt // LANES):
            ref[lt] = val[:, lt * LANES:(lt + 1) * LANES].astype(ref.dtype)


def _peer_sel(h2t, wqt, keys, tt):
    D, T = h2t.shape
    out = pl.BlockSpec((tt // LANES, N_KEYS, LANES), lambda i, h: (i, h, 0))
    shp = lambda dt: jax.ShapeDtypeStruct((T // LANES, HEADS * N_KEYS, LANES), dt)
    return pl.pallas_call(
        _peer_sel_kernel,
        grid=(T // tt, HEADS),
        in_specs=[pl.BlockSpec((D, tt), lambda i, h: (0, i)),
                  pl.BlockSpec((2 * LANES, D), lambda i, h: (h, 0)),
                  pl.BlockSpec((2, N_KEYS, LANES), lambda i, h: (h, 0, 0))],
        out_specs=[out, out, out, out],
        out_shape=[shp(BF16), shp(BF16), shp(F32), shp(F32)],
        scratch_shapes=[pltpu.VMEM((2, N_KEYS, tt), F32), pltpu.VMEM((2, TOPK, tt), F32),
                        pltpu.VMEM((_CAND_ROWS, tt), F32), pltpu.VMEM((1, tt), F32)],
        compiler_params=_cparams(("parallel", "parallel")),
        name="peer_sel",
    )(h2t, wqt, keys)


PEER_ROWS = 64
MXU_COLS = 256
ACT_ROWS = 256
OUT_ROWS = 512


def _gate_rows(row, rows):
    packed = jnp.broadcast_to(row, (2 * SUBLANES, LANES)).astype(BF16)
    return jnp.tile(packed, (rows // (2 * SUBLANES), 1))


def _peer_main_kernel(final, h2t_ref, u_ref, vt_ref, r2_ref, e2_ref, a_ref, c_ref, x1_ref, gt_ref, fw_ref,
                      o_ref, acc_ref, act_ref, p_ref):
    e = pl.program_id(1)
    eb, tt = act_ref.shape
    nj = eb // N_KEYS

    @pl.when(e == 0)
    def _():
        acc_ref[...] = jnp.zeros(acc_ref.shape, F32)

    n_half = tt // MXU_COLS
    tcols = lambda th: slice(th * MXU_COLS, (th + 1) * MXU_COLS)

    def act_piece(th, m):
        ms = slice(m * ACT_ROWS, (m + 1) * ACT_ROWS)
        act_ref[ms, tcols(th)] = jnp.dot(u_ref[ms, :], h2t_ref[:, tcols(th)], preferred_element_type=F32)

    def out_piece(th, r):
        rs = slice(r * OUT_ROWS, (r + 1) * OUT_ROWS)
        acc_ref[rs, tcols(th)] += jnp.dot(vt_ref[rs, :], p_ref[:, tcols(th)], preferred_element_type=F32)

    def gate_chunk(tl, bs):
        g = [None] * nj
        for h in range(HEADS):
            rs = slice(h * N_KEYS + bs * PEER_ROWS, h * N_KEYS + (bs + 1) * PEER_ROWS)
            r2c = r2_ref[tl, rs, :]
            e2c = e2_ref[tl, rs, :]
            for j in range(nj):
                idx = h * N_KEYS + e * nj + j
                arow = _gate_rows(a_ref[tl, pl.ds(idx, 1), :], PEER_ROWS)
                crow = _gate_rows(c_ref[tl, pl.ds(idx, 1), :], PEER_ROWS)
                w = jnp.where(r2c < arow, e2c, jnp.zeros_like(e2c)) * crow
                g[j] = w if g[j] is None else g[j] + w
        ls = slice(tl * LANES, (tl + 1) * LANES)
        for j in range(nj):
            es = slice(j * N_KEYS + bs * PEER_ROWS, j * N_KEYS + (bs + 1) * PEER_ROWS)
            aj = act_ref[es, ls]
            gelu = 0.5 * aj * (1.0 + lax.erf(aj * (2.0 ** -0.5)))
            p_ref[es, ls] = g[j] * gelu.astype(BF16)

    n_act = eb // ACT_ROWS
    n_out = acc_ref.shape[0] // OUT_ROWS
    for m in range(n_act):
        act_piece(0, m)
    for th in range(n_half):
        mxu_work = [functools.partial(act_piece, th + 1, m) for m in range(n_act)] if th + 1 < n_half else []
        if th > 0:
            mxu_work += [functools.partial(out_piece, th - 1, r) for r in range(n_out)]
        chunks = [(th * (MXU_COLS // LANES) + lt, bs)
                  for lt in range(MXU_COLS // LANES) for bs in range(N_KEYS // PEER_ROWS)]
        per_chunk = -(-len(mxu_work) // len(chunks))
        for tl, bs in chunks:
            gate_chunk(tl, bs)
            for piece in mxu_work[:per_chunk]:
                piece()
            mxu_work = mxu_work[per_chunk:]
    for r in range(n_out):
        out_piece(n_half - 1, r)

    @pl.when(e == pl.num_programs(1) - 1)
    def _():
        x2 = x1_ref[...] + gt_ref[0] * acc_ref[...].T
        o_ref[...] = _rms(x2) * fw_ref[...] if final else x2


def _peer_main(h2t, u, vt, sel, x1, mod3, fw, final, B, tps, tt, eb):
    D, T = h2t.shape
    ne = u.shape[0]
    wide = pl.BlockSpec((tt // LANES, HEADS * N_KEYS, LANES), lambda i, e: (i, 0, 0))
    return pl.pallas_call(
        functools.partial(_peer_main_kernel, final),
        grid=(T // tt, ne // eb),
        in_specs=[pl.BlockSpec((D, tt), lambda i, e: (0, i)),
                  pl.BlockSpec((eb, D), lambda i, e: (e, 0)),
                  pl.BlockSpec((D, eb), lambda i, e: (0, e)),
                  wide, wide, wide, wide,
                  pl.BlockSpec((tt, D), lambda i, e: (i, 0)),
                  pl.BlockSpec((1, 1, D), lambda i, e: (5 * B + i // tps, 0, 0)),
                  pl.BlockSpec((1, D), lambda i, e: (0, 0))],
        out_specs=pl.BlockSpec((tt, D), lambda i, e: (i, 0)),
        out_shape=jax.ShapeDtypeStruct((T, D), F32),
        scratch_shapes=[pltpu.VMEM((D, tt), F32),
                        pltpu.VMEM((eb, tt), F32),
                        pltpu.VMEM((eb, tt), BF16)],
        compiler_params=_cparams(("parallel", "arbitrary")),
        name="peer_main",
    )(h2t, u, vt, *sel, x1, mod3, fw.reshape(1, D))


def _rope_tables(S):
    inv = ROPE_THETA ** (-jnp.arange(ROPE_HALF, dtype=F32) * 2.0 / ROPE_DIM)
    ang = jnp.arange(S, dtype=jnp.int32).astype(F32)[:, None] * inv[None, :]
    cos, sin = jnp.cos(ang), jnp.sin(ang)
    zeros = jnp.zeros((S, DA_QK - ROPE_DIM), F32)
    z8 = jnp.zeros((S, ROPE_HALF), F32)
    cosf = jnp.concatenate([cos, cos, zeros + 1.0] * 2, axis=1)
    sa = jnp.concatenate([-sin, z8, zeros] * 2, axis=1)
    sb = jnp.concatenate([z8, sin, zeros] * 2, axis=1)
    return cosf, sa, sb


def kernel(x, c, ada_w, ada_b, norm1_w, norm2_w, w_in, conv_w, conv_b, ml_i_bias, ml_f_bias, ml_norm_w, lam_q1, lam_k1, lam_q2, lam_k2, subln_w, w_out, peer_wq, peer_keys, peer_u, peer_v, final_norm_w):
    B, S, D = x.shape
    assert D == D_MODEL and S % PROJ_TM == 0 and S % ATTN_TQ == 0
    T = B * S
    depth = w_in.shape[0]
    tm = 512
    tps = S // tm
    tables = _rope_tables(S)
    x2 = x.reshape(T, D)
    for l in range(depth):
        mod3 = _mod(c, ada_w[l], ada_b[l])
        h = _norm(x2, norm1_w[l], mod3, B, tps, tm)

        wl = w_in[l]
        o = 0
        cols = {}
        for name, width in (("qa", D), ("ka", D), ("va", D), ("qm", D // 2), ("km", D // 2), ("vm", D),
                            ("om", D), ("ip", HEADS), ("fp", HEADS), ("ga", D), ("gm", D)):
            cols[name] = wl[:, o:o + width]
            o += width
        w_rope = jnp.concatenate([cols["qa"], cols["ka"]], axis=1).astype(BF16)
        w_conv = jnp.concatenate([cols["qm"], cols["km"]], axis=1).astype(BF16)
        w_plain = jnp.concatenate([cols[n] for n in ("va", "vm", "om", "ga", "gm")], axis=1).astype(BF16)

        qk = _rope_proj(h, w_rope, tables, S // PROJ_TM, PROJ_TM)
        plain = _proj(h, w_plain, PROJ_TM, D)
        qkc, gi, gf, git, gft = _conv_proj(h, w_conv, cols["ip"].astype(BF16), cols["fp"].astype(BF16),
                                           conv_w[l], conv_b[l], ml_i_bias[l], ml_f_bias[l], tps, tm)

        lam_init = 0.8 - 0.6 * math.exp(-0.3 * l)
        ya = _attention(qk, plain, (lam_q1[l], lam_k1[l], lam_q2[l], lam_k2[l]), subln_w[l], lam_init, B, S, ATTN_TQ, ATTN_TK)
        ym = _mlstm(qkc, plain, gi, gf, git, gft, ml_norm_w[l], B, S)
        x1, h2t = _outproj(ya, ym, x2, w_out[l].astype(BF16), norm2_w[l], mod3, B, tps, tm)

        wqt = peer_wq[l].T.astype(BF16)
        keys = peer_keys[l].reshape(2 * HEADS, N_KEYS, LANES).astype(BF16)
        sel = _peer_sel(h2t, wqt, keys, tm)
        x2 = _peer_main(h2t, peer_u[l].astype(BF16), peer_v[l].T.astype(BF16), sel, x1, mod3,
                        final_norm_w, l == depth - 1, B, tps, tm, 8 * N_KEYS)
    if depth == 0:
        raise ValueError("depth must be positive")
    return x2.reshape(B, S, D)
```

```python
import functools
import math

import jax
import jax.numpy as jnp
from jax import lax
from jax.experimental import pallas as pl
from jax.experimental.pallas import tpu as pltpu

F32 = jnp.float32
BF16 = jnp.bfloat16
HIGHEST = lax.Precision.HIGHEST

NORM_EPS = 1e-6
LOG2E = 1.4426950408889634
D_MODEL = 1024
HEADS = 8
HEAD_V = 128
DA_QK = 64
ROPE_DIM = 16
ROPE_HALF = 8
ROPE_THETA = 500000.0
ML_QK = 64
CONV_WIDTH = 4
ML_CHUNK = 128
PROJ_TM = 1024
ATTN_TQ = 512
ATTN_TK = 512
N_KEYS = 128
TOPK = 16
LANES = 128
SUBLANES = 8
VMEM_LIMIT = 56 * 1024 * 1024

_CAND = [(r1, r2) for r1 in range(TOPK) for r2 in range(TOPK) if (r1 + 1) * (r2 + 1) <= TOPK]
_CAND_ROWS = -(-len(_CAND) // SUBLANES) * SUBLANES


def _cparams(sem):
    return pltpu.CompilerParams(dimension_semantics=sem, vmem_limit_bytes=VMEM_LIMIT)


def _rms(x):
    return x * lax.rsqrt(jnp.mean(x * x, axis=-1, keepdims=True) + NORM_EPS)


def _sigmoid(x):
    return 1.0 / (1.0 + jnp.exp(-x))


def _log_sigmoid(x):
    return jnp.minimum(x, 0.0) - jnp.log(1.0 + jnp.exp(-jnp.abs(x)))


def _mod_kernel(c_ref, w_ref, b_ref, o_ref):
    c = c_ref[...]
    cond = c * _sigmoid(c)
    o_ref[0] = jnp.dot(cond, w_ref[...], precision=HIGHEST, preferred_element_type=F32) + b_ref[...]


def _mod(c, ada_w, ada_b):
    B, D = c.shape
    out = pl.pallas_call(
        _mod_kernel,
        grid=(6,),
        in_specs=[pl.BlockSpec((B, D), lambda j: (0, 0)),
                  pl.BlockSpec((D, D), lambda j: (0, j)),
                  pl.BlockSpec((1, D), lambda j: (0, j))],
        out_specs=pl.BlockSpec((1, B, D), lambda j: (j, 0, 0)),
        out_shape=jax.ShapeDtypeStruct((6, B, D), F32),
        compiler_params=_cparams(("parallel",)),
        name="mod",
    )(c, ada_w, ada_b.reshape(1, 6 * D))
    return out.reshape(6 * B, 1, D)


def _norm_kernel(x_ref, w_ref, sc_ref, sh_ref, o_ref):
    y = _rms(x_ref[...]) * w_ref[...]
    o_ref[...] = (y * (1.0 + sc_ref[0]) + sh_ref[0]).astype(o_ref.dtype)


def _norm(x2, w, mod3, B, tps, tm):
    T, D = x2.shape
    return pl.pallas_call(
        _norm_kernel,
        grid=(T // tm,),
        in_specs=[pl.BlockSpec((tm, D), lambda i: (i, 0)),
                  pl.BlockSpec((1, D), lambda i: (0, 0)),
                  pl.BlockSpec((1, 1, D), lambda i: (1 * B + i // tps, 0, 0)),
                  pl.BlockSpec((1, 1, D), lambda i: (0 * B + i // tps, 0, 0))],
        out_specs=pl.BlockSpec((tm, D), lambda i: (i, 0)),
        out_shape=jax.ShapeDtypeStruct((T, D), BF16),
        compiler_params=_cparams(("parallel",)),
        name="norm1",
    )(x2, w.reshape(1, D), mod3, mod3)


def _proj_kernel(h_ref, w_ref, o_ref):
    o_ref[...] = jnp.dot(h_ref[...], w_ref[...], preferred_element_type=F32).astype(o_ref.dtype)


def _proj(h, w, tm, tn):
    T, D = h.shape
    N = w.shape[1]
    return pl.pallas_call(
        _proj_kernel,
        grid=(N // tn, T // tm),
        in_specs=[pl.BlockSpec((tm, D), lambda j, i: (i, 0)),
                  pl.BlockSpec((D, tn), lambda j, i: (0, j))],
        out_specs=pl.BlockSpec((tm, tn), lambda j, i: (i, j)),
        out_shape=jax.ShapeDtypeStruct((T, N), BF16),
        compiler_params=_cparams(("parallel", "parallel")),
        name="proj_plain",
    )(h, w)


def _rope_proj_kernel(h_ref, w_ref, cos_ref, sa_ref, sb_ref, o_ref):
    acc = jnp.dot(h_ref[...], w_ref[...], preferred_element_type=F32)
    scale = jnp.where(pl.program_id(0) == 0, DA_QK ** -0.5 * LOG2E, 1.0).astype(F32)
    cosf, sa, sb = cos_ref[...], sa_ref[...], sb_ref[...]
    for hh in range(HEADS):
        blk = acc[:, hh * LANES:(hh + 1) * LANES]
        rot = (blk * cosf + pltpu.roll(blk, LANES - ROPE_HALF, 1) * sa
               + pltpu.roll(blk, ROPE_HALF, 1) * sb)
        o_ref[:, hh * LANES:(hh + 1) * LANES] = (rot * scale).astype(o_ref.dtype)


def _rope_proj(h, w, tables, tps, tm):
    T, D = h.shape
    cosf, sa, sb = tables
    tab_spec = pl.BlockSpec((tm, LANES), lambda j, i: (i % tps, 0))
    return pl.pallas_call(
        _rope_proj_kernel,
        grid=(2, T // tm),
        in_specs=[pl.BlockSpec((tm, D), lambda j, i: (i, 0)),
                  pl.BlockSpec((D, D), lambda j, i: (0, j)),
                  tab_spec, tab_spec, tab_spec],
        out_specs=pl.BlockSpec((tm, D), lambda j, i: (i, j)),
        out_shape=jax.ShapeDtypeStruct((T, 2 * D), BF16),
        compiler_params=_cparams(("parallel", "parallel")),
        name="proj_rope",
    )(h, w, cosf, sa, sb)


def _conv_proj_kernel(tps, h_ref, w_ref, wgi_ref, wgf_ref, wgit_ref, wgft_ref, cw_ref, cb_ref,
                      bi_ref, bf_ref, bit_ref, bft_ref,
                      o_ref, gi_ref, gf_ref, git_ref, gft_ref, buf):
    i = pl.program_id(0)
    tm = h_ref.shape[0]
    h = h_ref[...]
    acc = jnp.dot(h, w_ref[...], preferred_element_type=F32)

    @pl.when(i % tps == 0)
    def _():
        buf[0:SUBLANES, :] = jnp.zeros((SUBLANES, acc.shape[1]), F32)

    buf[SUBLANES:SUBLANES + tm, :] = acc
    cw = cw_ref[...]
    y = cb_ref[...]
    for j in range(CONV_WIDTH):
        off = SUBLANES - (CONV_WIDTH - 1) + j
        y = y + buf[off:off + tm, :] * cw[j:j + 1, :]
    buf[0:SUBLANES, :] = buf[tm:tm + SUBLANES, :]
    y = y * _sigmoid(y)
    lane = lax.broadcasted_iota(jnp.int32, (1, y.shape[1]), 1)
    kscale = jnp.where(lane >= HEADS * ML_QK, ML_QK ** -0.5, 1.0).astype(F32)
    o_ref[...] = (y * kscale).astype(o_ref.dtype)

    gi_ref[...] = jnp.dot(h, wgi_ref[...], preferred_element_type=F32)[:, :HEADS] + bi_ref[...]
    gf_ref[...] = jnp.dot(h, wgf_ref[...], preferred_element_type=F32)[:, :HEADS] + bf_ref[...]
    nt = (((1,), (1,)), ((), ()))
    git_ref[...] = lax.dot_general(wgit_ref[...], h, nt, preferred_element_type=F32) + bit_ref[...]
    gft_ref[...] = lax.dot_general(wgft_ref[...], h, nt, preferred_element_type=F32) + bft_ref[...]


def _conv_proj(h, w, wgi, wgf, conv_w, conv_b, bi, bf, tps, tm):
    T, D = h.shape
    N = w.shape[1]
    pad = lambda a: jnp.pad(a, ((0, 0), (0, LANES - a.shape[1])))
    full = lambda shape: pl.BlockSpec(shape, lambda i: tuple(0 for _ in shape))
    return pl.pallas_call(
        functools.partial(_conv_proj_kernel, tps),
        grid=(T // tm,),
        in_specs=[pl.BlockSpec((tm, D), lambda i: (i, 0)),
                  full((D, N)), full((D, LANES)), full((D, LANES)),
                  full((HEADS, D)), full((HEADS, D)),
                  full((CONV_WIDTH, N)), full((1, N)),
                  full((1, HEADS)), full((1, HEADS)), full((HEADS, 1)), full((HEADS, 1))],
        out_specs=[pl.BlockSpec((tm, N), lambda i: (i, 0)),
                   pl.BlockSpec((tm, HEADS), lambda i: (i, 0)),
                   pl.BlockSpec((tm, HEADS), lambda i: (i, 0)),
                   pl.BlockSpec((HEADS, tm), lambda i: (0, i)),
                   pl.BlockSpec((HEADS, tm), lambda i: (0, i))],
        out_shape=[jax.ShapeDtypeStruct((T, N), BF16),
                   jax.ShapeDtypeStruct((T, HEADS), F32),
                   jax.ShapeDtypeStruct((T, HEADS), F32),
                   jax.ShapeDtypeStruct((HEADS, T), F32),
                   jax.ShapeDtypeStruct((HEADS, T), F32)],
        scratch_shapes=[pltpu.VMEM((tm + 2 * SUBLANES, N), F32)],
        compiler_params=_cparams(("arbitrary",)),
        name="proj_conv",
    )(h, w, pad(wgi), pad(wgf), wgi.T, wgf.T, conv_w, conv_b.reshape(1, N),
      bi.reshape(1, HEADS), bf.reshape(1, HEADS), bi.reshape(HEADS, 1), bf.reshape(HEADS, 1))


def _attn_kernel(lam_init, q_ref, k_ref, v_ref, ga_ref, lq1_ref, lk1_ref, lq2_ref, lk2_ref, sw_ref,
                 o_ref, qt_s, vt_s, sa_s, sb_s, m_s, l_s, acc_s):
    qi = pl.program_id(2)
    tq = q_ref.shape[0]
    tk = vt_s.shape[2]
    ratio = tq // tk

    @pl.when(qi == 0)
    def _():
        for kk in range(vt_s.shape[0]):
            vt_s[kk] = v_ref[kk * tk:(kk + 1) * tk, :].T

    q = q_ref[...].astype(F32)
    lane = lax.broadcasted_iota(jnp.int32, q.shape, 1)
    qt_s[0] = jnp.where(lane < DA_QK, q, 0.0).T.astype(BF16)
    qt_s[1] = jnp.where(lane >= DA_QK, q, 0.0).T.astype(BF16)
    m_s[...] = jnp.full(m_s.shape, -jnp.inf, F32)
    l_s[...] = jnp.zeros(l_s.shape, F32)
    acc_s[...] = jnp.zeros(acc_s.shape, F32)

    def scores(kk, st_ref):
        k = k_ref[pl.ds(pl.multiple_of(kk * tk, tk), tk), :]
        for c in range(2):
            st_ref[c] = jnp.dot(k, qt_s[c], preferred_element_type=F32)

    def process(kk, st_ref, diag_offset):
        vt = vt_s[kk]
        for c in range(2):
            st = st_ref[c]
            if diag_offset is not None:
                key = lax.broadcasted_iota(jnp.int32, st.shape, 0) + diag_offset
                qry = lax.broadcasted_iota(jnp.int32, st.shape, 1)
                st = jnp.where(key <= qry, st, -jnp.inf)
            m_prev = m_s[c]
            m_new = jnp.maximum(m_prev, jnp.max(st, axis=0, keepdims=True))
            alpha = jnp.exp2(m_prev - m_new)
            p = jnp.exp2(st - m_new)
            l_s[c] = alpha * l_s[c] + jnp.sum(p, axis=0, keepdims=True)
            acc_s[c] = alpha * acc_s[c] + jnp.dot(vt, p.astype(BF16), preferred_element_type=F32)
            m_s[c] = m_new

    def diagonal(first, cur, nxt):
        for d in range(ratio):
            if d + 1 < ratio:
                scores(first + d + 1, nxt)
            process(first + d, cur, d * tk)
            cur, nxt = nxt, cur

    n_below = qi * ratio
    scores(0, sa_s)

    def pair(i, carry):
        kk = 2 * i
        scores(kk + 1, sb_s)
        process(kk, sa_s, None)
        scores(kk + 2, sa_s)
        process(kk + 1, sb_s, None)
        return carry

    lax.fori_loop(0, n_below // 2, pair, 0)
    odd = lax.rem(n_below, 2) == 1

    @pl.when(odd)
    def _():
        scores(n_below, sb_s)
        process(n_below - 1, sa_s, None)
        diagonal(n_below, sb_s, sa_s)

    @pl.when(jnp.logical_not(odd))
    def _():
        diagonal(n_below, sa_s, sb_s)

    lam = (jnp.exp(jnp.sum(lq1_ref[...] * lk1_ref[...], axis=-1, keepdims=True))
           - jnp.exp(jnp.sum(lq2_ref[...] * lk2_ref[...], axis=-1, keepdims=True)) + lam_init)
    o = (acc_s[0] / l_s[0] - lam * (acc_s[1] / l_s[1])).T
    o = _rms(o) * sw_ref[...] * (1.0 - lam_init)
    o_ref[...] = (o * _sigmoid(ga_ref[...].astype(F32))).astype(o_ref.dtype)


def _attention(qk, plain, lam_vecs, subln_w, lam_init, B, S, tq, tk):
    T = qk.shape[0]
    nq = S // tq
    vec = pl.BlockSpec((1, DA_QK), lambda b, h, i: (0, 0))
    return pl.pallas_call(
        functools.partial(_attn_kernel, lam_init),
        grid=(B, HEADS, nq),
        in_specs=[pl.BlockSpec((tq, LANES), lambda b, h, i: (b * nq + i, h)),
                  pl.BlockSpec((S, LANES), lambda b, h, i: (b, HEADS + h)),
                  pl.BlockSpec((S, LANES), lambda b, h, i: (b, h)),
                  pl.BlockSpec((tq, LANES), lambda b, h, i: (b * nq + i, 3 * HEADS + h)),
                  vec, vec, vec, vec,
                  pl.BlockSpec((1, HEAD_V), lambda b, h, i: (0, 0))],
        out_specs=pl.BlockSpec((tq, LANES), lambda b, h, i: (b * nq + i, h)),
        out_shape=jax.ShapeDtypeStruct((T, D_MODEL), BF16),
        scratch_shapes=[pltpu.VMEM((2, LANES, tq), BF16),
                        pltpu.VMEM((S // tk, HEAD_V, tk), BF16),
                        pltpu.VMEM((2, tk, tq), F32),
                        pltpu.VMEM((2, tk, tq), F32),
                        pltpu.VMEM((2, 1, tq), F32),
                        pltpu.VMEM((2, 1, tq), F32),
                        pltpu.VMEM((2, HEAD_V, tq), F32)],
        compiler_params=_cparams(("parallel", "parallel", "arbitrary")),
        name="attn",
    )(qk, qk, plain, plain, *[v.reshape(1, DA_QK) for v in lam_vecs], subln_w.reshape(1, HEAD_V))


def _mlstm_kernel(qk_ref, v_ref, om_ref, gm_ref, gi_ref, gf_ref, git_ref, gft_ref, nw_ref,
                  o_ref, c_s, n_s, m_s):
    L = qk_ref.shape[0]

    @pl.when(pl.program_id(1) == 0)
    def _():
        c_s[...] = jnp.zeros(c_s.shape, F32)
        n_s[...] = jnp.zeros(n_s.shape, F32)
        m_s[...] = jnp.zeros(m_s.shape, F32)

    row = lax.broadcasted_iota(jnp.int32, (L, L), 0)
    col = lax.broadcasted_iota(jnp.int32, (L, L), 1)
    tri = (col <= row).astype(F32)
    causal_t = row <= col
    tri_t = causal_t.astype(F32)
    bcols = jnp.dot(tri, _log_sigmoid(gf_ref[...]), precision=HIGHEST, preferred_element_type=F32)
    brows = jnp.dot(_log_sigmoid(gft_ref[...]), tri_t, precision=HIGHEST, preferred_element_type=F32)
    ucols = gi_ref[...] - bcols
    git = git_ref[...]
    lane = lax.broadcasted_iota(jnp.int32, (1, LANES), 1)
    first_row = lax.broadcasted_iota(jnp.int32, (SUBLANES, 1), 0) == 0
    nt = (((1,), (1,)), ((), ()))

    for h in range(HEADS):
        p = h // 2
        hmask = ((lane >= (h % 2) * ML_QK) & (lane < (h % 2 + 1) * ML_QK)).astype(F32)
        qh = (qk_ref[:, p * LANES:(p + 1) * LANES].astype(F32) * hmask).astype(BF16)
        kp = qk_ref[:, (HEADS // 2 + p) * LANES:(HEADS // 2 + p + 1) * LANES]
        vt = v_ref[:, h * HEAD_V:(h + 1) * HEAD_V].T
        br = brows[h:h + 1, :]
        igr = git[h:h + 1, :]
        g_tot = br[:, L - 1:L]
        m_prev = m_s[h]
        ct_prev = c_s[h]
        n_prev = n_s[h]

        dm = jnp.where(causal_t, br + ucols[:, h:h + 1], -jnp.inf)
        m_inter = br + m_prev
        m_j = jnp.maximum(jnp.max(dm, axis=0, keepdims=True), m_inter)
        st = lax.dot_general(kp, qh, nt, preferred_element_type=F32)
        qkw = st * jnp.exp(dm - m_j)
        inter_w = jnp.exp(m_inter - m_j)
        num = (jnp.dot(vt, qkw.astype(BF16), preferred_element_type=F32)
               + inter_w * lax.dot_general(ct_prev.astype(BF16), qh, nt, preferred_element_type=F32))
        qn = lax.dot_general(n_prev.astype(BF16), qh, nt, preferred_element_type=F32)[0:1, :]
        den = jnp.sum(qkw, axis=0, keepdims=True) + inter_w * qn
        ht = num / jnp.maximum(jnp.abs(den), jnp.exp(-m_j))

        a = g_tot - br + igr
        m_loc = jnp.max(a, axis=-1, keepdims=True)
        w_loc = jnp.exp(a - m_loc)
        m_new = jnp.maximum(g_tot + m_prev, m_loc)
        dec = jnp.exp(g_tot + m_prev - m_new)
        inc = jnp.exp(m_loc - m_new)
        c_loc = jnp.dot((vt.astype(F32) * w_loc).astype(BF16), kp, preferred_element_type=F32)
        w8 = jnp.where(first_row, w_loc, 0.0)
        n_loc = jnp.dot(w8, kp.astype(F32), precision=HIGHEST, preferred_element_type=F32)
        c_s[h] = dec * ct_prev + inc * c_loc
        n_s[h] = dec * n_prev + inc * n_loc
        m_s[h] = m_new

        sl = slice(h * HEAD_V, (h + 1) * HEAD_V)
        yt = ht * lax.rsqrt(jnp.mean(ht * ht, axis=0, keepdims=True) + NORM_EPS)
        y = yt.T * nw_ref[:, sl]
        y = y * _sigmoid(om_ref[:, sl].astype(F32)) * _sigmoid(gm_ref[:, sl].astype(F32))
        o_ref[:, sl] = y.astype(o_ref.dtype)


def _mlstm(qkc, plain, gi, gf, git, gft, ml_norm_w, B, S):
    T, D = qkc.shape
    L = ML_CHUNK
    nc = S // L
    wide = lambda cb: pl.BlockSpec((L, D), lambda b, c: (b * nc + c, cb))
    return pl.pallas_call(
        _mlstm_kernel,
        grid=(B, nc),
        in_specs=[wide(0), wide(1), wide(2), wide(4),
                  pl.BlockSpec((L, HEADS), lambda b, c: (b * nc + c, 0)),
                  pl.BlockSpec((L, HEADS), lambda b, c: (b * nc + c, 0)),
                  pl.BlockSpec((HEADS, L), lambda b, c: (0, b * nc + c)),
                  pl.BlockSpec((HEADS, L), lambda b, c: (0, b * nc + c)),
                  pl.BlockSpec((1, D), lambda b, c: (0, 0))],
        out_specs=pl.BlockSpec((L, D), lambda b, c: (b * nc + c, 0)),
        out_shape=jax.ShapeDtypeStruct((T, D), BF16),
        scratch_shapes=[pltpu.VMEM((HEADS, LANES, HEAD_V), F32),
                        pltpu.VMEM((HEADS, SUBLANES, LANES), F32),
                        pltpu.VMEM((HEADS, 1, 1), F32)],
        compiler_params=_cparams(("parallel", "arbitrary")),
        name="mlstm",
    )(qkc, plain, plain, plain, gi, gf, git, gft, ml_norm_w.reshape(1, D))


def _outproj_kernel(ya_ref, ym_ref, x_ref, w_ref, gt_ref, nw_ref, sc_ref, sh_ref, x1_ref, h2t_ref):
    merged = (ya_ref[...].astype(F32) + ym_ref[...].astype(F32)).astype(BF16)
    x1 = x_ref[...] + gt_ref[0] * jnp.dot(merged, w_ref[...], preferred_element_type=F32)
    x1_ref[...] = x1
    h2 = _rms(x1) * nw_ref[...] * (1.0 + sc_ref[0]) + sh_ref[0]
    h2t_ref[...] = h2.T.astype(h2t_ref.dtype)


def _outproj(ya, ym, x2, w_out, norm2_w, mod3, B, tps, tm):
    T, D = x2.shape
    row = lambda k: pl.BlockSpec((1, 1, D), lambda i: (k * B + i // tps, 0, 0))
    tile = pl.BlockSpec((tm, D), lambda i: (i, 0))
    return pl.pallas_call(
        _outproj_kernel,
        grid=(T // tm,),
        in_specs=[tile, tile, tile, pl.BlockSpec((D, D), lambda i: (0, 0)), row(2),
                  pl.BlockSpec((1, D), lambda i: (0, 0)), row(4), row(3)],
        out_specs=[tile, pl.BlockSpec((D, tm), lambda i: (0, i))],
        out_shape=[jax.ShapeDtypeStruct((T, D), F32), jax.ShapeDtypeStruct((D, T), BF16)],
        compiler_params=_cparams(("parallel",)),
        name="outproj",
    )(ya, ym, x2, w_out, mod3, norm2_w.reshape(1, D), mod3, mod3)


def _topk_ranks(s):
    n = s.shape[0]
    it = lax.broadcasted_iota(jnp.int32, s.shape, 0)
    rank = jnp.full(s.shape, float(TOPK), F32)
    x = s
    vals = []
    for r in range(TOPK):
        m = jnp.max(x, axis=0, keepdims=True)
        first = jnp.min(jnp.where(x == m, it, n), axis=0, keepdims=True)
        hit = it == first
        x = jnp.where(hit, -jnp.inf, x)
        rank = jnp.where(hit, float(r), rank)
        vals.append(m)
    return rank, vals


MARK_BASE = 3.0e38
MARK_STEP = 1.0e36


def _topk_ranks_distinct(s):
    x = s
    vals = []
    for r in range(TOPK):
        m = jnp.max(x, axis=0, keepdims=True)
        x = jnp.where(x == m, -(MARK_BASE + r * MARK_STEP), x)
        vals.append(m)
    taken = x <= -MARK_BASE
    rank = jnp.where(taken, jnp.round((-x - MARK_BASE) * (1.0 / MARK_STEP)), float(TOPK))
    removed = jnp.sum(jnp.where(taken, 1.0, 0.0), axis=0, keepdims=True)
    return rank, vals, removed == float(TOPK)


def _cand_select(cand, top):
    it = lax.broadcasted_iota(jnp.int32, cand.shape, 0)
    sel = jnp.zeros(cand.shape, F32)
    z = jnp.zeros(top.shape, F32)
    x = cand
    for _ in range(TOPK):
        m = jnp.max(x, axis=0, keepdims=True)
        first = jnp.min(jnp.where(x == m, it, cand.shape[0]), axis=0, keepdims=True)
        hit = it == first
        x = jnp.where(hit, -jnp.inf, x)
        sel = jnp.where(hit, 1.0, sel)
        z = z + jnp.exp(m - top)
    return sel, z


def _cand_select_distinct(cand, top):
    z = jnp.zeros(top.shape, F32)
    x = cand
    for _ in range(TOPK):
        m = jnp.max(x, axis=0, keepdims=True)
        x = jnp.where(x == m, -MARK_BASE, x)
        z = z + jnp.exp(m - top)
    sel = jnp.where(x == -MARK_BASE, 1.0, 0.0)
    return sel, z, jnp.sum(sel, axis=0, keepdims=True) == float(TOPK)


def _peer_sel_kernel(h2t_ref, wqt_ref, keys_ref, r2_ref, e2_ref, a_ref, c_ref, rank_s, vals_s, sel_s, z_s):
    ht = h2t_ref[...]
    scores = []
    suspect = []
    for p in range(2):
        qt = jnp.dot(wqt_ref[p * LANES:(p + 1) * LANES, :], ht, preferred_element_type=F32)
        s = jnp.dot(keys_ref[p], qt.astype(BF16), preferred_element_type=F32)
        scores.append(s)
        rank, vals, distinct = _topk_ranks_distinct(s)
        rank_s[p] = rank
        vals_s[p] = jnp.concatenate(vals, axis=0)
        suspect.append(jnp.sum(jnp.where(distinct, 0.0, 1.0)) > 0.0)

    for p in range(2):
        @pl.when(suspect[p])
        def _():
            rank, vals = _topk_ranks(scores[p])
            rank_s[p] = rank
            vals_s[p] = jnp.concatenate(vals, axis=0)

    s1, s2 = scores
    rank1, rank2 = rank_s[0], rank_s[1]
    v1 = [vals_s[0, r:r + 1, :] for r in range(TOPK)]
    v2 = [vals_s[1, r:r + 1, :] for r in range(TOPK)]

    tt = s1.shape[1]
    rows = [v1[r1] + v2[r2] for (r1, r2) in _CAND]
    rows += [jnp.full((1, tt), -jnp.inf, F32)] * (_CAND_ROWS - len(_CAND))
    cand = jnp.concatenate(rows, axis=0)
    top = v1[0] + v2[0]
    sel, z, distinct = _cand_select_distinct(cand, top)
    sel_s[...] = sel
    z_s[...] = z

    @pl.when(jnp.sum(jnp.where(distinct, 0.0, 1.0)) > 0.0)
    def _():
        sel, z = _cand_select(cand, top)
        sel_s[...] = sel
        z_s[...] = z

    sel = sel_s[...]
    z = z_s[...]
    a = jnp.zeros(s1.shape, F32)
    start = 0
    for r1 in range(TOPK):
        width = sum(1 for c in _CAND if c[0] == r1)
        cnt = jnp.sum(sel[start:start + width, :], axis=0, keepdims=True)
        a = jnp.where(rank1 == float(r1), cnt, a)
        start += width

    for ref, val in ((r2_ref, rank2), (e2_ref, jnp.exp(s2 - v2[0])), (a_ref, a), (c_ref, jnp.exp(s1 - v1[0]) / z)):
        for lt in range(tt // LANES):
            ref[lt] = val[:, lt * LANES:(lt + 1) * LANES].astype(ref.dtype)


def _peer_sel(h2t, wqt, keys, tt):
    D, T = h2t.shape
    out = pl.BlockSpec((tt // LANES, N_KEYS, LANES), lambda i, h: (i, h, 0))
    shp = lambda dt: jax.ShapeDtypeStruct((T // LANES, HEADS * N_KEYS, LANES), dt)
    return pl.pallas_call(
        _peer_sel_kernel,
        grid=(T // tt, HEADS),
        in_specs=[pl.BlockSpec((D, tt), lambda i, h: (0, i)),
                  pl.BlockSpec((2 * LANES, D), lambda i, h: (h, 0)),
                  pl.BlockSpec((2, N_KEYS, LANES), lambda i, h: (h, 0, 0))],
        out_specs=[out, out, out, out],
        out_shape=[shp(BF16), shp(BF16), shp(F32), shp(F32)],
        scratch_shapes=[pltpu.VMEM((2, N_KEYS, tt), F32), pltpu.VMEM((2, TOPK, tt), F32),
                        pltpu.VMEM((_CAND_ROWS, tt), F32), pltpu.VMEM((1, tt), F32)],
        compiler_params=_cparams(("parallel", "parallel")),
        name="peer_sel",
    )(h2t, wqt, keys)


PEER_ROWS = 64
GATE_GROUP = 8
MXU_COLS = 256
ACT_ROWS = 256
OUT_ROWS = 512


def _gate_rows(row, rows):
    packed = jnp.broadcast_to(row, (2 * SUBLANES, LANES)).astype(BF16)
    return jnp.tile(packed, (rows // (2 * SUBLANES), 1))


def _peer_main_kernel(final, h2t_ref, u_ref, vt_ref, r2_ref, e2_ref, a_ref, c_ref, x1_ref, gt_ref, fw_ref,
                      o_ref, acc_ref, act_ref, p_ref):
    e = pl.program_id(1)
    eb, tt = act_ref.shape
    nj = eb // N_KEYS

    @pl.when(e == 0)
    def _():
        acc_ref[...] = jnp.zeros(acc_ref.shape, F32)

    n_half = tt // MXU_COLS
    tcols = lambda th: slice(th * MXU_COLS, (th + 1) * MXU_COLS)

    def act_piece(th, m):
        ms = slice(m * ACT_ROWS, (m + 1) * ACT_ROWS)
        act_ref[ms, tcols(th)] = jnp.dot(u_ref[ms, :], h2t_ref[:, tcols(th)], preferred_element_type=F32)

    def out_piece(th, r):
        rs = slice(r * OUT_ROWS, (r + 1) * OUT_ROWS)
        acc_ref[rs, tcols(th)] += jnp.dot(vt_ref[rs, :], p_ref[:, tcols(th)], preferred_element_type=F32)

    def gate_chunk(tl, bs):
        ls = slice(tl * LANES, (tl + 1) * LANES)
        for j0 in range(0, nj, GATE_GROUP):
            g = [None] * GATE_GROUP
            for h in range(HEADS):
                rs = slice(h * N_KEYS + bs * PEER_ROWS, h * N_KEYS + (bs + 1) * PEER_ROWS)
                r2c = r2_ref[tl, rs, :]
                e2c = e2_ref[tl, rs, :]
                for jj in range(GATE_GROUP):
                    idx = h * N_KEYS + e * nj + j0 + jj
                    arow = _gate_rows(a_ref[tl, pl.ds(idx, 1), :], PEER_ROWS)
                    crow = _gate_rows(c_ref[tl, pl.ds(idx, 1), :], PEER_ROWS)
                    w = jnp.where(r2c < arow, e2c, jnp.zeros_like(e2c)) * crow
                    g[jj] = w if g[jj] is None else g[jj] + w
            for jj in range(GATE_GROUP):
                es = slice((j0 + jj) * N_KEYS + bs * PEER_ROWS, (j0 + jj) * N_KEYS + (bs + 1) * PEER_ROWS)
                aj = act_ref[es, ls]
                gelu = 0.5 * aj * (1.0 + lax.erf(aj * (2.0 ** -0.5)))
                p_ref[es, ls] = g[jj] * gelu.astype(BF16)

    n_act = eb // ACT_ROWS
    n_out = acc_ref.shape[0] // OUT_ROWS
    for m in range(n_act):
        act_piece(0, m)
    for th in range(n_half):
        mxu_work = [functools.partial(act_piece, th + 1, m) for m in range(n_act)] if th + 1 < n_half else []
        if th > 0:
            mxu_work += [functools.partial(out_piece, th - 1, r) for r in range(n_out)]
        chunks = [(th * (MXU_COLS // LANES) + lt, bs)
                  for lt in range(MXU_COLS // LANES) for bs in range(N_KEYS // PEER_ROWS)]
        per_chunk = -(-len(mxu_work) // len(chunks))
        for tl, bs in chunks:
            gate_chunk(tl, bs)
            for piece in mxu_work[:per_chunk]:
                piece()
            mxu_work = mxu_work[per_chunk:]
    for r in range(n_out):
        out_piece(n_half - 1, r)

    @pl.when(e == pl.num_programs(1) - 1)
    def _():
        x2 = x1_ref[...] + gt_ref[0] * acc_ref[...].T
        o_ref[...] = _rms(x2) * fw_ref[...] if final else x2


def _peer_main(h2t, u, vt, sel, x1, mod3, fw, final, B, tps, tt, eb):
    D, T = h2t.shape
    ne = u.shape[0]
    wide = pl.BlockSpec((tt // LANES, HEADS * N_KEYS, LANES), lambda i, e: (i, 0, 0))
    return pl.pallas_call(
        functools.partial(_peer_main_kernel, final),
        grid=(T // tt, ne // eb),
        in_specs=[pl.BlockSpec((D, tt), lambda i, e: (0, i)),
                  pl.BlockSpec((eb, D), lambda i, e: (e, 0)),
                  pl.BlockSpec((D, eb), lambda i, e: (0, e)),
                  wide, wide, wide, wide,
                  pl.BlockSpec((tt, D), lambda i, e: (i, 0)),
                  pl.BlockSpec((1, 1, D), lambda i, e: (5 * B + i // tps, 0, 0)),
                  pl.BlockSpec((1, D), lambda i, e: (0, 0))],
        out_specs=pl.BlockSpec((tt, D), lambda i, e: (i, 0)),
        out_shape=jax.ShapeDtypeStruct((T, D), F32),
        scratch_shapes=[pltpu.VMEM((D, tt), F32),
                        pltpu.VMEM((eb, tt), F32),
                        pltpu.VMEM((eb, tt), BF16)],
        compiler_params=_cparams(("parallel", "arbitrary")),
        name="peer_main",
    )(h2t, u, vt, *sel, x1, mod3, fw.reshape(1, D))


def _rope_tables(S):
    inv = ROPE_THETA ** (-jnp.arange(ROPE_HALF, dtype=F32) * 2.0 / ROPE_DIM)
    ang = jnp.arange(S, dtype=jnp.int32).astype(F32)[:, None] * inv[None, :]
    cos, sin = jnp.cos(ang), jnp.sin(ang)
    zeros = jnp.zeros((S, DA_QK - ROPE_DIM), F32)
    z8 = jnp.zeros((S, ROPE_HALF), F32)
    cosf = jnp.concatenate([cos, cos, zeros + 1.0] * 2, axis=1)
    sa = jnp.concatenate([-sin, z8, zeros] * 2, axis=1)
    sb = jnp.concatenate([z8, sin, zeros] * 2, axis=1)
    return cosf, sa, sb


def kernel(x, c, ada_w, ada_b, norm1_w, norm2_w, w_in, conv_w, conv_b, ml_i_bias, ml_f_bias, ml_norm_w, lam_q1, lam_k1, lam_q2, lam_k2, subln_w, w_out, peer_wq, peer_keys, peer_u, peer_v, final_norm_w):
    B, S, D = x.shape
    assert D == D_MODEL and S % PROJ_TM == 0 and S % ATTN_TQ == 0
    T = B * S
    depth = w_in.shape[0]
    tm = 512
    tps = S // tm
    tables = _rope_tables(S)
    x2 = x.reshape(T, D)
    for l in range(depth):
        mod3 = _mod(c, ada_w[l], ada_b[l])
        h = _norm(x2, norm1_w[l], mod3, B, tps, tm)

        wl = w_in[l]
        o = 0
        cols = {}
        for name, width in (("qa", D), ("ka", D), ("va", D), ("qm", D // 2), ("km", D // 2), ("vm", D),
                            ("om", D), ("ip", HEADS), ("fp", HEADS), ("ga", D), ("gm", D)):
            cols[name] = wl[:, o:o + width]
            o += width
        w_rope = jnp.concatenate([cols["qa"], cols["ka"]], axis=1).astype(BF16)
        w_conv = jnp.concatenate([cols["qm"], cols["km"]], axis=1).astype(BF16)
        w_plain = jnp.concatenate([cols[n] for n in ("va", "vm", "om", "ga", "gm")], axis=1).astype(BF16)

        qk = _rope_proj(h, w_rope, tables, S // PROJ_TM, PROJ_TM)
        plain = _proj(h, w_plain, PROJ_TM, D)
        qkc, gi, gf, git, gft = _conv_proj(h, w_conv, cols["ip"].astype(BF16), cols["fp"].astype(BF16),
                                           conv_w[l], conv_b[l], ml_i_bias[l], ml_f_bias[l], tps, tm)

        lam_init = 0.8 - 0.6 * math.exp(-0.3 * l)
        ya = _attention(qk, plain, (lam_q1[l], lam_k1[l], lam_q2[l], lam_k2[l]), subln_w[l], lam_init, B, S, ATTN_TQ, ATTN_TK)
        ym = _mlstm(qkc, plain, gi, gf, git, gft, ml_norm_w[l], B, S)
        x1, h2t = _outproj(ya, ym, x2, w_out[l].astype(BF16), norm2_w[l], mod3, B, tps, tm)

        wqt = peer_wq[l].T.astype(BF16)
        keys = peer_keys[l].reshape(2 * HEADS, N_KEYS, LANES).astype(BF16)
        sel = _peer_sel(h2t, wqt, keys, tm)
        x2 = _peer_main(h2t, peer_u[l].astype(BF16), peer_v[l].T.astype(BF16), sel, x1, mod3,
                        final_norm_w, l == depth - 1, B, tps, tm, 16 * N_KEYS)
    if depth == 0:
        raise ValueError("depth must be positive")
    return x2.reshape(B, S, D)
```

```python
import functools
import math

import jax
import jax.numpy as jnp
from jax import lax
from jax.experimental import pallas as pl
from jax.experimental.pallas import tpu as pltpu

F32 = jnp.float32
BF16 = jnp.bfloat16
HIGHEST = lax.Precision.HIGHEST

NORM_EPS = 1e-6
LOG2E = 1.4426950408889634
D_MODEL = 1024
HEADS = 8
HEAD_V = 128
DA_QK = 64
ROPE_DIM = 16
ROPE_HALF = 8
ROPE_THETA = 500000.0
ML_QK = 64
CONV_WIDTH = 4
ML_CHUNK = 128
PROJ_TM = 1024
ATTN_TQ = 512
ATTN_TK = 512
N_KEYS = 128
TOPK = 16
LANES = 128
SUBLANES = 8
VMEM_LIMIT = 56 * 1024 * 1024

_CAND = [(r1, r2) for r1 in range(TOPK) for r2 in range(TOPK) if (r1 + 1) * (r2 + 1) <= TOPK]
_CAND_ROWS = -(-len(_CAND) // SUBLANES) * SUBLANES


def _cparams(sem):
    return pltpu.CompilerParams(dimension_semantics=sem, vmem_limit_bytes=VMEM_LIMIT)


def _rms(x):
    return x * lax.rsqrt(jnp.mean(x * x, axis=-1, keepdims=True) + NORM_EPS)


def _sigmoid(x):
    return 1.0 / (1.0 + jnp.exp(-x))


def _log_sigmoid(x):
    return jnp.minimum(x, 0.0) - jnp.log(1.0 + jnp.exp(-jnp.abs(x)))


def _mod_kernel(c_ref, w_ref, b_ref, o_ref):
    c = c_ref[...]
    cond = c * _sigmoid(c)
    o_ref[0] = jnp.dot(cond, w_ref[...], precision=HIGHEST, preferred_element_type=F32) + b_ref[...]


def _mod(c, ada_w, ada_b):
    B, D = c.shape
    out = pl.pallas_call(
        _mod_kernel,
        grid=(6,),
        in_specs=[pl.BlockSpec((B, D), lambda j: (0, 0)),
                  pl.BlockSpec((D, D), lambda j: (0, j)),
                  pl.BlockSpec((1, D), lambda j: (0, j))],
        out_specs=pl.BlockSpec((1, B, D), lambda j: (j, 0, 0)),
        out_shape=jax.ShapeDtypeStruct((6, B, D), F32),
        compiler_params=_cparams(("parallel",)),
        name="mod",
    )(c, ada_w, ada_b.reshape(1, 6 * D))
    return out.reshape(6 * B, 1, D)


def _norm_kernel(x_ref, w_ref, sc_ref, sh_ref, o_ref):
    y = _rms(x_ref[...]) * w_ref[...]
    o_ref[...] = (y * (1.0 + sc_ref[0]) + sh_ref[0]).astype(o_ref.dtype)


def _norm(x2, w, mod3, B, tps, tm):
    T, D = x2.shape
    return pl.pallas_call(
        _norm_kernel,
        grid=(T // tm,),
        in_specs=[pl.BlockSpec((tm, D), lambda i: (i, 0)),
                  pl.BlockSpec((1, D), lambda i: (0, 0)),
                  pl.BlockSpec((1, 1, D), lambda i: (1 * B + i // tps, 0, 0)),
                  pl.BlockSpec((1, 1, D), lambda i: (0 * B + i // tps, 0, 0))],
        out_specs=pl.BlockSpec((tm, D), lambda i: (i, 0)),
        out_shape=jax.ShapeDtypeStruct((T, D), BF16),
        compiler_params=_cparams(("parallel",)),
        name="norm1",
    )(x2, w.reshape(1, D), mod3, mod3)


def _proj_kernel(h_ref, w_ref, o_ref):
    o_ref[...] = jnp.dot(h_ref[...], w_ref[...], preferred_element_type=F32).astype(o_ref.dtype)


def _proj(h, w, tm, tn):
    T, D = h.shape
    N = w.shape[1]
    return pl.pallas_call(
        _proj_kernel,
        grid=(N // tn, T // tm),
        in_specs=[pl.BlockSpec((tm, D), lambda j, i: (i, 0)),
                  pl.BlockSpec((D, tn), lambda j, i: (0, j))],
        out_specs=pl.BlockSpec((tm, tn), lambda j, i: (i, j)),
        out_shape=jax.ShapeDtypeStruct((T, N), BF16),
        compiler_params=_cparams(("parallel", "parallel")),
        name="proj_plain",
    )(h, w)


def _rope_proj_kernel(h_ref, w_ref, cos_ref, sa_ref, sb_ref, o_ref):
    acc = jnp.dot(h_ref[...], w_ref[...], preferred_element_type=F32)
    scale = jnp.where(pl.program_id(0) == 0, DA_QK ** -0.5 * LOG2E, 1.0).astype(F32)
    cosf, sa, sb = cos_ref[...], sa_ref[...], sb_ref[...]
    for hh in range(HEADS):
        blk = acc[:, hh * LANES:(hh + 1) * LANES]
        rot = (blk * cosf + pltpu.roll(blk, LANES - ROPE_HALF, 1) * sa
               + pltpu.roll(blk, ROPE_HALF, 1) * sb)
        o_ref[:, hh * LANES:(hh + 1) * LANES] = (rot * scale).astype(o_ref.dtype)


def _rope_proj(h, w, tables, tps, tm):
    T, D = h.shape
    cosf, sa, sb = tables
    tab_spec = pl.BlockSpec((tm, LANES), lambda j, i: (i % tps, 0))
    return pl.pallas_call(
        _rope_proj_kernel,
        grid=(2, T // tm),
        in_specs=[pl.BlockSpec((tm, D), lambda j, i: (i, 0)),
                  pl.BlockSpec((D, D), lambda j, i: (0, j)),
                  tab_spec, tab_spec, tab_spec],
        out_specs=pl.BlockSpec((tm, D), lambda j, i: (i, j)),
        out_shape=jax.ShapeDtypeStruct((T, 2 * D), BF16),
        compiler_params=_cparams(("parallel", "parallel")),
        name="proj_rope",
    )(h, w, cosf, sa, sb)


def _conv_proj_kernel(tps, h_ref, w_ref, wgt_ref, cw_ref, cb_ref, bi_ref, bf_ref, bit_ref, bft_ref,
                      o_ref, gi_ref, gf_ref, git_ref, gft_ref, buf):
    i = pl.program_id(0)
    tm = h_ref.shape[0]
    n = o_ref.shape[1]
    h = h_ref[...]
    full = jnp.dot(h, w_ref[...], preferred_element_type=F32)
    acc = full[:, :n]

    @pl.when(i % tps == 0)
    def _():
        buf[0:SUBLANES, :] = jnp.zeros((SUBLANES, n), F32)

    buf[SUBLANES:SUBLANES + tm, :] = acc
    cw = cw_ref[...]
    y = cb_ref[...]
    for j in range(CONV_WIDTH):
        off = SUBLANES - (CONV_WIDTH - 1) + j
        y = y + buf[off:off + tm, :] * cw[j:j + 1, :]
    buf[0:SUBLANES, :] = buf[tm:tm + SUBLANES, :]
    y = y * _sigmoid(y)
    lane = lax.broadcasted_iota(jnp.int32, (1, n), 1)
    kscale = jnp.where(lane >= HEADS * ML_QK, ML_QK ** -0.5, 1.0).astype(F32)
    o_ref[...] = (y * kscale).astype(o_ref.dtype)

    gates = full[:, n:n + LANES]
    gi_ref[...] = gates[:, :HEADS] + bi_ref[...]
    gf_ref[...] = gates[:, HEADS:2 * HEADS] + bf_ref[...]
    gates_t = lax.dot_general(wgt_ref[...], h, (((1,), (1,)), ((), ())), preferred_element_type=F32)
    git_ref[...] = gates_t[:HEADS] + bit_ref[...]
    gft_ref[...] = gates_t[HEADS:] + bft_ref[...]


def _conv_proj(h, w, wgi, wgf, conv_w, conv_b, bi, bf, tps, tm):
    T, D = h.shape
    N = w.shape[1]
    wg = jnp.concatenate([wgi, wgf], axis=1)
    w_ext = jnp.concatenate([w, jnp.pad(wg, ((0, 0), (0, LANES - 2 * HEADS)))], axis=1)
    full = lambda shape: pl.BlockSpec(shape, lambda i: tuple(0 for _ in shape))
    return pl.pallas_call(
        functools.partial(_conv_proj_kernel, tps),
        grid=(T // tm,),
        in_specs=[pl.BlockSpec((tm, D), lambda i: (i, 0)),
                  full((D, N + LANES)), full((2 * HEADS, D)),
                  full((CONV_WIDTH, N)), full((1, N)),
                  full((1, HEADS)), full((1, HEADS)), full((HEADS, 1)), full((HEADS, 1))],
        out_specs=[pl.BlockSpec((tm, N), lambda i: (i, 0)),
                   pl.BlockSpec((tm, HEADS), lambda i: (i, 0)),
                   pl.BlockSpec((tm, HEADS), lambda i: (i, 0)),
                   pl.BlockSpec((HEADS, tm), lambda i: (0, i)),
                   pl.BlockSpec((HEADS, tm), lambda i: (0, i))],
        out_shape=[jax.ShapeDtypeStruct((T, N), BF16),
                   jax.ShapeDtypeStruct((T, HEADS), F32),
                   jax.ShapeDtypeStruct((T, HEADS), F32),
                   jax.ShapeDtypeStruct((HEADS, T), F32),
                   jax.ShapeDtypeStruct((HEADS, T), F32)],
        scratch_shapes=[pltpu.VMEM((tm + 2 * SUBLANES, N), F32)],
        compiler_params=_cparams(("arbitrary",)),
        name="proj_conv",
    )(h, w_ext, wg.T, conv_w, conv_b.reshape(1, N),
      bi.reshape(1, HEADS), bf.reshape(1, HEADS), bi.reshape(HEADS, 1), bf.reshape(HEADS, 1))


def _attn_kernel(lam_init, q_ref, k_ref, v_ref, ga_ref, lq1_ref, lk1_ref, lq2_ref, lk2_ref, sw_ref,
                 o_ref, qt_s, vt_s, sa_s, sb_s, m_s, l_s, acc_s):
    qi = pl.program_id(2)
    tq = q_ref.shape[0]
    tk = vt_s.shape[2]
    ratio = tq // tk

    @pl.when(qi == 0)
    def _():
        for kk in range(vt_s.shape[0]):
            vt_s[kk] = v_ref[kk * tk:(kk + 1) * tk, :].T

    q = q_ref[...].astype(F32)
    lane = lax.broadcasted_iota(jnp.int32, q.shape, 1)
    qt_s[0] = jnp.where(lane < DA_QK, q, 0.0).T.astype(BF16)
    qt_s[1] = jnp.where(lane >= DA_QK, q, 0.0).T.astype(BF16)
    m_s[...] = jnp.full(m_s.shape, -jnp.inf, F32)
    l_s[...] = jnp.zeros(l_s.shape, F32)
    acc_s[...] = jnp.zeros(acc_s.shape, F32)

    def scores(kk, st_ref):
        k = k_ref[pl.ds(pl.multiple_of(kk * tk, tk), tk), :]
        for c in range(2):
            st_ref[c] = jnp.dot(k, qt_s[c], preferred_element_type=F32)

    def process(kk, st_ref, diag_offset):
        vt = vt_s[kk]
        for c in range(2):
            st = st_ref[c]
            if diag_offset is not None:
                key = lax.broadcasted_iota(jnp.int32, st.shape, 0) + diag_offset
                qry = lax.broadcasted_iota(jnp.int32, st.shape, 1)
                st = jnp.where(key <= qry, st, -jnp.inf)
            m_prev = m_s[c]
            m_new = jnp.maximum(m_prev, jnp.max(st, axis=0, keepdims=True))
            alpha = jnp.exp2(m_prev - m_new)
            p = jnp.exp2(st - m_new)
            l_s[c] = alpha * l_s[c] + jnp.sum(p, axis=0, keepdims=True)
            acc_s[c] = alpha * acc_s[c] + jnp.dot(vt, p.astype(BF16), preferred_element_type=F32)
            m_s[c] = m_new

    def diagonal(first, cur, nxt):
        for d in range(ratio):
            if d + 1 < ratio:
                scores(first + d + 1, nxt)
            process(first + d, cur, d * tk)
            cur, nxt = nxt, cur

    n_below = qi * ratio
    scores(0, sa_s)

    def pair(i, carry):
        kk = 2 * i
        scores(kk + 1, sb_s)
        process(kk, sa_s, None)
        scores(kk + 2, sa_s)
        process(kk + 1, sb_s, None)
        return carry

    lax.fori_loop(0, n_below // 2, pair, 0)
    odd = lax.rem(n_below, 2) == 1

    @pl.when(odd)
    def _():
        scores(n_below, sb_s)
        process(n_below - 1, sa_s, None)
        diagonal(n_below, sb_s, sa_s)

    @pl.when(jnp.logical_not(odd))
    def _():
        diagonal(n_below, sa_s, sb_s)

    lam = (jnp.exp(jnp.sum(lq1_ref[...] * lk1_ref[...], axis=-1, keepdims=True))
           - jnp.exp(jnp.sum(lq2_ref[...] * lk2_ref[...], axis=-1, keepdims=True)) + lam_init)
    o = (acc_s[0] / l_s[0] - lam * (acc_s[1] / l_s[1])).T
    o = _rms(o) * sw_ref[...] * (1.0 - lam_init)
    o_ref[...] = (o * _sigmoid(ga_ref[...].astype(F32))).astype(o_ref.dtype)


def _attention(qk, plain, lam_vecs, subln_w, lam_init, B, S, tq, tk):
    T = qk.shape[0]
    nq = S // tq
    vec = pl.BlockSpec((1, DA_QK), lambda b, h, i: (0, 0))
    return pl.pallas_call(
        functools.partial(_attn_kernel, lam_init),
        grid=(B, HEADS, nq),
        in_specs=[pl.BlockSpec((tq, LANES), lambda b, h, i: (b * nq + i, h)),
                  pl.BlockSpec((S, LANES), lambda b, h, i: (b, HEADS + h)),
                  pl.BlockSpec((S, LANES), lambda b, h, i: (b, h)),
                  pl.BlockSpec((tq, LANES), lambda b, h, i: (b * nq + i, 3 * HEADS + h)),
                  vec, vec, vec, vec,
                  pl.BlockSpec((1, HEAD_V), lambda b, h, i: (0, 0))],
        out_specs=pl.BlockSpec((tq, LANES), lambda b, h, i: (b * nq + i, h)),
        out_shape=jax.ShapeDtypeStruct((T, D_MODEL), BF16),
        scratch_shapes=[pltpu.VMEM((2, LANES, tq), BF16),
                        pltpu.VMEM((S // tk, HEAD_V, tk), BF16),
                        pltpu.VMEM((2, tk, tq), F32),
                        pltpu.VMEM((2, tk, tq), F32),
                        pltpu.VMEM((2, 1, tq), F32),
                        pltpu.VMEM((2, 1, tq), F32),
                        pltpu.VMEM((2, HEAD_V, tq), F32)],
        compiler_params=_cparams(("parallel", "parallel", "arbitrary")),
        name="attn",
    )(qk, qk, plain, plain, *[v.reshape(1, DA_QK) for v in lam_vecs], subln_w.reshape(1, HEAD_V))


def _mlstm_kernel(qk_ref, v_ref, om_ref, gm_ref, gi_ref, gf_ref, git_ref, gft_ref, nw_ref,
                  o_ref, c_s, n_s, m_s):
    L = qk_ref.shape[0]

    @pl.when(pl.program_id(1) == 0)
    def _():
        c_s[...] = jnp.zeros(c_s.shape, F32)
        n_s[...] = jnp.zeros(n_s.shape, F32)
        m_s[...] = jnp.zeros(m_s.shape, F32)

    row = lax.broadcasted_iota(jnp.int32, (L, L), 0)
    col = lax.broadcasted_iota(jnp.int32, (L, L), 1)
    tri = (col <= row).astype(F32)
    causal_t = row <= col
    tri_t = causal_t.astype(F32)
    bcols = jnp.dot(tri, _log_sigmoid(gf_ref[...]), precision=HIGHEST, preferred_element_type=F32)
    brows = jnp.dot(_log_sigmoid(gft_ref[...]), tri_t, precision=HIGHEST, preferred_element_type=F32)
    ucols = gi_ref[...] - bcols
    git = git_ref[...]
    lane = lax.broadcasted_iota(jnp.int32, (1, LANES), 1)
    first_row = lax.broadcasted_iota(jnp.int32, (SUBLANES, 1), 0) == 0
    nt = (((1,), (1,)), ((), ()))

    for h in range(HEADS):
        p = h // 2
        hmask = ((lane >= (h % 2) * ML_QK) & (lane < (h % 2 + 1) * ML_QK)).astype(F32)
        qh = (qk_ref[:, p * LANES:(p + 1) * LANES].astype(F32) * hmask).astype(BF16)
        kp = qk_ref[:, (HEADS // 2 + p) * LANES:(HEADS // 2 + p + 1) * LANES]
        vt = v_ref[:, h * HEAD_V:(h + 1) * HEAD_V].T
        br = brows[h:h + 1, :]
        igr = git[h:h + 1, :]
        g_tot = br[:, L - 1:L]
        m_prev = m_s[h]
        ct_prev = c_s[h]
        n_prev = n_s[h]

        dm = jnp.where(causal_t, br + ucols[:, h:h + 1], -jnp.inf)
        m_inter = br + m_prev
        m_j = jnp.maximum(jnp.max(dm, axis=0, keepdims=True), m_inter)
        st = lax.dot_general(kp, qh, nt, preferred_element_type=F32)
        qkw = st * jnp.exp(dm - m_j)
        inter_w = jnp.exp(m_inter - m_j)
        num = (jnp.dot(vt, qkw.astype(BF16), preferred_element_type=F32)
               + inter_w * lax.dot_general(ct_prev.astype(BF16), qh, nt, preferred_element_type=F32))
        qn = lax.dot_general(n_prev.astype(BF16), qh, nt, preferred_element_type=F32)[0:1, :]
        den = jnp.sum(qkw, axis=0, keepdims=True) + inter_w * qn
        ht = num / jnp.maximum(jnp.abs(den), jnp.exp(-m_j))

        a = g_tot - br + igr
        m_loc = jnp.max(a, axis=-1, keepdims=True)
        w_loc = jnp.exp(a - m_loc)
        m_new = jnp.maximum(g_tot + m_prev, m_loc)
        dec = jnp.exp(g_tot + m_prev - m_new)
        inc = jnp.exp(m_loc - m_new)
        c_loc = jnp.dot((vt.astype(F32) * w_loc).astype(BF16), kp, preferred_element_type=F32)
        w8 = jnp.where(first_row, w_loc, 0.0)
        n_loc = jnp.dot(w8, kp.astype(F32), precision=HIGHEST, preferred_element_type=F32)
        c_s[h] = dec * ct_prev + inc * c_loc
        n_s[h] = dec * n_prev + inc * n_loc
        m_s[h] = m_new

        sl = slice(h * HEAD_V, (h + 1) * HEAD_V)
        yt = ht * lax.rsqrt(jnp.mean(ht * ht, axis=0, keepdims=True) + NORM_EPS)
        y = yt.T * nw_ref[:, sl]
        y = y * _sigmoid(om_ref[:, sl].astype(F32)) * _sigmoid(gm_ref[:, sl].astype(F32))
        o_ref[:, sl] = y.astype(o_ref.dtype)


def _mlstm(qkc, plain, gi, gf, git, gft, ml_norm_w, B, S):
    T, D = qkc.shape
    L = ML_CHUNK
    nc = S // L
    wide = lambda cb: pl.BlockSpec((L, D), lambda b, c: (b * nc + c, cb))
    return pl.pallas_call(
        _mlstm_kernel,
        grid=(B, nc),
        in_specs=[wide(0), wide(1), wide(2), wide(4),
                  pl.BlockSpec((L, HEADS), lambda b, c: (b * nc + c, 0)),
                  pl.BlockSpec((L, HEADS), lambda b, c: (b * nc + c, 0)),
                  pl.BlockSpec((HEADS, L), lambda b, c: (0, b * nc + c)),
                  pl.BlockSpec((HEADS, L), lambda b, c: (0, b * nc + c)),
                  pl.BlockSpec((1, D), lambda b, c: (0, 0))],
        out_specs=pl.BlockSpec((L, D), lambda b, c: (b * nc + c, 0)),
        out_shape=jax.ShapeDtypeStruct((T, D), BF16),
        scratch_shapes=[pltpu.VMEM((HEADS, LANES, HEAD_V), F32),
                        pltpu.VMEM((HEADS, SUBLANES, LANES), F32),
                        pltpu.VMEM((HEADS, 1, 1), F32)],
        compiler_params=_cparams(("parallel", "arbitrary")),
        name="mlstm",
    )(qkc, plain, plain, plain, gi, gf, git, gft, ml_norm_w.reshape(1, D))


def _outproj_kernel(ya_ref, ym_ref, x_ref, w_ref, gt_ref, nw_ref, sc_ref, sh_ref, x1_ref, h2t_ref):
    merged = (ya_ref[...].astype(F32) + ym_ref[...].astype(F32)).astype(BF16)
    x1 = x_ref[...] + gt_ref[0] * jnp.dot(merged, w_ref[...], preferred_element_type=F32)
    x1_ref[...] = x1
    h2 = _rms(x1) * nw_ref[...] * (1.0 + sc_ref[0]) + sh_ref[0]
    h2t_ref[...] = h2.T.astype(h2t_ref.dtype)


def _outproj(ya, ym, x2, w_out, norm2_w, mod3, B, tps, tm):
    T, D = x2.shape
    row = lambda k: pl.BlockSpec((1, 1, D), lambda i: (k * B + i // tps, 0, 0))
    tile = pl.BlockSpec((tm, D), lambda i: (i, 0))
    return pl.pallas_call(
        _outproj_kernel,
        grid=(T // tm,),
        in_specs=[tile, tile, tile, pl.BlockSpec((D, D), lambda i: (0, 0)), row(2),
                  pl.BlockSpec((1, D), lambda i: (0, 0)), row(4), row(3)],
        out_specs=[tile, pl.BlockSpec((D, tm), lambda i: (0, i))],
        out_shape=[jax.ShapeDtypeStruct((T, D), F32), jax.ShapeDtypeStruct((D, T), BF16)],
        compiler_params=_cparams(("parallel",)),
        name="outproj",
    )(ya, ym, x2, w_out, mod3, norm2_w.reshape(1, D), mod3, mod3)


def _topk_ranks(s):
    n = s.shape[0]
    it = lax.broadcasted_iota(jnp.int32, s.shape, 0)
    rank = jnp.full(s.shape, float(TOPK), F32)
    x = s
    vals = []
    for r in range(TOPK):
        m = jnp.max(x, axis=0, keepdims=True)
        first = jnp.min(jnp.where(x == m, it, n), axis=0, keepdims=True)
        hit = it == first
        x = jnp.where(hit, -jnp.inf, x)
        rank = jnp.where(hit, float(r), rank)
        vals.append(m)
    return rank, vals


MARK_BASE = 3.0e38
MARK_STEP = 1.0e36


def _topk_ranks_distinct(s):
    x = s
    vals = []
    for r in range(TOPK):
        m = jnp.max(x, axis=0, keepdims=True)
        x = jnp.where(x == m, -(MARK_BASE + r * MARK_STEP), x)
        vals.append(m)
    taken = x <= -MARK_BASE
    rank = jnp.where(taken, jnp.round((-x - MARK_BASE) * (1.0 / MARK_STEP)), float(TOPK))
    removed = jnp.sum(jnp.where(taken, 1.0, 0.0), axis=0, keepdims=True)
    return rank, vals, removed == float(TOPK)


def _cand_select(cand, top):
    it = lax.broadcasted_iota(jnp.int32, cand.shape, 0)
    sel = jnp.zeros(cand.shape, F32)
    z = jnp.zeros(top.shape, F32)
    x = cand
    for _ in range(TOPK):
        m = jnp.max(x, axis=0, keepdims=True)
        first = jnp.min(jnp.where(x == m, it, cand.shape[0]), axis=0, keepdims=True)
        hit = it == first
        x = jnp.where(hit, -jnp.inf, x)
        sel = jnp.where(hit, 1.0, sel)
        z = z + jnp.exp(m - top)
    return sel, z


def _cand_select_distinct(cand, top):
    z = jnp.zeros(top.shape, F32)
    x = cand
    for _ in range(TOPK):
        m = jnp.max(x, axis=0, keepdims=True)
        x = jnp.where(x == m, -MARK_BASE, x)
        z = z + jnp.exp(m - top)
    sel = jnp.where(x == -MARK_BASE, 1.0, 0.0)
    return sel, z, jnp.sum(sel, axis=0, keepdims=True) == float(TOPK)


def _selection(s1, s2, exact):
    if exact:
        (rank1, v1), (rank2, v2) = _topk_ranks(s1), _topk_ranks(s2)
    else:
        (rank1, v1, ok1), (rank2, v2, ok2) = _topk_ranks_distinct(s1), _topk_ranks_distinct(s2)
    tt = s1.shape[1]
    rows = [v1[r1] + v2[r2] for (r1, r2) in _CAND]
    rows += [jnp.full((1, tt), -jnp.inf, F32)] * (_CAND_ROWS - len(_CAND))
    cand = jnp.concatenate(rows, axis=0)
    top = v1[0] + v2[0]
    if exact:
        sel, z = _cand_select(cand, top)
        safe = None
    else:
        sel, z, ok3 = _cand_select_distinct(cand, top)
        safe = ok1 & ok2 & ok3
    a = jnp.zeros(s1.shape, F32)
    start = 0
    for r1 in range(TOPK):
        width = sum(1 for c in _CAND if c[0] == r1)
        cnt = jnp.sum(sel[start:start + width, :], axis=0, keepdims=True)
        a = jnp.where(rank1 == float(r1), cnt, a)
        start += width
    return (rank2, jnp.exp(s2 - v2[0]), a, jnp.exp(s1 - v1[0]) / z), safe


def _peer_sel_kernel(h2t_ref, wqt_ref, keys_ref, r2_ref, e2_ref, a_ref, c_ref):
    ht = h2t_ref[...]
    scores = []
    for p in range(2):
        qt = jnp.dot(wqt_ref[p * LANES:(p + 1) * LANES, :], ht, preferred_element_type=F32)
        scores.append(jnp.dot(keys_ref[p], qt.astype(BF16), preferred_element_type=F32))

    def write(vals):
        for ref, val in zip((r2_ref, e2_ref, a_ref, c_ref), vals):
            for lt in range(val.shape[1] // LANES):
                ref[lt] = val[:, lt * LANES:(lt + 1) * LANES].astype(ref.dtype)

    vals, safe = _selection(*scores, exact=False)
    write(vals)

    @pl.when(jnp.sum(jnp.where(safe, 0.0, 1.0)) > 0.0)
    def _():
        write(_selection(*scores, exact=True)[0])


def _peer_sel(h2t, wqt, keys, tt):
    D, T = h2t.shape
    out = pl.BlockSpec((tt // LANES, N_KEYS, LANES), lambda i, h: (i, h, 0))
    shp = lambda dt: jax.ShapeDtypeStruct((T // LANES, HEADS * N_KEYS, LANES), dt)
    return pl.pallas_call(
        _peer_sel_kernel,
        grid=(T // tt, HEADS),
        in_specs=[pl.BlockSpec((D, tt), lambda i, h: (0, i)),
                  pl.BlockSpec((2 * LANES, D), lambda i, h: (h, 0)),
                  pl.BlockSpec((2, N_KEYS, LANES), lambda i, h: (h, 0, 0))],
        out_specs=[out, out, out, out],
        out_shape=[shp(BF16), shp(BF16), shp(F32), shp(F32)],
        compiler_params=_cparams(("parallel", "parallel")),
        name="peer_sel",
    )(h2t, wqt, keys)


PEER_ROWS = 64
GATE_GROUP = 8
MXU_COLS = 256
ACT_ROWS = 256
OUT_ROWS = 512


def _gate_rows(row, rows):
    packed = jnp.broadcast_to(row, (2 * SUBLANES, LANES)).astype(BF16)
    return jnp.tile(packed, (rows // (2 * SUBLANES), 1))


def _peer_main_kernel(final, h2t_ref, u_ref, vt_ref, r2_ref, e2_ref, a_ref, c_ref, x1_ref, gt_ref, fw_ref,
                      o_ref, acc_ref, act_ref, p_ref):
    e = pl.program_id(1)
    eb, tt = act_ref.shape
    nj = eb // N_KEYS

    @pl.when(e == 0)
    def _():
        acc_ref[...] = jnp.zeros(acc_ref.shape, F32)

    n_half = tt // MXU_COLS
    tcols = lambda th: slice(th * MXU_COLS, (th + 1) * MXU_COLS)

    def act_piece(th, m):
        ms = slice(m * ACT_ROWS, (m + 1) * ACT_ROWS)
        act_ref[ms, tcols(th)] = jnp.dot(u_ref[ms, :], h2t_ref[:, tcols(th)], preferred_element_type=F32)

    def out_piece(th, r):
        rs = slice(r * OUT_ROWS, (r + 1) * OUT_ROWS)
        acc_ref[rs, tcols(th)] += jnp.dot(vt_ref[rs, :], p_ref[:, tcols(th)], preferred_element_type=F32)

    def gate_chunk(tl, bs):
        ls = slice(tl * LANES, (tl + 1) * LANES)
        for j0 in range(0, nj, GATE_GROUP):
            g = [None] * GATE_GROUP
            for h in range(HEADS):
                rs = slice(h * N_KEYS + bs * PEER_ROWS, h * N_KEYS + (bs + 1) * PEER_ROWS)
                r2c = r2_ref[tl, rs, :]
                e2c = e2_ref[tl, rs, :]
                for jj in range(GATE_GROUP):
                    idx = h * N_KEYS + e * nj + j0 + jj
                    arow = _gate_rows(a_ref[tl, pl.ds(idx, 1), :], PEER_ROWS)
                    crow = _gate_rows(c_ref[tl, pl.ds(idx, 1), :], PEER_ROWS)
                    w = jnp.where(r2c < arow, e2c, jnp.zeros_like(e2c)) * crow
                    g[jj] = w if g[jj] is None else g[jj] + w
            for jj in range(GATE_GROUP):
                es = slice((j0 + jj) * N_KEYS + bs * PEER_ROWS, (j0 + jj) * N_KEYS + (bs + 1) * PEER_ROWS)
                aj = act_ref[es, ls].astype(BF16)
                gelu = 0.5 * aj * (1.0 + lax.erf(aj * (2.0 ** -0.5)))
                p_ref[es, ls] = g[jj] * gelu

    n_act = eb // ACT_ROWS
    n_out = acc_ref.shape[0] // OUT_ROWS
    for m in range(n_act):
        act_piece(0, m)
    for th in range(n_half):
        mxu_work = [functools.partial(act_piece, th + 1, m) for m in range(n_act)] if th + 1 < n_half else []
        if th > 0:
            mxu_work += [functools.partial(out_piece, th - 1, r) for r in range(n_out)]
        chunks = [(th * (MXU_COLS // LANES) + lt, bs)
                  for lt in range(MXU_COLS // LANES) for bs in range(N_KEYS // PEER_ROWS)]
        per_chunk = -(-len(mxu_work) // len(chunks))
        for tl, bs in chunks:
            gate_chunk(tl, bs)
            for piece in mxu_work[:per_chunk]:
                piece()
            mxu_work = mxu_work[per_chunk:]
    for r in range(n_out):
        out_piece(n_half - 1, r)

    @pl.when(e == pl.num_programs(1) - 1)
    def _():
        x2 = x1_ref[...] + gt_ref[0] * acc_ref[...].T
        o_ref[...] = _rms(x2) * fw_ref[...] if final else x2


def _peer_main(h2t, u, vt, sel, x1, mod3, fw, final, B, tps, tt, eb):
    D, T = h2t.shape
    ne = u.shape[0]
    wide = pl.BlockSpec((tt // LANES, HEADS * N_KEYS, LANES), lambda i, e: (i, 0, 0))
    return pl.pallas_call(
        functools.partial(_peer_main_kernel, final),
        grid=(T // tt, ne // eb),
        in_specs=[pl.BlockSpec((D, tt), lambda i, e: (0, i)),
                  pl.BlockSpec((eb, D), lambda i, e: (e, 0)),
                  pl.BlockSpec((D, eb), lambda i, e: (0, e)),
                  wide, wide, wide, wide,
                  pl.BlockSpec((tt, D), lambda i, e: (i, 0)),
                  pl.BlockSpec((1, 1, D), lambda i, e: (5 * B + i // tps, 0, 0)),
                  pl.BlockSpec((1, D), lambda i, e: (0, 0))],
        out_specs=pl.BlockSpec((tt, D), lambda i, e: (i, 0)),
        out_shape=jax.ShapeDtypeStruct((T, D), F32),
        scratch_shapes=[pltpu.VMEM((D, tt), F32),
                        pltpu.VMEM((eb, tt), F32),
                        pltpu.VMEM((eb, tt), BF16)],
        compiler_params=_cparams(("parallel", "arbitrary")),
        name="peer_main",
    )(h2t, u, vt, *sel, x1, mod3, fw.reshape(1, D))


def _rope_tables(S):
    inv = ROPE_THETA ** (-jnp.arange(ROPE_HALF, dtype=F32) * 2.0 / ROPE_DIM)
    ang = jnp.arange(S, dtype=jnp.int32).astype(F32)[:, None] * inv[None, :]
    cos, sin = jnp.cos(ang), jnp.sin(ang)
    zeros = jnp.zeros((S, DA_QK - ROPE_DIM), F32)
    z8 = jnp.zeros((S, ROPE_HALF), F32)
    cosf = jnp.concatenate([cos, cos, zeros + 1.0] * 2, axis=1)
    sa = jnp.concatenate([-sin, z8, zeros] * 2, axis=1)
    sb = jnp.concatenate([z8, sin, zeros] * 2, axis=1)
    return cosf, sa, sb


def kernel(x, c, ada_w, ada_b, norm1_w, norm2_w, w_in, conv_w, conv_b, ml_i_bias, ml_f_bias, ml_norm_w, lam_q1, lam_k1, lam_q2, lam_k2, subln_w, w_out, peer_wq, peer_keys, peer_u, peer_v, final_norm_w):
    B, S, D = x.shape
    assert D == D_MODEL and S % PROJ_TM == 0 and S % ATTN_TQ == 0
    T = B * S
    depth = w_in.shape[0]
    tm = 512
    tps = S // tm
    tables = _rope_tables(S)
    x2 = x.reshape(T, D)
    for l in range(depth):
        mod3 = _mod(c, ada_w[l], ada_b[l])
        h = _norm(x2, norm1_w[l], mod3, B, tps, tm)

        wl = w_in[l]
        o = 0
        cols = {}
        for name, width in (("qa", D), ("ka", D), ("va", D), ("qm", D // 2), ("km", D // 2), ("vm", D),
                            ("om", D), ("ip", HEADS), ("fp", HEADS), ("ga", D), ("gm", D)):
            cols[name] = wl[:, o:o + width]
            o += width
        w_rope = jnp.concatenate([cols["qa"], cols["ka"]], axis=1).astype(BF16)
        w_conv = jnp.concatenate([cols["qm"], cols["km"]], axis=1).astype(BF16)
        w_plain = jnp.concatenate([cols[n] for n in ("va", "vm", "om", "ga", "gm")], axis=1).astype(BF16)

        qk = _rope_proj(h, w_rope, tables, S // PROJ_TM, PROJ_TM)
        plain = _proj(h, w_plain, PROJ_TM, D)
        qkc, gi, gf, git, gft = _conv_proj(h, w_conv, cols["ip"].astype(BF16), cols["fp"].astype(BF16),
                                           conv_w[l], conv_b[l], ml_i_bias[l], ml_f_bias[l], tps, tm)

        lam_init = 0.8 - 0.6 * math.exp(-0.3 * l)
        ya = _attention(qk, plain, (lam_q1[l], lam_k1[l], lam_q2[l], lam_k2[l]), subln_w[l], lam_init, B, S, ATTN_TQ, ATTN_TK)
        ym = _mlstm(qkc, plain, gi, gf, git, gft, ml_norm_w[l], B, S)
        x1, h2t = _outproj(ya, ym, x2, w_out[l].astype(BF16), norm2_w[l], mod3, B, tps, tm)

        wqt = peer_wq[l].T.astype(BF16)
        keys = peer_keys[l].reshape(2 * HEADS, N_KEYS, LANES).astype(BF16)
        sel = _peer_sel(h2t, wqt, keys, tm)
        x2 = _peer_main(h2t, peer_u[l].astype(BF16), peer_v[l].T.astype(BF16), sel, x1, mod3,
                        final_norm_w, l == depth - 1, B, tps, tm, 16 * N_KEYS)
    if depth == 0:
        raise ValueError("depth must be positive")
    return x2.reshape(B, S, D)
```

```python
import functools
import math

import jax
import jax.numpy as jnp
from jax import lax
from jax.experimental import pallas as pl
from jax.experimental.pallas import tpu as pltpu

F32 = jnp.float32
BF16 = jnp.bfloat16
HIGHEST = lax.Precision.HIGHEST

NORM_EPS = 1e-6
LOG2E = 1.4426950408889634
D_MODEL = 1024
HEADS = 8
HEAD_V = 128
DA_QK = 64
ROPE_DIM = 16
ROPE_HALF = 8
ROPE_THETA = 500000.0
ML_QK = 64
CONV_WIDTH = 4
ML_CHUNK = 128
PROJ_TM = 1024
ATTN_TQ = 512
ATTN_TK = 512
N_KEYS = 128
TOPK = 16
LANES = 128
SUBLANES = 8
VMEM_LIMIT = 56 * 1024 * 1024

_CAND = [(r1, r2) for r1 in range(TOPK) for r2 in range(TOPK) if (r1 + 1) * (r2 + 1) <= TOPK]
_CAND_ROWS = -(-len(_CAND) // SUBLANES) * SUBLANES


def _cparams(sem):
    return pltpu.CompilerParams(dimension_semantics=sem, vmem_limit_bytes=VMEM_LIMIT)


def _rms(x):
    return x * lax.rsqrt(jnp.mean(x * x, axis=-1, keepdims=True) + NORM_EPS)


def _sigmoid(x):
    return 1.0 / (1.0 + jnp.exp(-x))


def _log_sigmoid(x):
    return jnp.minimum(x, 0.0) - jnp.log(1.0 + jnp.exp(-jnp.abs(x)))


def _mod_kernel(c_ref, w_ref, b_ref, o_ref):
    c = c_ref[...]
    cond = c * _sigmoid(c)
    o_ref[0] = jnp.dot(cond, w_ref[...], precision=HIGHEST, preferred_element_type=F32) + b_ref[...]


def _mod(c, ada_w, ada_b):
    B, D = c.shape
    out = pl.pallas_call(
        _mod_kernel,
        grid=(6,),
        in_specs=[pl.BlockSpec((B, D), lambda j: (0, 0)),
                  pl.BlockSpec((D, D), lambda j: (0, j)),
                  pl.BlockSpec((1, D), lambda j: (0, j))],
        out_specs=pl.BlockSpec((1, B, D), lambda j: (j, 0, 0)),
        out_shape=jax.ShapeDtypeStruct((6, B, D), F32),
        compiler_params=_cparams(("parallel",)),
        name="mod",
    )(c, ada_w, ada_b.reshape(1, 6 * D))
    return out.reshape(6 * B, 1, D)


def _norm_kernel(x_ref, w_ref, sc_ref, sh_ref, o_ref):
    y = _rms(x_ref[...]) * w_ref[...]
    o_ref[...] = (y * (1.0 + sc_ref[0]) + sh_ref[0]).astype(o_ref.dtype)


def _norm(x2, w, mod3, B, tps, tm):
    T, D = x2.shape
    return pl.pallas_call(
        _norm_kernel,
        grid=(T // tm,),
        in_specs=[pl.BlockSpec((tm, D), lambda i: (i, 0)),
                  pl.BlockSpec((1, D), lambda i: (0, 0)),
                  pl.BlockSpec((1, 1, D), lambda i: (1 * B + i // tps, 0, 0)),
                  pl.BlockSpec((1, 1, D), lambda i: (0 * B + i // tps, 0, 0))],
        out_specs=pl.BlockSpec((tm, D), lambda i: (i, 0)),
        out_shape=jax.ShapeDtypeStruct((T, D), BF16),
        compiler_params=_cparams(("parallel",)),
        name="norm1",
    )(x2, w.reshape(1, D), mod3, mod3)


def _proj_kernel(h_ref, w_ref, o_ref):
    o_ref[...] = jnp.dot(h_ref[...], w_ref[...], preferred_element_type=F32).astype(o_ref.dtype)


def _proj(h, w, tm, tn):
    T, D = h.shape
    N = w.shape[1]
    return pl.pallas_call(
        _proj_kernel,
        grid=(N // tn, T // tm),
        in_specs=[pl.BlockSpec((tm, D), lambda j, i: (i, 0)),
                  pl.BlockSpec((D, tn), lambda j, i: (0, j))],
        out_specs=pl.BlockSpec((tm, tn), lambda j, i: (i, j)),
        out_shape=jax.ShapeDtypeStruct((T, N), BF16),
        compiler_params=_cparams(("parallel", "parallel")),
        name="proj_plain",
    )(h, w)


def _rope_proj_kernel(h_ref, w_ref, cos_ref, sa_ref, sb_ref, o_ref):
    acc = jnp.dot(h_ref[...], w_ref[...], preferred_element_type=F32)
    scale = jnp.where(pl.program_id(0) == 0, DA_QK ** -0.5 * LOG2E, 1.0).astype(F32)
    cosf, sa, sb = cos_ref[...], sa_ref[...], sb_ref[...]
    for hh in range(HEADS):
        blk = acc[:, hh * LANES:(hh + 1) * LANES]
        rot = (blk * cosf + pltpu.roll(blk, LANES - ROPE_HALF, 1) * sa
               + pltpu.roll(blk, ROPE_HALF, 1) * sb)
        o_ref[:, hh * LANES:(hh + 1) * LANES] = (rot * scale).astype(o_ref.dtype)


def _rope_proj(h, w, tables, tps, tm):
    T, D = h.shape
    cosf, sa, sb = tables
    tab_spec = pl.BlockSpec((tm, LANES), lambda j, i: (i % tps, 0))
    return pl.pallas_call(
        _rope_proj_kernel,
        grid=(2, T // tm),
        in_specs=[pl.BlockSpec((tm, D), lambda j, i: (i, 0)),
                  pl.BlockSpec((D, D), lambda j, i: (0, j)),
                  tab_spec, tab_spec, tab_spec],
        out_specs=pl.BlockSpec((tm, D), lambda j, i: (i, j)),
        out_shape=jax.ShapeDtypeStruct((T, 2 * D), BF16),
        compiler_params=_cparams(("parallel", "parallel")),
        name="proj_rope",
    )(h, w, cosf, sa, sb)


def _conv_proj_kernel(tps, h_ref, w_ref, wgt_ref, cw_ref, cb_ref, bi_ref, bf_ref, bit_ref, bft_ref,
                      o_ref, gi_ref, gf_ref, git_ref, gft_ref, buf):
    i = pl.program_id(0)
    tm = h_ref.shape[0]
    n = o_ref.shape[1]
    h = h_ref[...]
    full = jnp.dot(h, w_ref[...], preferred_element_type=F32)
    acc = full[:, :n]

    @pl.when(i % tps == 0)
    def _():
        buf[0:SUBLANES, :] = jnp.zeros((SUBLANES, n), F32)

    buf[SUBLANES:SUBLANES + tm, :] = acc
    cw = cw_ref[...]
    y = cb_ref[...]
    for j in range(CONV_WIDTH):
        off = SUBLANES - (CONV_WIDTH - 1) + j
        y = y + buf[off:off + tm, :] * cw[j:j + 1, :]
    buf[0:SUBLANES, :] = buf[tm:tm + SUBLANES, :]
    y = y * _sigmoid(y)
    lane = lax.broadcasted_iota(jnp.int32, (1, n), 1)
    kscale = jnp.where(lane >= HEADS * ML_QK, ML_QK ** -0.5, 1.0).astype(F32)
    o_ref[...] = (y * kscale).astype(o_ref.dtype)

    gates = full[:, n:n + LANES]
    gi_ref[...] = gates[:, :HEADS] + bi_ref[...]
    gf_ref[...] = gates[:, HEADS:2 * HEADS] + bf_ref[...]
    gates_t = lax.dot_general(wgt_ref[...], h, (((1,), (1,)), ((), ())), preferred_element_type=F32)
    git_ref[...] = gates_t[:HEADS] + bit_ref[...]
    gft_ref[...] = gates_t[HEADS:] + bft_ref[...]


def _conv_proj(h, w, wgi, wgf, conv_w, conv_b, bi, bf, tps, tm):
    T, D = h.shape
    N = w.shape[1]
    wg = jnp.concatenate([wgi, wgf], axis=1)
    w_ext = jnp.concatenate([w, jnp.pad(wg, ((0, 0), (0, LANES - 2 * HEADS)))], axis=1)
    full = lambda shape: pl.BlockSpec(shape, lambda i: tuple(0 for _ in shape))
    return pl.pallas_call(
        functools.partial(_conv_proj_kernel, tps),
        grid=(T // tm,),
        in_specs=[pl.BlockSpec((tm, D), lambda i: (i, 0)),
                  full((D, N + LANES)), full((2 * HEADS, D)),
                  full((CONV_WIDTH, N)), full((1, N)),
                  full((1, HEADS)), full((1, HEADS)), full((HEADS, 1)), full((HEADS, 1))],
        out_specs=[pl.BlockSpec((tm, N), lambda i: (i, 0)),
                   pl.BlockSpec((tm, HEADS), lambda i: (i, 0)),
                   pl.BlockSpec((tm, HEADS), lambda i: (i, 0)),
                   pl.BlockSpec((HEADS, tm), lambda i: (0, i)),
                   pl.BlockSpec((HEADS, tm), lambda i: (0, i))],
        out_shape=[jax.ShapeDtypeStruct((T, N), BF16),
                   jax.ShapeDtypeStruct((T, HEADS), F32),
                   jax.ShapeDtypeStruct((T, HEADS), F32),
                   jax.ShapeDtypeStruct((HEADS, T), F32),
                   jax.ShapeDtypeStruct((HEADS, T), F32)],
        scratch_shapes=[pltpu.VMEM((tm + 2 * SUBLANES, N), F32)],
        compiler_params=_cparams(("arbitrary",)),
        name="proj_conv",
    )(h, w_ext, wg.T, conv_w, conv_b.reshape(1, N),
      bi.reshape(1, HEADS), bf.reshape(1, HEADS), bi.reshape(HEADS, 1), bf.reshape(HEADS, 1))


def _attn_kernel(lam_init, q_ref, k_ref, v_ref, ga_ref, lq1_ref, lk1_ref, lq2_ref, lk2_ref, sw_ref,
                 o_ref, qt_s, vt_s, sa_s, sb_s, m_s, l_s, acc_s):
    qi = pl.program_id(2)
    tq = q_ref.shape[0]
    tk = vt_s.shape[2]
    ratio = tq // tk

    @pl.when(qi == 0)
    def _():
        for kk in range(vt_s.shape[0]):
            vt_s[kk] = v_ref[kk * tk:(kk + 1) * tk, :].T

    q = q_ref[...].astype(F32)
    lane = lax.broadcasted_iota(jnp.int32, q.shape, 1)
    qt_s[0] = jnp.where(lane < DA_QK, q, 0.0).T.astype(BF16)
    qt_s[1] = jnp.where(lane >= DA_QK, q, 0.0).T.astype(BF16)
    m_s[...] = jnp.full(m_s.shape, -jnp.inf, F32)
    l_s[...] = jnp.zeros(l_s.shape, F32)
    acc_s[...] = jnp.zeros(acc_s.shape, F32)

    def scores(kk, st_ref):
        k = k_ref[pl.ds(pl.multiple_of(kk * tk, tk), tk), :]
        for c in range(2):
            st_ref[c] = jnp.dot(k, qt_s[c], preferred_element_type=F32)

    def process(kk, st_ref, diag_offset):
        vt = vt_s[kk]
        for c in range(2):
            st = st_ref[c]
            if diag_offset is not None:
                key = lax.broadcasted_iota(jnp.int32, st.shape, 0) + diag_offset
                qry = lax.broadcasted_iota(jnp.int32, st.shape, 1)
                st = jnp.where(key <= qry, st, -jnp.inf)
            m_prev = m_s[c]
            m_new = jnp.maximum(m_prev, jnp.max(st, axis=0, keepdims=True))
            alpha = jnp.exp2(m_prev - m_new)
            p = jnp.exp2(st - m_new)
            l_s[c] = alpha * l_s[c] + jnp.sum(p, axis=0, keepdims=True)
            acc_s[c] = alpha * acc_s[c] + jnp.dot(vt, p.astype(BF16), preferred_element_type=F32)
            m_s[c] = m_new

    def diagonal(first, cur, nxt):
        for d in range(ratio):
            if d + 1 < ratio:
                scores(first + d + 1, nxt)
            process(first + d, cur, d * tk)
            cur, nxt = nxt, cur

    n_below = qi * ratio
    scores(0, sa_s)

    def pair(i, carry):
        kk = 2 * i
        scores(kk + 1, sb_s)
        process(kk, sa_s, None)
        scores(kk + 2, sa_s)
        process(kk + 1, sb_s, None)
        return carry

    lax.fori_loop(0, n_below // 2, pair, 0)
    odd = lax.rem(n_below, 2) == 1

    @pl.when(odd)
    def _():
        scores(n_below, sb_s)
        process(n_below - 1, sa_s, None)
        diagonal(n_below, sb_s, sa_s)

    @pl.when(jnp.logical_not(odd))
    def _():
        diagonal(n_below, sa_s, sb_s)

    lam = (jnp.exp(jnp.sum(lq1_ref[...] * lk1_ref[...], axis=-1, keepdims=True))
           - jnp.exp(jnp.sum(lq2_ref[...] * lk2_ref[...], axis=-1, keepdims=True)) + lam_init)
    o = (acc_s[0] / l_s[0] - lam * (acc_s[1] / l_s[1])).T
    o = _rms(o) * sw_ref[...] * (1.0 - lam_init)
    o_ref[...] = (o * _sigmoid(ga_ref[...].astype(F32))).astype(o_ref.dtype)


def _attention(qk, plain, lam_vecs, subln_w, lam_init, B, S, tq, tk):
    T = qk.shape[0]
    nq = S // tq
    vec = pl.BlockSpec((1, DA_QK), lambda b, h, i: (0, 0))
    return pl.pallas_call(
        functools.partial(_attn_kernel, lam_init),
        grid=(B, HEADS, nq),
        in_specs=[pl.BlockSpec((tq, LANES), lambda b, h, i: (b * nq + i, h)),
                  pl.BlockSpec((S, LANES), lambda b, h, i: (b, HEADS + h)),
                  pl.BlockSpec((S, LANES), lambda b, h, i: (b, h)),
                  pl.BlockSpec((tq, LANES), lambda b, h, i: (b * nq + i, 3 * HEADS + h)),
                  vec, vec, vec, vec,
                  pl.BlockSpec((1, HEAD_V), lambda b, h, i: (0, 0))],
        out_specs=pl.BlockSpec((tq, LANES), lambda b, h, i: (b * nq + i, h)),
        out_shape=jax.ShapeDtypeStruct((T, D_MODEL), BF16),
        scratch_shapes=[pltpu.VMEM((2, LANES, tq), BF16),
                        pltpu.VMEM((S // tk, HEAD_V, tk), BF16),
                        pltpu.VMEM((2, tk, tq), F32),
                        pltpu.VMEM((2, tk, tq), F32),
                        pltpu.VMEM((2, 1, tq), F32),
                        pltpu.VMEM((2, 1, tq), F32),
                        pltpu.VMEM((2, HEAD_V, tq), F32)],
        compiler_params=_cparams(("parallel", "parallel", "arbitrary")),
        name="attn",
    )(qk, qk, plain, plain, *[v.reshape(1, DA_QK) for v in lam_vecs], subln_w.reshape(1, HEAD_V))


def _mlstm_kernel(qk_ref, v_ref, om_ref, gm_ref, gi_ref, gf_ref, git_ref, gft_ref, nw_ref,
                  o_ref, c_s, n_s, m_s):
    L = qk_ref.shape[0]

    @pl.when(pl.program_id(1) == 0)
    def _():
        c_s[...] = jnp.zeros(c_s.shape, F32)
        n_s[...] = jnp.zeros(n_s.shape, F32)
        m_s[...] = jnp.zeros(m_s.shape, F32)

    row = lax.broadcasted_iota(jnp.int32, (L, L), 0)
    col = lax.broadcasted_iota(jnp.int32, (L, L), 1)
    tri = (col <= row).astype(F32)
    causal_t = row <= col
    tri_t = causal_t.astype(F32)
    bcols = jnp.dot(tri, _log_sigmoid(gf_ref[...]), precision=HIGHEST, preferred_element_type=F32)
    brows = jnp.dot(_log_sigmoid(gft_ref[...]), tri_t, precision=HIGHEST, preferred_element_type=F32)
    ucols = gi_ref[...] - bcols
    git = git_ref[...]
    lane = lax.broadcasted_iota(jnp.int32, (1, LANES), 1)
    first_row = lax.broadcasted_iota(jnp.int32, (SUBLANES, 1), 0) == 0
    nt = (((1,), (1,)), ((), ()))

    for h in range(HEADS):
        p = h // 2
        hmask = ((lane >= (h % 2) * ML_QK) & (lane < (h % 2 + 1) * ML_QK)).astype(F32)
        qh = (qk_ref[:, p * LANES:(p + 1) * LANES].astype(F32) * hmask).astype(BF16)
        kp = qk_ref[:, (HEADS // 2 + p) * LANES:(HEADS // 2 + p + 1) * LANES]
        vt = v_ref[:, h * HEAD_V:(h + 1) * HEAD_V].T
        br = brows[h:h + 1, :]
        igr = git[h:h + 1, :]
        g_tot = br[:, L - 1:L]
        m_prev = m_s[h]
        ct_prev = c_s[h]
        n_prev = n_s[h]

        dm = jnp.where(causal_t, br + ucols[:, h:h + 1], -jnp.inf)
        m_inter = br + m_prev
        m_j = jnp.maximum(jnp.max(dm, axis=0, keepdims=True), m_inter)
        st = lax.dot_general(kp, qh, nt, preferred_element_type=F32)
        qkw = st * jnp.exp(dm - m_j)
        inter_w = jnp.exp(m_inter - m_j)
        num = (jnp.dot(vt, qkw.astype(BF16), preferred_element_type=F32)
               + inter_w * lax.dot_general(ct_prev.astype(BF16), qh, nt, preferred_element_type=F32))
        qn = lax.dot_general(n_prev.astype(BF16), qh, nt, preferred_element_type=F32)[0:1, :]
        den = jnp.sum(qkw, axis=0, keepdims=True) + inter_w * qn
        ht = num / jnp.maximum(jnp.abs(den), jnp.exp(-m_j))

        a = g_tot - br + igr
        m_loc = jnp.max(a, axis=-1, keepdims=True)
        w_loc = jnp.exp(a - m_loc)
        m_new = jnp.maximum(g_tot + m_prev, m_loc)
        dec = jnp.exp(g_tot + m_prev - m_new)
        inc = jnp.exp(m_loc - m_new)
        c_loc = jnp.dot((vt.astype(F32) * w_loc).astype(BF16), kp, preferred_element_type=F32)
        w8 = jnp.where(first_row, w_loc, 0.0)
        n_loc = jnp.dot(w8, kp.astype(F32), precision=HIGHEST, preferred_element_type=F32)
        c_s[h] = dec * ct_prev + inc * c_loc
        n_s[h] = dec * n_prev + inc * n_loc
        m_s[h] = m_new

        sl = slice(h * HEAD_V, (h + 1) * HEAD_V)
        yt = ht * lax.rsqrt(jnp.mean(ht * ht, axis=0, keepdims=True) + NORM_EPS)
        y = yt.T * nw_ref[:, sl]
        y = y * _sigmoid(om_ref[:, sl].astype(F32)) * _sigmoid(gm_ref[:, sl].astype(F32))
        o_ref[:, sl] = y.astype(o_ref.dtype)


def _mlstm(qkc, plain, gi, gf, git, gft, ml_norm_w, B, S):
    T, D = qkc.shape
    L = ML_CHUNK
    nc = S // L
    wide = lambda cb: pl.BlockSpec((L, D), lambda b, c: (b * nc + c, cb))
    return pl.pallas_call(
        _mlstm_kernel,
        grid=(B, nc),
        in_specs=[wide(0), wide(1), wide(2), wide(4),
                  pl.BlockSpec((L, HEADS), lambda b, c: (b * nc + c, 0)),
                  pl.BlockSpec((L, HEADS), lambda b, c: (b * nc + c, 0)),
                  pl.BlockSpec((HEADS, L), lambda b, c: (0, b * nc + c)),
                  pl.BlockSpec((HEADS, L), lambda b, c: (0, b * nc + c)),
                  pl.BlockSpec((1, D), lambda b, c: (0, 0))],
        out_specs=pl.BlockSpec((L, D), lambda b, c: (b * nc + c, 0)),
        out_shape=jax.ShapeDtypeStruct((T, D), BF16),
        scratch_shapes=[pltpu.VMEM((HEADS, LANES, HEAD_V), F32),
                        pltpu.VMEM((HEADS, SUBLANES, LANES), F32),
                        pltpu.VMEM((HEADS, 1, 1), F32)],
        compiler_params=_cparams(("parallel", "arbitrary")),
        name="mlstm",
    )(qkc, plain, plain, plain, gi, gf, git, gft, ml_norm_w.reshape(1, D))


def _outproj_kernel(ya_ref, ym_ref, x_ref, w_ref, gt_ref, nw_ref, sc_ref, sh_ref, x1_ref, h2t_ref):
    merged = (ya_ref[...].astype(F32) + ym_ref[...].astype(F32)).astype(BF16)
    x1 = x_ref[...] + gt_ref[0] * jnp.dot(merged, w_ref[...], preferred_element_type=F32)
    x1_ref[...] = x1
    h2 = _rms(x1) * nw_ref[...] * (1.0 + sc_ref[0]) + sh_ref[0]
    h2t_ref[...] = h2.T.astype(h2t_ref.dtype)


def _outproj(ya, ym, x2, w_out, norm2_w, mod3, B, tps, tm):
    T, D = x2.shape
    row = lambda k: pl.BlockSpec((1, 1, D), lambda i: (k * B + i // tps, 0, 0))
    tile = pl.BlockSpec((tm, D), lambda i: (i, 0))
    return pl.pallas_call(
        _outproj_kernel,
        grid=(T // tm,),
        in_specs=[tile, tile, tile, pl.BlockSpec((D, D), lambda i: (0, 0)), row(2),
                  pl.BlockSpec((1, D), lambda i: (0, 0)), row(4), row(3)],
        out_specs=[tile, pl.BlockSpec((D, tm), lambda i: (0, i))],
        out_shape=[jax.ShapeDtypeStruct((T, D), F32), jax.ShapeDtypeStruct((D, T), BF16)],
        compiler_params=_cparams(("parallel",)),
        name="outproj",
    )(ya, ym, x2, w_out, mod3, norm2_w.reshape(1, D), mod3, mod3)


def _topk_ranks(s):
    n = s.shape[0]
    it = lax.broadcasted_iota(jnp.int32, s.shape, 0)
    rank = jnp.full(s.shape, float(TOPK), F32)
    x = s
    vals = []
    for r in range(TOPK):
        m = jnp.max(x, axis=0, keepdims=True)
        first = jnp.min(jnp.where(x == m, it, n), axis=0, keepdims=True)
        hit = it == first
        x = jnp.where(hit, -jnp.inf, x)
        rank = jnp.where(hit, float(r), rank)
        vals.append(m)
    return rank, vals


MARK_BASE = 3.0e38
MARK_STEP = 1.0e36


def _topk_ranks_distinct(s):
    x = s
    vals = []
    for r in range(TOPK):
        m = jnp.max(x, axis=0, keepdims=True)
        x = jnp.where(x == m, -(MARK_BASE + r * MARK_STEP), x)
        vals.append(m)
    taken = x <= -MARK_BASE
    rank = jnp.where(taken, jnp.round((-x - MARK_BASE) * (1.0 / MARK_STEP)), float(TOPK))
    removed = jnp.sum(jnp.where(taken, 1.0, 0.0), axis=0, keepdims=True)
    return rank, vals, removed == float(TOPK)


def _cand_select(cand, top):
    it = lax.broadcasted_iota(jnp.int32, cand.shape, 0)
    sel = jnp.zeros(cand.shape, F32)
    z = jnp.zeros(top.shape, F32)
    x = cand
    for _ in range(TOPK):
        m = jnp.max(x, axis=0, keepdims=True)
        first = jnp.min(jnp.where(x == m, it, cand.shape[0]), axis=0, keepdims=True)
        hit = it == first
        x = jnp.where(hit, -jnp.inf, x)
        sel = jnp.where(hit, 1.0, sel)
        z = z + jnp.exp(m - top)
    return sel, z


def _cand_select_distinct(cand, top):
    z = jnp.zeros(top.shape, F32)
    x = cand
    for _ in range(TOPK):
        m = jnp.max(x, axis=0, keepdims=True)
        x = jnp.where(x == m, -MARK_BASE, x)
        z = z + jnp.exp(m - top)
    sel = jnp.where(x == -MARK_BASE, 1.0, 0.0)
    return sel, z, jnp.sum(sel, axis=0, keepdims=True) == float(TOPK)


def _selection(s1, s2, exact):
    if exact:
        (rank1, v1), (rank2, v2) = _topk_ranks(s1), _topk_ranks(s2)
    else:
        (rank1, v1, ok1), (rank2, v2, ok2) = _topk_ranks_distinct(s1), _topk_ranks_distinct(s2)
    tt = s1.shape[1]
    rows = [v1[r1] + v2[r2] for (r1, r2) in _CAND]
    rows += [jnp.full((1, tt), -jnp.inf, F32)] * (_CAND_ROWS - len(_CAND))
    cand = jnp.concatenate(rows, axis=0)
    top = v1[0] + v2[0]
    if exact:
        sel, z = _cand_select(cand, top)
        safe = None
    else:
        sel, z, ok3 = _cand_select_distinct(cand, top)
        safe = ok1 & ok2 & ok3
    a = jnp.zeros(s1.shape, F32)
    start = 0
    for r1 in range(TOPK):
        width = sum(1 for c in _CAND if c[0] == r1)
        cnt = jnp.sum(sel[start:start + width, :], axis=0, keepdims=True)
        a = jnp.where(rank1 == float(r1), cnt, a)
        start += width
    return (rank2, jnp.exp(s2 - v2[0]), a, jnp.exp(s1 - v1[0]) / z), safe


SEL_HEADS = 2


def _peer_sel_kernel(h2t_ref, wqt_ref, keys_ref, r2_ref, e2_ref, a_ref, c_ref):
    ht = h2t_ref[...]
    scores = []
    for hp in range(2 * SEL_HEADS):
        qt = jnp.dot(wqt_ref[hp * LANES:(hp + 1) * LANES, :], ht, preferred_element_type=F32)
        scores.append(jnp.dot(keys_ref[hp], qt.astype(BF16), preferred_element_type=F32))

    def write(hh, vals):
        for ref, val in zip((r2_ref, e2_ref, a_ref, c_ref), vals):
            for lt in range(val.shape[1] // LANES):
                ref[lt, hh * N_KEYS:(hh + 1) * N_KEYS, :] = val[:, lt * LANES:(lt + 1) * LANES].astype(ref.dtype)

    unsafe = 0.0
    for hh in range(SEL_HEADS):
        vals, safe = _selection(scores[2 * hh], scores[2 * hh + 1], exact=False)
        write(hh, vals)
        unsafe = unsafe + jnp.sum(jnp.where(safe, 0.0, 1.0))

    @pl.when(unsafe > 0.0)
    def _():
        for hh in range(SEL_HEADS):
            write(hh, _selection(scores[2 * hh], scores[2 * hh + 1], exact=True)[0])


def _peer_sel(h2t, wqt, keys, tt):
    D, T = h2t.shape
    out = pl.BlockSpec((tt // LANES, SEL_HEADS * N_KEYS, LANES), lambda i, h: (i, h, 0))
    shp = lambda dt: jax.ShapeDtypeStruct((T // LANES, HEADS * N_KEYS, LANES), dt)
    return pl.pallas_call(
        _peer_sel_kernel,
        grid=(T // tt, HEADS // SEL_HEADS),
        in_specs=[pl.BlockSpec((D, tt), lambda i, h: (0, i)),
                  pl.BlockSpec((SEL_HEADS * 2 * LANES, D), lambda i, h: (h, 0)),
                  pl.BlockSpec((SEL_HEADS * 2, N_KEYS, LANES), lambda i, h: (h, 0, 0))],
        out_specs=[out, out, out, out],
        out_shape=[shp(BF16), shp(BF16), shp(F32), shp(F32)],
        compiler_params=_cparams(("parallel", "parallel")),
        name="peer_sel",
    )(h2t, wqt, keys)


PEER_ROWS = 64
GATE_GROUP = 8
MXU_COLS = 256
ACT_ROWS = 256
OUT_ROWS = 512


def _gate_rows(row, rows):
    packed = jnp.broadcast_to(row, (2 * SUBLANES, LANES)).astype(BF16)
    return jnp.tile(packed, (rows // (2 * SUBLANES), 1))


def _peer_main_kernel(final, h2t_ref, u_ref, vt_ref, r2_ref, e2_ref, a_ref, c_ref, x1_ref, gt_ref, fw_ref,
                      o_ref, acc_ref, act_ref, p_ref):
    e = pl.program_id(1)
    eb, tt = act_ref.shape
    nj = eb // N_KEYS

    @pl.when(e == 0)
    def _():
        acc_ref[...] = jnp.zeros(acc_ref.shape, F32)

    n_half = tt // MXU_COLS
    tcols = lambda th: slice(th * MXU_COLS, (th + 1) * MXU_COLS)

    def act_piece(th, m):
        ms = slice(m * ACT_ROWS, (m + 1) * ACT_ROWS)
        act_ref[ms, tcols(th)] = jnp.dot(u_ref[ms, :], h2t_ref[:, tcols(th)], preferred_element_type=F32)

    def out_piece(th, r):
        rs = slice(r * OUT_ROWS, (r + 1) * OUT_ROWS)
        acc_ref[rs, tcols(th)] += jnp.dot(vt_ref[rs, :], p_ref[:, tcols(th)], preferred_element_type=F32)

    def gate_chunk(tl, bs):
        ls = slice(tl * LANES, (tl + 1) * LANES)
        for j0 in range(0, nj, GATE_GROUP):
            g = [None] * GATE_GROUP
            for h in range(HEADS):
                rs = slice(h * N_KEYS + bs * PEER_ROWS, h * N_KEYS + (bs + 1) * PEER_ROWS)
                r2c = r2_ref[tl, rs, :]
                e2c = e2_ref[tl, rs, :]
                for jj in range(GATE_GROUP):
                    idx = h * N_KEYS + e * nj + j0 + jj
                    arow = _gate_rows(a_ref[tl, pl.ds(idx, 1), :], PEER_ROWS)
                    crow = _gate_rows(c_ref[tl, pl.ds(idx, 1), :], PEER_ROWS)
                    w = jnp.where(r2c < arow, e2c, jnp.zeros_like(e2c)) * crow
                    g[jj] = w if g[jj] is None else g[jj] + w
            for jj in range(GATE_GROUP):
                es = slice((j0 + jj) * N_KEYS + bs * PEER_ROWS, (j0 + jj) * N_KEYS + (bs + 1) * PEER_ROWS)
                aj = act_ref[es, ls].astype(BF16)
                gelu = 0.5 * aj * (1.0 + lax.erf(aj * (2.0 ** -0.5)))
                p_ref[es, ls] = g[jj] * gelu

    n_act = eb // ACT_ROWS
    n_out = acc_ref.shape[0] // OUT_ROWS
    for m in range(n_act):
        act_piece(0, m)
    for th in range(n_half):
        mxu_work = [functools.partial(act_piece, th + 1, m) for m in range(n_act)] if th + 1 < n_half else []
        if th > 0:
            mxu_work += [functools.partial(out_piece, th - 1, r) for r in range(n_out)]
        chunks = [(th * (MXU_COLS // LANES) + lt, bs)
                  for lt in range(MXU_COLS // LANES) for bs in range(N_KEYS // PEER_ROWS)]
        per_chunk = -(-len(mxu_work) // len(chunks))
        for tl, bs in chunks:
            gate_chunk(tl, bs)
            for piece in mxu_work[:per_chunk]:
                piece()
            mxu_work = mxu_work[per_chunk:]
    for r in range(n_out):
        out_piece(n_half - 1, r)

    @pl.when(e == pl.num_programs(1) - 1)
    def _():
        x2 = x1_ref[...] + gt_ref[0] * acc_ref[...].T
        o_ref[...] = _rms(x2) * fw_ref[...] if final else x2


def _peer_main(h2t, u, vt, sel, x1, mod3, fw, final, B, tps, tt, eb):
    D, T = h2t.shape
    ne = u.shape[0]
    wide = pl.BlockSpec((tt // LANES, HEADS * N_KEYS, LANES), lambda i, e: (i, 0, 0))
    return pl.pallas_call(
        functools.partial(_peer_main_kernel, final),
        grid=(T // tt, ne // eb),
        in_specs=[pl.BlockSpec((D, tt), lambda i, e: (0, i)),
                  pl.BlockSpec((eb, D), lambda i, e: (e, 0)),
                  pl.BlockSpec((D, eb), lambda i, e: (0, e)),
                  wide, wide, wide, wide,
                  pl.BlockSpec((tt, D), lambda i, e: (i, 0)),
                  pl.BlockSpec((1, 1, D), lambda i, e: (5 * B + i // tps, 0, 0)),
                  pl.BlockSpec((1, D), lambda i, e: (0, 0))],
        out_specs=pl.BlockSpec((tt, D), lambda i, e: (i, 0)),
        out_shape=jax.ShapeDtypeStruct((T, D), F32),
        scratch_shapes=[pltpu.VMEM((D, tt), F32),
                        pltpu.VMEM((eb, tt), F32),
                        pltpu.VMEM((eb, tt), BF16)],
        compiler_params=_cparams(("parallel", "arbitrary")),
        name="peer_main",
    )(h2t, u, vt, *sel, x1, mod3, fw.reshape(1, D))


def _rope_tables(S):
    inv = ROPE_THETA ** (-jnp.arange(ROPE_HALF, dtype=F32) * 2.0 / ROPE_DIM)
    ang = jnp.arange(S, dtype=jnp.int32).astype(F32)[:, None] * inv[None, :]
    cos, sin = jnp.cos(ang), jnp.sin(ang)
    zeros = jnp.zeros((S, DA_QK - ROPE_DIM), F32)
    z8 = jnp.zeros((S, ROPE_HALF), F32)
    cosf = jnp.concatenate([cos, cos, zeros + 1.0] * 2, axis=1)
    sa = jnp.concatenate([-sin, z8, zeros] * 2, axis=1)
    sb = jnp.concatenate([z8, sin, zeros] * 2, axis=1)
    return cosf, sa, sb


def kernel(x, c, ada_w, ada_b, norm1_w, norm2_w, w_in, conv_w, conv_b, ml_i_bias, ml_f_bias, ml_norm_w, lam_q1, lam_k1, lam_q2, lam_k2, subln_w, w_out, peer_wq, peer_keys, peer_u, peer_v, final_norm_w):
    B, S, D = x.shape
    assert D == D_MODEL and S % PROJ_TM == 0 and S % ATTN_TQ == 0
    T = B * S
    depth = w_in.shape[0]
    tm = 512
    tps = S // tm
    tables = _rope_tables(S)
    x2 = x.reshape(T, D)
    for l in range(depth):
        mod3 = _mod(c, ada_w[l], ada_b[l])
        h = _norm(x2, norm1_w[l], mod3, B, tps, tm)

        wl = w_in[l]
        o = 0
        cols = {}
        for name, width in (("qa", D), ("ka", D), ("va", D), ("qm", D // 2), ("km", D // 2), ("vm", D),
                            ("om", D), ("ip", HEADS), ("fp", HEADS), ("ga", D), ("gm", D)):
            cols[name] = wl[:, o:o + width]
            o += width
        w_rope = jnp.concatenate([cols["qa"], cols["ka"]], axis=1).astype(BF16)
        w_conv = jnp.concatenate([cols["qm"], cols["km"]], axis=1).astype(BF16)
        w_plain = jnp.concatenate([cols[n] for n in ("va", "vm", "om", "ga", "gm")], axis=1).astype(BF16)

        qk = _rope_proj(h, w_rope, tables, S // PROJ_TM, PROJ_TM)
        plain = _proj(h, w_plain, PROJ_TM, D)
        qkc, gi, gf, git, gft = _conv_proj(h, w_conv, cols["ip"].astype(BF16), cols["fp"].astype(BF16),
                                           conv_w[l], conv_b[l], ml_i_bias[l], ml_f_bias[l], tps, tm)

        lam_init = 0.8 - 0.6 * math.exp(-0.3 * l)
        ya = _attention(qk, plain, (lam_q1[l], lam_k1[l], lam_q2[l], lam_k2[l]), subln_w[l], lam_init, B, S, ATTN_TQ, ATTN_TK)
        ym = _mlstm(qkc, plain, gi, gf, git, gft, ml_norm_w[l], B, S)
        x1, h2t = _outproj(ya, ym, x2, w_out[l].astype(BF16), norm2_w[l], mod3, B, tps, tm)

        wqt = peer_wq[l].T.astype(BF16)
        keys = peer_keys[l].reshape(2 * HEADS, N_KEYS, LANES).astype(BF16)
        sel = _peer_sel(h2t, wqt, keys, tm)
        x2 = _peer_main(h2t, peer_u[l].astype(BF16), peer_v[l].T.astype(BF16), sel, x1, mod3,
                        final_norm_w, l == depth - 1, B, tps, tm, 16 * N_KEYS)
    if depth == 0:
        raise ValueError("depth must be positive")
    return x2.reshape(B, S, D)
```

```python
import functools
import math

import jax
import jax.numpy as jnp
from jax import lax
from jax.experimental import pallas as pl
from jax.experimental.pallas import tpu as pltpu

F32 = jnp.float32
BF16 = jnp.bfloat16
HIGHEST = lax.Precision.HIGHEST

NORM_EPS = 1e-6
LOG2E = 1.4426950408889634
D_MODEL = 1024
HEADS = 8
HEAD_V = 128
DA_QK = 64
ROPE_DIM = 16
ROPE_HALF = 8
ROPE_THETA = 500000.0
ML_QK = 64
CONV_WIDTH = 4
ML_CHUNK = 128
PROJ_TM = 1024
ATTN_TQ = 1024
ATTN_TK = 512
N_KEYS = 128
TOPK = 16
LANES = 128
SUBLANES = 8
VMEM_LIMIT = 56 * 1024 * 1024

_CAND = [(r1, r2) for r1 in range(TOPK) for r2 in range(TOPK) if (r1 + 1) * (r2 + 1) <= TOPK]
_CAND_ROWS = -(-len(_CAND) // SUBLANES) * SUBLANES


def _cparams(sem):
    return pltpu.CompilerParams(dimension_semantics=sem, vmem_limit_bytes=VMEM_LIMIT)


def _rms(x):
    return x * lax.rsqrt(jnp.mean(x * x, axis=-1, keepdims=True) + NORM_EPS)


def _sigmoid(x):
    return 1.0 / (1.0 + jnp.exp(-x))


def _log_sigmoid(x):
    return jnp.minimum(x, 0.0) - jnp.log(1.0 + jnp.exp(-jnp.abs(x)))


def _mod_kernel(c_ref, w_ref, b_ref, o_ref):
    c = c_ref[...]
    cond = c * _sigmoid(c)
    o_ref[0] = jnp.dot(cond, w_ref[...], precision=HIGHEST, preferred_element_type=F32) + b_ref[...]


def _mod(c, ada_w, ada_b):
    B, D = c.shape
    out = pl.pallas_call(
        _mod_kernel,
        grid=(6,),
        in_specs=[pl.BlockSpec((B, D), lambda j: (0, 0)),
                  pl.BlockSpec((D, D), lambda j: (0, j)),
                  pl.BlockSpec((1, D), lambda j: (0, j))],
        out_specs=pl.BlockSpec((1, B, D), lambda j: (j, 0, 0)),
        out_shape=jax.ShapeDtypeStruct((6, B, D), F32),
        compiler_params=_cparams(("parallel",)),
        name="mod",
    )(c, ada_w, ada_b.reshape(1, 6 * D))
    return out.reshape(6 * B, 1, D)


def _norm_kernel(x_ref, w_ref, sc_ref, sh_ref, o_ref):
    y = _rms(x_ref[...]) * w_ref[...]
    o_ref[...] = (y * (1.0 + sc_ref[0]) + sh_ref[0]).astype(o_ref.dtype)


def _norm(x2, w, mod3, B, tps, tm):
    T, D = x2.shape
    return pl.pallas_call(
        _norm_kernel,
        grid=(T // tm,),
        in_specs=[pl.BlockSpec((tm, D), lambda i: (i, 0)),
                  pl.BlockSpec((1, D), lambda i: (0, 0)),
                  pl.BlockSpec((1, 1, D), lambda i: (1 * B + i // tps, 0, 0)),
                  pl.BlockSpec((1, 1, D), lambda i: (0 * B + i // tps, 0, 0))],
        out_specs=pl.BlockSpec((tm, D), lambda i: (i, 0)),
        out_shape=jax.ShapeDtypeStruct((T, D), BF16),
        compiler_params=_cparams(("parallel",)),
        name="norm1",
    )(x2, w.reshape(1, D), mod3, mod3)


def _proj_kernel(h_ref, w_ref, o_ref):
    o_ref[...] = jnp.dot(h_ref[...], w_ref[...], preferred_element_type=F32).astype(o_ref.dtype)


def _proj(h, w, tm, tn):
    T, D = h.shape
    N = w.shape[1]
    return pl.pallas_call(
        _proj_kernel,
        grid=(N // tn, T // tm),
        in_specs=[pl.BlockSpec((tm, D), lambda j, i: (i, 0)),
                  pl.BlockSpec((D, tn), lambda j, i: (0, j))],
        out_specs=pl.BlockSpec((tm, tn), lambda j, i: (i, j)),
        out_shape=jax.ShapeDtypeStruct((T, N), BF16),
        compiler_params=_cparams(("parallel", "parallel")),
        name="proj_plain",
    )(h, w)


def _rope_proj_kernel(h_ref, w_ref, cos_ref, sa_ref, sb_ref, o_ref):
    acc = jnp.dot(h_ref[...], w_ref[...], preferred_element_type=F32)
    scale = jnp.where(pl.program_id(0) == 0, DA_QK ** -0.5 * LOG2E, 1.0).astype(F32)
    cosf, sa, sb = cos_ref[...], sa_ref[...], sb_ref[...]
    for hh in range(HEADS):
        blk = acc[:, hh * LANES:(hh + 1) * LANES]
        rot = (blk * cosf + pltpu.roll(blk, LANES - ROPE_HALF, 1) * sa
               + pltpu.roll(blk, ROPE_HALF, 1) * sb)
        o_ref[:, hh * LANES:(hh + 1) * LANES] = (rot * scale).astype(o_ref.dtype)


def _rope_proj(h, w, tables, tps, tm):
    T, D = h.shape
    cosf, sa, sb = tables
    tab_spec = pl.BlockSpec((tm, LANES), lambda j, i: (i % tps, 0))
    return pl.pallas_call(
        _rope_proj_kernel,
        grid=(2, T // tm),
        in_specs=[pl.BlockSpec((tm, D), lambda j, i: (i, 0)),
                  pl.BlockSpec((D, D), lambda j, i: (0, j)),
                  tab_spec, tab_spec, tab_spec],
        out_specs=pl.BlockSpec((tm, D), lambda j, i: (i, j)),
        out_shape=jax.ShapeDtypeStruct((T, 2 * D), BF16),
        compiler_params=_cparams(("parallel", "parallel")),
        name="proj_rope",
    )(h, w, cosf, sa, sb)


def _conv_proj_kernel(tps, h_ref, w_ref, wgt_ref, cw_ref, cb_ref, bi_ref, bf_ref, bit_ref, bft_ref,
                      o_ref, gi_ref, gf_ref, git_ref, gft_ref, buf):
    i = pl.program_id(0)
    tm = h_ref.shape[0]
    n = o_ref.shape[1]
    h = h_ref[...]
    full = jnp.dot(h, w_ref[...], preferred_element_type=F32)
    acc = full[:, :n]

    @pl.when(i % tps == 0)
    def _():
        buf[0:SUBLANES, :] = jnp.zeros((SUBLANES, n), F32)

    buf[SUBLANES:SUBLANES + tm, :] = acc
    cw = cw_ref[...]
    y = cb_ref[...]
    for j in range(CONV_WIDTH):
        off = SUBLANES - (CONV_WIDTH - 1) + j
        y = y + buf[off:off + tm, :] * cw[j:j + 1, :]
    buf[0:SUBLANES, :] = buf[tm:tm + SUBLANES, :]
    y = y * _sigmoid(y)
    lane = lax.broadcasted_iota(jnp.int32, (1, n), 1)
    kscale = jnp.where(lane >= HEADS * ML_QK, ML_QK ** -0.5, 1.0).astype(F32)
    o_ref[...] = (y * kscale).astype(o_ref.dtype)

    gates = full[:, n:n + LANES]
    gi_ref[...] = gates[:, :HEADS] + bi_ref[...]
    gf_ref[...] = gates[:, HEADS:2 * HEADS] + bf_ref[...]
    gates_t = lax.dot_general(wgt_ref[...], h, (((1,), (1,)), ((), ())), preferred_element_type=F32)
    git_ref[...] = gates_t[:HEADS] + bit_ref[...]
    gft_ref[...] = gates_t[HEADS:] + bft_ref[...]


def _conv_proj(h, w, wgi, wgf, conv_w, conv_b, bi, bf, tps, tm):
    T, D = h.shape
    N = w.shape[1]
    wg = jnp.concatenate([wgi, wgf], axis=1)
    w_ext = jnp.concatenate([w, jnp.pad(wg, ((0, 0), (0, LANES - 2 * HEADS)))], axis=1)
    full = lambda shape: pl.BlockSpec(shape, lambda i: tuple(0 for _ in shape))
    return pl.pallas_call(
        functools.partial(_conv_proj_kernel, tps),
        grid=(T // tm,),
        in_specs=[pl.BlockSpec((tm, D), lambda i: (i, 0)),
                  full((D, N + LANES)), full((2 * HEADS, D)),
                  full((CONV_WIDTH, N)), full((1, N)),
                  full((1, HEADS)), full((1, HEADS)), full((HEADS, 1)), full((HEADS, 1))],
        out_specs=[pl.BlockSpec((tm, N), lambda i: (i, 0)),
                   pl.BlockSpec((tm, HEADS), lambda i: (i, 0)),
                   pl.BlockSpec((tm, HEADS), lambda i: (i, 0)),
                   pl.BlockSpec((HEADS, tm), lambda i: (0, i)),
                   pl.BlockSpec((HEADS, tm), lambda i: (0, i))],
        out_shape=[jax.ShapeDtypeStruct((T, N), BF16),
                   jax.ShapeDtypeStruct((T, HEADS), F32),
                   jax.ShapeDtypeStruct((T, HEADS), F32),
                   jax.ShapeDtypeStruct((HEADS, T), F32),
                   jax.ShapeDtypeStruct((HEADS, T), F32)],
        scratch_shapes=[pltpu.VMEM((tm + 2 * SUBLANES, N), F32)],
        compiler_params=_cparams(("arbitrary",)),
        name="proj_conv",
    )(h, w_ext, wg.T, conv_w, conv_b.reshape(1, N),
      bi.reshape(1, HEADS), bf.reshape(1, HEADS), bi.reshape(HEADS, 1), bf.reshape(HEADS, 1))


def _attn_kernel(lam_init, q_ref, k_ref, v_ref, ga_ref, lq1_ref, lk1_ref, lq2_ref, lk2_ref, sw_ref,
                 o_ref, qt_s, vt_s, sa_s, sb_s, m_s, l_s, acc_s):
    qi = pl.program_id(2)
    tq = q_ref.shape[0]
    tk = vt_s.shape[2]
    ratio = tq // tk

    @pl.when(qi == 0)
    def _():
        for kk in range(vt_s.shape[0]):
            vt_s[kk] = v_ref[kk * tk:(kk + 1) * tk, :].T

    q = q_ref[...].astype(F32)
    lane = lax.broadcasted_iota(jnp.int32, q.shape, 1)
    qt_s[0] = jnp.where(lane < DA_QK, q, 0.0).T.astype(BF16)
    qt_s[1] = jnp.where(lane >= DA_QK, q, 0.0).T.astype(BF16)
    m_s[...] = jnp.full(m_s.shape, -jnp.inf, F32)
    l_s[...] = jnp.zeros(l_s.shape, F32)
    acc_s[...] = jnp.zeros(acc_s.shape, F32)

    def scores(kk, st_ref):
        k = k_ref[pl.ds(pl.multiple_of(kk * tk, tk), tk), :]
        for c in range(2):
            st_ref[c] = jnp.dot(k, qt_s[c], preferred_element_type=F32)

    def process(kk, st_ref, diag_offset):
        vt = vt_s[kk]
        for c in range(2):
            st = st_ref[c]
            if diag_offset is not None:
                key = lax.broadcasted_iota(jnp.int32, st.shape, 0) + diag_offset
                qry = lax.broadcasted_iota(jnp.int32, st.shape, 1)
                st = jnp.where(key <= qry, st, -jnp.inf)
            m_prev = m_s[c]
            m_new = jnp.maximum(m_prev, jnp.max(st, axis=0, keepdims=True))
            alpha = jnp.exp2(m_prev - m_new)
            p = jnp.exp2(st - m_new)
            l_s[c] = alpha * l_s[c] + jnp.sum(p, axis=0, keepdims=True)
            acc_s[c] = alpha * acc_s[c] + jnp.dot(vt, p.astype(BF16), preferred_element_type=F32)
            m_s[c] = m_new

    def diagonal(first, cur, nxt):
        for d in range(ratio):
            if d + 1 < ratio:
                scores(first + d + 1, nxt)
            process(first + d, cur, d * tk)
            cur, nxt = nxt, cur

    n_below = qi * ratio
    scores(0, sa_s)

    def pair(i, carry):
        kk = 2 * i
        scores(kk + 1, sb_s)
        process(kk, sa_s, None)
        scores(kk + 2, sa_s)
        process(kk + 1, sb_s, None)
        return carry

    lax.fori_loop(0, n_below // 2, pair, 0)
    odd = lax.rem(n_below, 2) == 1

    @pl.when(odd)
    def _():
        scores(n_below, sb_s)
        process(n_below - 1, sa_s, None)
        diagonal(n_below, sb_s, sa_s)

    @pl.when(jnp.logical_not(odd))
    def _():
        diagonal(n_below, sa_s, sb_s)

    lam = (jnp.exp(jnp.sum(lq1_ref[...] * lk1_ref[...], axis=-1, keepdims=True))
           - jnp.exp(jnp.sum(lq2_ref[...] * lk2_ref[...], axis=-1, keepdims=True)) + lam_init)
    o = (acc_s[0] / l_s[0] - lam * (acc_s[1] / l_s[1])).T
    o = _rms(o) * sw_ref[...] * (1.0 - lam_init)
    o_ref[...] = (o * _sigmoid(ga_ref[...].astype(F32))).astype(o_ref.dtype)


def _attention(qk, plain, lam_vecs, subln_w, lam_init, B, S, tq, tk):
    T = qk.shape[0]
    nq = S // tq
    vec = pl.BlockSpec((1, DA_QK), lambda b, h, i: (0, 0))
    return pl.pallas_call(
        functools.partial(_attn_kernel, lam_init),
        grid=(B, HEADS, nq),
        in_specs=[pl.BlockSpec((tq, LANES), lambda b, h, i: (b * nq + i, h)),
                  pl.BlockSpec((S, LANES), lambda b, h, i: (b, HEADS + h)),
                  pl.BlockSpec((S, LANES), lambda b, h, i: (b, h)),
                  pl.BlockSpec((tq, LANES), lambda b, h, i: (b * nq + i, 3 * HEADS + h)),
                  vec, vec, vec, vec,
                  pl.BlockSpec((1, HEAD_V), lambda b, h, i: (0, 0))],
        out_specs=pl.BlockSpec((tq, LANES), lambda b, h, i: (b * nq + i, h)),
        out_shape=jax.ShapeDtypeStruct((T, D_MODEL), BF16),
        scratch_shapes=[pltpu.VMEM((2, LANES, tq), BF16),
                        pltpu.VMEM((S // tk, HEAD_V, tk), BF16),
                        pltpu.VMEM((2, tk, tq), F32),
                        pltpu.VMEM((2, tk, tq), F32),
                        pltpu.VMEM((2, 1, tq), F32),
                        pltpu.VMEM((2, 1, tq), F32),
                        pltpu.VMEM((2, HEAD_V, tq), F32)],
        compiler_params=_cparams(("parallel", "parallel", "arbitrary")),
        name="attn",
    )(qk, qk, plain, plain, *[v.reshape(1, DA_QK) for v in lam_vecs], subln_w.reshape(1, HEAD_V))


def _mlstm_kernel(qk_ref, v_ref, om_ref, gm_ref, gi_ref, gf_ref, git_ref, gft_ref, nw_ref,
                  o_ref, c_s, n_s, m_s):
    L = qk_ref.shape[0]

    @pl.when(pl.program_id(1) == 0)
    def _():
        c_s[...] = jnp.zeros(c_s.shape, F32)
        n_s[...] = jnp.zeros(n_s.shape, F32)
        m_s[...] = jnp.zeros(m_s.shape, F32)

    row = lax.broadcasted_iota(jnp.int32, (L, L), 0)
    col = lax.broadcasted_iota(jnp.int32, (L, L), 1)
    tri = (col <= row).astype(F32)
    causal_t = row <= col
    tri_t = causal_t.astype(F32)
    bcols = jnp.dot(tri, _log_sigmoid(gf_ref[...]), precision=HIGHEST, preferred_element_type=F32)
    brows = jnp.dot(_log_sigmoid(gft_ref[...]), tri_t, precision=HIGHEST, preferred_element_type=F32)
    ucols = gi_ref[...] - bcols
    git = git_ref[...]
    lane = lax.broadcasted_iota(jnp.int32, (1, LANES), 1)
    first_row = lax.broadcasted_iota(jnp.int32, (SUBLANES, 1), 0) == 0
    nt = (((1,), (1,)), ((), ()))

    for h in range(HEADS):
        p = h // 2
        hmask = ((lane >= (h % 2) * ML_QK) & (lane < (h % 2 + 1) * ML_QK)).astype(F32)
        qh = (qk_ref[:, p * LANES:(p + 1) * LANES].astype(F32) * hmask).astype(BF16)
        kp = qk_ref[:, (HEADS // 2 + p) * LANES:(HEADS // 2 + p + 1) * LANES]
        vt = v_ref[:, h * HEAD_V:(h + 1) * HEAD_V].T
        br = brows[h:h + 1, :]
        igr = git[h:h + 1, :]
        g_tot = br[:, L - 1:L]
        m_prev = m_s[h]
        ct_prev = c_s[h]
        n_prev = n_s[h]

        dm = jnp.where(causal_t, br + ucols[:, h:h + 1], -jnp.inf)
        m_inter = br + m_prev
        m_j = jnp.maximum(jnp.max(dm, axis=0, keepdims=True), m_inter)
        st = lax.dot_general(kp, qh, nt, preferred_element_type=F32)
        qkw = st * jnp.exp(dm - m_j)
        inter_w = jnp.exp(m_inter - m_j)
        num = (jnp.dot(vt, qkw.astype(BF16), preferred_element_type=F32)
               + inter_w * lax.dot_general(ct_prev.astype(BF16), qh, nt, preferred_element_type=F32))
        qn = lax.dot_general(n_prev.astype(BF16), qh, nt, preferred_element_type=F32)[0:1, :]
        den = jnp.sum(qkw, axis=0, keepdims=True) + inter_w * qn
        ht = num / jnp.maximum(jnp.abs(den), jnp.exp(-m_j))

        a = g_tot - br + igr
        m_loc = jnp.max(a, axis=-1, keepdims=True)
        w_loc = jnp.exp(a - m_loc)
        m_new = jnp.maximum(g_tot + m_prev, m_loc)
        dec = jnp.exp(g_tot + m_prev - m_new)
        inc = jnp.exp(m_loc - m_new)
        c_loc = jnp.dot((vt.astype(F32) * w_loc).astype(BF16), kp, preferred_element_type=F32)
        w8 = jnp.where(first_row, w_loc, 0.0)
        n_loc = jnp.dot(w8, kp.astype(F32), precision=HIGHEST, preferred_element_type=F32)
        c_s[h] = dec * ct_prev + inc * c_loc
        n_s[h] = dec * n_prev + inc * n_loc
        m_s[h] = m_new

        sl = slice(h * HEAD_V, (h + 1) * HEAD_V)
        yt = ht * lax.rsqrt(jnp.mean(ht * ht, axis=0, keepdims=True) + NORM_EPS)
        y = yt.T * nw_ref[:, sl]
        y = y * _sigmoid(om_ref[:, sl].astype(F32)) * _sigmoid(gm_ref[:, sl].astype(F32))
        o_ref[:, sl] = y.astype(o_ref.dtype)


def _mlstm(qkc, plain, gi, gf, git, gft, ml_norm_w, B, S):
    T, D = qkc.shape
    L = ML_CHUNK
    nc = S // L
    wide = lambda cb: pl.BlockSpec((L, D), lambda b, c: (b * nc + c, cb))
    return pl.pallas_call(
        _mlstm_kernel,
        grid=(B, nc),
        in_specs=[wide(0), wide(1), wide(2), wide(4),
                  pl.BlockSpec((L, HEADS), lambda b, c: (b * nc + c, 0)),
                  pl.BlockSpec((L, HEADS), lambda b, c: (b * nc + c, 0)),
                  pl.BlockSpec((HEADS, L), lambda b, c: (0, b * nc + c)),
                  pl.BlockSpec((HEADS, L), lambda b, c: (0, b * nc + c)),
                  pl.BlockSpec((1, D), lambda b, c: (0, 0))],
        out_specs=pl.BlockSpec((L, D), lambda b, c: (b * nc + c, 0)),
        out_shape=jax.ShapeDtypeStruct((T, D), BF16),
        scratch_shapes=[pltpu.VMEM((HEADS, LANES, HEAD_V), F32),
                        pltpu.VMEM((HEADS, SUBLANES, LANES), F32),
                        pltpu.VMEM((HEADS, 1, 1), F32)],
        compiler_params=_cparams(("parallel", "arbitrary")),
        name="mlstm",
    )(qkc, plain, plain, plain, gi, gf, git, gft, ml_norm_w.reshape(1, D))


def _outproj_kernel(ya_ref, ym_ref, x_ref, w_ref, gt_ref, nw_ref, sc_ref, sh_ref, x1_ref, h2t_ref):
    merged = (ya_ref[...].astype(F32) + ym_ref[...].astype(F32)).astype(BF16)
    x1 = x_ref[...] + gt_ref[0] * jnp.dot(merged, w_ref[...], preferred_element_type=F32)
    x1_ref[...] = x1
    h2 = _rms(x1) * nw_ref[...] * (1.0 + sc_ref[0]) + sh_ref[0]
    h2t_ref[...] = h2.T.astype(h2t_ref.dtype)


def _outproj(ya, ym, x2, w_out, norm2_w, mod3, B, tps, tm):
    T, D = x2.shape
    row = lambda k: pl.BlockSpec((1, 1, D), lambda i: (k * B + i // tps, 0, 0))
    tile = pl.BlockSpec((tm, D), lambda i: (i, 0))
    return pl.pallas_call(
        _outproj_kernel,
        grid=(T // tm,),
        in_specs=[tile, tile, tile, pl.BlockSpec((D, D), lambda i: (0, 0)), row(2),
                  pl.BlockSpec((1, D), lambda i: (0, 0)), row(4), row(3)],
        out_specs=[tile, pl.BlockSpec((D, tm), lambda i: (0, i))],
        out_shape=[jax.ShapeDtypeStruct((T, D), F32), jax.ShapeDtypeStruct((D, T), BF16)],
        compiler_params=_cparams(("parallel",)),
        name="outproj",
    )(ya, ym, x2, w_out, mod3, norm2_w.reshape(1, D), mod3, mod3)


def _topk_ranks(s):
    n = s.shape[0]
    it = lax.broadcasted_iota(jnp.int32, s.shape, 0)
    rank = jnp.full(s.shape, float(TOPK), F32)
    x = s
    vals = []
    for r in range(TOPK):
        m = jnp.max(x, axis=0, keepdims=True)
        first = jnp.min(jnp.where(x == m, it, n), axis=0, keepdims=True)
        hit = it == first
        x = jnp.where(hit, -jnp.inf, x)
        rank = jnp.where(hit, float(r), rank)
        vals.append(m)
    return rank, vals


MARK_BASE = 3.0e38
MARK_STEP = 1.0e36


def _topk_ranks_distinct(s):
    x = s
    vals = []
    for r in range(TOPK):
        m = jnp.max(x, axis=0, keepdims=True)
        x = jnp.where(x == m, -(MARK_BASE + r * MARK_STEP), x)
        vals.append(m)
    taken = x <= -MARK_BASE
    rank = jnp.where(taken, jnp.round((-x - MARK_BASE) * (1.0 / MARK_STEP)), float(TOPK))
    removed = jnp.sum(jnp.where(taken, 1.0, 0.0), axis=0, keepdims=True)
    return rank, vals, removed == float(TOPK)


def _cand_select(cand, top):
    it = lax.broadcasted_iota(jnp.int32, cand.shape, 0)
    sel = jnp.zeros(cand.shape, F32)
    z = jnp.zeros(top.shape, F32)
    x = cand
    for _ in range(TOPK):
        m = jnp.max(x, axis=0, keepdims=True)
        first = jnp.min(jnp.where(x == m, it, cand.shape[0]), axis=0, keepdims=True)
        hit = it == first
        x = jnp.where(hit, -jnp.inf, x)
        sel = jnp.where(hit, 1.0, sel)
        z = z + jnp.exp(m - top)
    return sel, z


def _cand_select_distinct(cand, top):
    z = jnp.zeros(top.shape, F32)
    x = cand
    for _ in range(TOPK):
        m = jnp.max(x, axis=0, keepdims=True)
        x = jnp.where(x == m, -MARK_BASE, x)
        z = z + jnp.exp(m - top)
    sel = jnp.where(x == -MARK_BASE, 1.0, 0.0)
    return sel, z, jnp.sum(sel, axis=0, keepdims=True) == float(TOPK)


def _selection(s1, s2, exact):
    if exact:
        (rank1, v1), (rank2, v2) = _topk_ranks(s1), _topk_ranks(s2)
    else:
        (rank1, v1, ok1), (rank2, v2, ok2) = _topk_ranks_distinct(s1), _topk_ranks_distinct(s2)
    tt = s1.shape[1]
    rows = [v1[r1] + v2[r2] for (r1, r2) in _CAND]
    rows += [jnp.full((1, tt), -jnp.inf, F32)] * (_CAND_ROWS - len(_CAND))
    cand = jnp.concatenate(rows, axis=0)
    top = v1[0] + v2[0]
    if exact:
        sel, z = _cand_select(cand, top)
        safe = None
    else:
        sel, z, ok3 = _cand_select_distinct(cand, top)
        safe = ok1 & ok2 & ok3
    a = jnp.zeros(s1.shape, F32)
    start = 0
    for r1 in range(TOPK):
        width = sum(1 for c in _CAND if c[0] == r1)
        cnt = jnp.sum(sel[start:start + width, :], axis=0, keepdims=True)
        a = jnp.where(rank1 == float(r1), cnt, a)
        start += width
    return (rank2, jnp.exp(s2 - v2[0]), a, jnp.exp(s1 - v1[0]) / z), safe


SEL_HEADS = 2


def _peer_sel_kernel(h2t_ref, wqt_ref, keys_ref, r2_ref, e2_ref, a_ref, c_ref):
    ht = h2t_ref[...]
    scores = []
    for hp in range(2 * SEL_HEADS):
        qt = jnp.dot(wqt_ref[hp * LANES:(hp + 1) * LANES, :], ht, preferred_element_type=F32)
        scores.append(jnp.dot(keys_ref[hp], qt.astype(BF16), preferred_element_type=F32))

    def write(hh, vals):
        for ref, val in zip((r2_ref, e2_ref, a_ref, c_ref), vals):
            for lt in range(val.shape[1] // LANES):
                ref[lt, hh * N_KEYS:(hh + 1) * N_KEYS, :] = val[:, lt * LANES:(lt + 1) * LANES].astype(ref.dtype)

    unsafe = 0.0
    for hh in range(SEL_HEADS):
        vals, safe = _selection(scores[2 * hh], scores[2 * hh + 1], exact=False)
        write(hh, vals)
        unsafe = unsafe + jnp.sum(jnp.where(safe, 0.0, 1.0))

    @pl.when(unsafe > 0.0)
    def _():
        for hh in range(SEL_HEADS):
            write(hh, _selection(scores[2 * hh], scores[2 * hh + 1], exact=True)[0])


def _peer_sel(h2t, wqt, keys, tt):
    D, T = h2t.shape
    out = pl.BlockSpec((tt // LANES, SEL_HEADS * N_KEYS, LANES), lambda i, h: (i, h, 0))
    shp = lambda dt: jax.ShapeDtypeStruct((T // LANES, HEADS * N_KEYS, LANES), dt)
    return pl.pallas_call(
        _peer_sel_kernel,
        grid=(T // tt, HEADS // SEL_HEADS),
        in_specs=[pl.BlockSpec((D, tt), lambda i, h: (0, i)),
                  pl.BlockSpec((SEL_HEADS * 2 * LANES, D), lambda i, h: (h, 0)),
                  pl.BlockSpec((SEL_HEADS * 2, N_KEYS, LANES), lambda i, h: (h, 0, 0))],
        out_specs=[out, out, out, out],
        out_shape=[shp(BF16), shp(BF16), shp(F32), shp(F32)],
        compiler_params=_cparams(("parallel", "parallel")),
        name="peer_sel",
    )(h2t, wqt, keys)


PEER_ROWS = 128
GATE_GROUP = 4
MXU_COLS = 256
ACT_ROWS = 256
OUT_ROWS = 512


def _gate_rows(row, rows):
    packed = jnp.broadcast_to(row, (2 * SUBLANES, LANES)).astype(BF16)
    return jnp.tile(packed, (rows // (2 * SUBLANES), 1))


def _peer_main_kernel(final, h2t_ref, u_ref, vt_ref, r2_ref, e2_ref, a_ref, c_ref, x1_ref, gt_ref, fw_ref,
                      o_ref, acc_ref, act_ref, p_ref):
    e = pl.program_id(1)
    eb, tt = act_ref.shape
    nj = eb // N_KEYS

    @pl.when(e == 0)
    def _():
        acc_ref[...] = jnp.zeros(acc_ref.shape, F32)

    n_half = tt // MXU_COLS
    tcols = lambda th: slice(th * MXU_COLS, (th + 1) * MXU_COLS)

    def act_piece(th, m):
        ms = slice(m * ACT_ROWS, (m + 1) * ACT_ROWS)
        act_ref[ms, tcols(th)] = jnp.dot(u_ref[ms, :], h2t_ref[:, tcols(th)], preferred_element_type=F32)

    def out_piece(th, r):
        rs = slice(r * OUT_ROWS, (r + 1) * OUT_ROWS)
        acc_ref[rs, tcols(th)] += jnp.dot(vt_ref[rs, :], p_ref[:, tcols(th)], preferred_element_type=F32)

    def gate_chunk(tl, bs):
        ls = slice(tl * LANES, (tl + 1) * LANES)
        for j0 in range(0, nj, GATE_GROUP):
            g = [None] * GATE_GROUP
            for h in range(HEADS):
                rs = slice(h * N_KEYS + bs * PEER_ROWS, h * N_KEYS + (bs + 1) * PEER_ROWS)
                r2c = r2_ref[tl, rs, :]
                e2c = e2_ref[tl, rs, :]
                for jj in range(GATE_GROUP):
                    idx = h * N_KEYS + e * nj + j0 + jj
                    arow = _gate_rows(a_ref[tl, pl.ds(idx, 1), :], PEER_ROWS)
                    crow = _gate_rows(c_ref[tl, pl.ds(idx, 1), :], PEER_ROWS)
                    w = jnp.where(r2c < arow, e2c, jnp.zeros_like(e2c)) * crow
                    g[jj] = w if g[jj] is None else g[jj] + w
            for jj in range(GATE_GROUP):
                es = slice((j0 + jj) * N_KEYS + bs * PEER_ROWS, (j0 + jj) * N_KEYS + (bs + 1) * PEER_ROWS)
                aj = act_ref[es, ls].astype(BF16)
                gelu = 0.5 * aj * (1.0 + lax.erf(aj * (2.0 ** -0.5)))
                p_ref[es, ls] = g[jj] * gelu

    n_act = eb // ACT_ROWS
    n_out = acc_ref.shape[0] // OUT_ROWS
    for m in range(n_act):
        act_piece(0, m)
    for th in range(n_half):
        mxu_work = [functools.partial(act_piece, th + 1, m) for m in range(n_act)] if th + 1 < n_half else []
        if th > 0:
            mxu_work += [functools.partial(out_piece, th - 1, r) for r in range(n_out)]
        chunks = [(th * (MXU_COLS // LANES) + lt, bs)
                  for lt in range(MXU_COLS // LANES) for bs in range(N_KEYS // PEER_ROWS)]
        per_chunk = -(-len(mxu_work) // len(chunks))
        for tl, bs in chunks:
            gate_chunk(tl, bs)
            for piece in mxu_work[:per_chunk]:
                piece()
            mxu_work = mxu_work[per_chunk:]
    for r in range(n_out):
        out_piece(n_half - 1, r)

    @pl.when(e == pl.num_programs(1) - 1)
    def _():
        x2 = x1_ref[...] + gt_ref[0] * acc_ref[...].T
        o_ref[...] = _rms(x2) * fw_ref[...] if final else x2


def _peer_main(h2t, u, vt, sel, x1, mod3, fw, final, B, tps, tt, eb):
    D, T = h2t.shape
    ne = u.shape[0]
    wide = pl.BlockSpec((tt // LANES, HEADS * N_KEYS, LANES), lambda i, e: (i, 0, 0))
    return pl.pallas_call(
        functools.partial(_peer_main_kernel, final),
        grid=(T // tt, ne // eb),
        in_specs=[pl.BlockSpec((D, tt), lambda i, e: (0, i)),
                  pl.BlockSpec((eb, D), lambda i, e: (e, 0)),
                  pl.BlockSpec((D, eb), lambda i, e: (0, e)),
                  wide, wide, wide, wide,
                  pl.BlockSpec((tt, D), lambda i, e: (i, 0)),
                  pl.BlockSpec((1, 1, D), lambda i, e: (5 * B + i // tps, 0, 0)),
                  pl.BlockSpec((1, D), lambda i, e: (0, 0))],
        out_specs=pl.BlockSpec((tt, D), lambda i, e: (i, 0)),
        out_shape=jax.ShapeDtypeStruct((T, D), F32),
        scratch_shapes=[pltpu.VMEM((D, tt), F32),
                        pltpu.VMEM((eb, tt), F32),
                        pltpu.VMEM((eb, tt), BF16)],
        compiler_params=_cparams(("parallel", "arbitrary")),
        name="peer_main",
    )(h2t, u, vt, *sel, x1, mod3, fw.reshape(1, D))


def _rope_tables(S):
    inv = ROPE_THETA ** (-jnp.arange(ROPE_HALF, dtype=F32) * 2.0 / ROPE_DIM)
    ang = jnp.arange(S, dtype=jnp.int32).astype(F32)[:, None] * inv[None, :]
    cos, sin = jnp.cos(ang), jnp.sin(ang)
    zeros = jnp.zeros((S, DA_QK - ROPE_DIM), F32)
    z8 = jnp.zeros((S, ROPE_HALF), F32)
    cosf = jnp.concatenate([cos, cos, zeros + 1.0] * 2, axis=1)
    sa = jnp.concatenate([-sin, z8, zeros] * 2, axis=1)
    sb = jnp.concatenate([z8, sin, zeros] * 2, axis=1)
    return cosf, sa, sb


def kernel(x, c, ada_w, ada_b, norm1_w, norm2_w, w_in, conv_w, conv_b, ml_i_bias, ml_f_bias, ml_norm_w, lam_q1, lam_k1, lam_q2, lam_k2, subln_w, w_out, peer_wq, peer_keys, peer_u, peer_v, final_norm_w):
    B, S, D = x.shape
    assert D == D_MODEL and S % PROJ_TM == 0 and S % ATTN_TQ == 0
    T = B * S
    depth = w_in.shape[0]
    tm = 512
    tps = S // tm
    tables = _rope_tables(S)
    x2 = x.reshape(T, D)
    for l in range(depth):
        mod3 = _mod(c, ada_w[l], ada_b[l])
        h = _norm(x2, norm1_w[l], mod3, B, tps, tm)

        wl = w_in[l]
        o = 0
        cols = {}
        for name, width in (("qa", D), ("ka", D), ("va", D), ("qm", D // 2), ("km", D // 2), ("vm", D),
                            ("om", D), ("ip", HEADS), ("fp", HEADS), ("ga", D), ("gm", D)):
            cols[name] = wl[:, o:o + width]
            o += width
        w_rope = jnp.concatenate([cols["qa"], cols["ka"]], axis=1).astype(BF16)
        w_conv = jnp.concatenate([cols["qm"], cols["km"]], axis=1).astype(BF16)
        w_plain = jnp.concatenate([cols[n] for n in ("va", "vm", "om", "ga", "gm")], axis=1).astype(BF16)

        qk = _rope_proj(h, w_rope, tables, S // PROJ_TM, PROJ_TM)
        plain = _proj(h, w_plain, PROJ_TM, D)
        qkc, gi, gf, git, gft = _conv_proj(h, w_conv, cols["ip"].astype(BF16), cols["fp"].astype(BF16),
                                           conv_w[l], conv_b[l], ml_i_bias[l], ml_f_bias[l], tps, tm)

        lam_init = 0.8 - 0.6 * math.exp(-0.3 * l)
        ya = _attention(qk, plain, (lam_q1[l], lam_k1[l], lam_q2[l], lam_k2[l]), subln_w[l], lam_init, B, S, ATTN_TQ, ATTN_TK)
        ym = _mlstm(qkc, plain, gi, gf, git, gft, ml_norm_w[l], B, S)
        x1, h2t = _outproj(ya, ym, x2, w_out[l].astype(BF16), norm2_w[l], mod3, B, tps, tm)

        wqt = peer_wq[l].T.astype(BF16)
        keys = peer_keys[l].reshape(2 * HEADS, N_KEYS, LANES).astype(BF16)
        sel = _peer_sel(h2t, wqt, keys, tm)
        x2 = _peer_main(h2t, peer_u[l].astype(BF16), peer_v[l].T.astype(BF16), sel, x1, mod3,
                        final_norm_w, l == depth - 1, B, tps, tm, 16 * N_KEYS)
    if depth == 0:
        raise ValueError("depth must be positive")
    return x2.reshape(B, S, D)
```

```python
import functools
import math

import jax
import jax.numpy as jnp
from jax import lax
from jax.experimental import pallas as pl
from jax.experimental.pallas import tpu as pltpu

F32 = jnp.float32
BF16 = jnp.bfloat16
HIGHEST = lax.Precision.HIGHEST

NORM_EPS = 1e-6
LOG2E = 1.4426950408889634
D_MODEL = 1024
HEADS = 8
HEAD_V = 128
DA_QK = 64
ROPE_DIM = 16
ROPE_HALF = 8
ROPE_THETA = 500000.0
ML_QK = 64
CONV_WIDTH = 4
ML_CHUNK = 128
ROW_TILE = 512
PEER_EXPERT_BLOCK = 2048
PROJ_TM = 1024
ATTN_TQ = 1024
ATTN_TK = 512
N_KEYS = 128
TOPK = 16
LANES = 128
SUBLANES = 8
VMEM_LIMIT = 56 * 1024 * 1024

_CAND = [(r1, r2) for r1 in range(TOPK) for r2 in range(TOPK) if (r1 + 1) * (r2 + 1) <= TOPK]
_CAND_ROWS = -(-len(_CAND) // SUBLANES) * SUBLANES


def _cparams(sem):
    return pltpu.CompilerParams(dimension_semantics=sem, vmem_limit_bytes=VMEM_LIMIT)


def _rms(x):
    return x * lax.rsqrt(jnp.mean(x * x, axis=-1, keepdims=True) + NORM_EPS)


def _sigmoid(x):
    return 1.0 / (1.0 + jnp.exp(-x))


def _log_sigmoid(x):
    return jnp.minimum(x, 0.0) - jnp.log(1.0 + jnp.exp(-jnp.abs(x)))


def _mod_kernel(c_ref, w_ref, b_ref, o_ref):
    c = c_ref[...]
    cond = c * _sigmoid(c)
    o_ref[0] = jnp.dot(cond, w_ref[...], precision=HIGHEST, preferred_element_type=F32) + b_ref[...]


def _mod(c, ada_w, ada_b):
    B, D = c.shape
    out = pl.pallas_call(
        _mod_kernel,
        grid=(6,),
        in_specs=[pl.BlockSpec((B, D), lambda j: (0, 0)),
                  pl.BlockSpec((D, D), lambda j: (0, j)),
                  pl.BlockSpec((1, D), lambda j: (0, j))],
        out_specs=pl.BlockSpec((1, B, D), lambda j: (j, 0, 0)),
        out_shape=jax.ShapeDtypeStruct((6, B, D), F32),
        compiler_params=_cparams(("parallel",)),
        name="mod",
    )(c, ada_w, ada_b.reshape(1, 6 * D))
    return out.reshape(6 * B, 1, D)


def _norm_kernel(x_ref, w_ref, sc_ref, sh_ref, o_ref):
    y = _rms(x_ref[...]) * w_ref[...]
    o_ref[...] = (y * (1.0 + sc_ref[0]) + sh_ref[0]).astype(o_ref.dtype)


def _norm(x2, w, mod3, B, tps, tm):
    T, D = x2.shape
    return pl.pallas_call(
        _norm_kernel,
        grid=(T // tm,),
        in_specs=[pl.BlockSpec((tm, D), lambda i: (i, 0)),
                  pl.BlockSpec((1, D), lambda i: (0, 0)),
                  pl.BlockSpec((1, 1, D), lambda i: (1 * B + i // tps, 0, 0)),
                  pl.BlockSpec((1, 1, D), lambda i: (0 * B + i // tps, 0, 0))],
        out_specs=pl.BlockSpec((tm, D), lambda i: (i, 0)),
        out_shape=jax.ShapeDtypeStruct((T, D), BF16),
        compiler_params=_cparams(("parallel",)),
        name="norm1",
    )(x2, w.reshape(1, D), mod3, mod3)


def _proj_kernel(h_ref, w_ref, o_ref):
    o_ref[...] = jnp.dot(h_ref[...], w_ref[...], preferred_element_type=F32).astype(o_ref.dtype)


def _proj(h, w, tm, tn):
    T, D = h.shape
    N = w.shape[1]
    return pl.pallas_call(
        _proj_kernel,
        grid=(N // tn, T // tm),
        in_specs=[pl.BlockSpec((tm, D), lambda j, i: (i, 0)),
                  pl.BlockSpec((D, tn), lambda j, i: (0, j))],
        out_specs=pl.BlockSpec((tm, tn), lambda j, i: (i, j)),
        out_shape=jax.ShapeDtypeStruct((T, N), BF16),
        compiler_params=_cparams(("parallel", "parallel")),
        name="proj_plain",
    )(h, w)


def _rope_proj_kernel(h_ref, w_ref, cos_ref, sa_ref, sb_ref, o_ref):
    acc = jnp.dot(h_ref[...], w_ref[...], preferred_element_type=F32)
    scale = jnp.where(pl.program_id(0) == 0, DA_QK ** -0.5 * LOG2E, 1.0).astype(F32)
    cosf, sa, sb = cos_ref[...], sa_ref[...], sb_ref[...]
    for hh in range(HEADS):
        blk = acc[:, hh * LANES:(hh + 1) * LANES]
        rot = (blk * cosf + pltpu.roll(blk, LANES - ROPE_HALF, 1) * sa
               + pltpu.roll(blk, ROPE_HALF, 1) * sb)
        o_ref[:, hh * LANES:(hh + 1) * LANES] = (rot * scale).astype(o_ref.dtype)


def _rope_proj(h, w, tables, tps, tm):
    T, D = h.shape
    cosf, sa, sb = tables
    tab_spec = pl.BlockSpec((tm, LANES), lambda j, i: (i % tps, 0))
    return pl.pallas_call(
        _rope_proj_kernel,
        grid=(2, T // tm),
        in_specs=[pl.BlockSpec((tm, D), lambda j, i: (i, 0)),
                  pl.BlockSpec((D, D), lambda j, i: (0, j)),
                  tab_spec, tab_spec, tab_spec],
        out_specs=pl.BlockSpec((tm, D), lambda j, i: (i, j)),
        out_shape=jax.ShapeDtypeStruct((T, 2 * D), BF16),
        compiler_params=_cparams(("parallel", "parallel")),
        name="proj_rope",
    )(h, w, cosf, sa, sb)


def _conv_proj_kernel(tps, h_ref, w_ref, wgt_ref, cw_ref, cb_ref, bi_ref, bf_ref, bit_ref, bft_ref,
                      o_ref, gi_ref, gf_ref, git_ref, gft_ref, buf):
    i = pl.program_id(0)
    tm = h_ref.shape[0]
    n = o_ref.shape[1]
    h = h_ref[...]
    full = jnp.dot(h, w_ref[...], preferred_element_type=F32)
    acc = full[:, :n]

    @pl.when(i % tps == 0)
    def _():
        buf[0:SUBLANES, :] = jnp.zeros((SUBLANES, n), F32)

    buf[SUBLANES:SUBLANES + tm, :] = acc
    cw = cw_ref[...]
    y = cb_ref[...]
    for j in range(CONV_WIDTH):
        off = SUBLANES - (CONV_WIDTH - 1) + j
        y = y + buf[off:off + tm, :] * cw[j:j + 1, :]
    buf[0:SUBLANES, :] = buf[tm:tm + SUBLANES, :]
    y = y * _sigmoid(y)
    lane = lax.broadcasted_iota(jnp.int32, (1, n), 1)
    kscale = jnp.where(lane >= HEADS * ML_QK, ML_QK ** -0.5, 1.0).astype(F32)
    o_ref[...] = (y * kscale).astype(o_ref.dtype)

    gates = full[:, n:n + LANES]
    gi_ref[...] = gates[:, :HEADS] + bi_ref[...]
    gf_ref[...] = gates[:, HEADS:2 * HEADS] + bf_ref[...]
    gates_t = lax.dot_general(wgt_ref[...], h, (((1,), (1,)), ((), ())), preferred_element_type=F32)
    git_ref[...] = gates_t[:HEADS] + bit_ref[...]
    gft_ref[...] = gates_t[HEADS:] + bft_ref[...]


def _conv_proj(h, w, wgi, wgf, conv_w, conv_b, bi, bf, tps, tm):
    T, D = h.shape
    N = w.shape[1]
    wg = jnp.concatenate([wgi, wgf], axis=1)
    w_ext = jnp.concatenate([w, jnp.pad(wg, ((0, 0), (0, LANES - 2 * HEADS)))], axis=1)
    full = lambda shape: pl.BlockSpec(shape, lambda i: tuple(0 for _ in shape))
    return pl.pallas_call(
        functools.partial(_conv_proj_kernel, tps),
        grid=(T // tm,),
        in_specs=[pl.BlockSpec((tm, D), lambda i: (i, 0)),
                  full((D, N + LANES)), full((2 * HEADS, D)),
                  full((CONV_WIDTH, N)), full((1, N)),
                  full((1, HEADS)), full((1, HEADS)), full((HEADS, 1)), full((HEADS, 1))],
        out_specs=[pl.BlockSpec((tm, N), lambda i: (i, 0)),
                   pl.BlockSpec((tm, HEADS), lambda i: (i, 0)),
                   pl.BlockSpec((tm, HEADS), lambda i: (i, 0)),
                   pl.BlockSpec((HEADS, tm), lambda i: (0, i)),
                   pl.BlockSpec((HEADS, tm), lambda i: (0, i))],
        out_shape=[jax.ShapeDtypeStruct((T, N), BF16),
                   jax.ShapeDtypeStruct((T, HEADS), F32),
                   jax.ShapeDtypeStruct((T, HEADS), F32),
                   jax.ShapeDtypeStruct((HEADS, T), F32),
                   jax.ShapeDtypeStruct((HEADS, T), F32)],
        scratch_shapes=[pltpu.VMEM((tm + 2 * SUBLANES, N), F32)],
        compiler_params=_cparams(("arbitrary",)),
        name="proj_conv",
    )(h, w_ext, wg.T, conv_w, conv_b.reshape(1, N),
      bi.reshape(1, HEADS), bf.reshape(1, HEADS), bi.reshape(HEADS, 1), bf.reshape(HEADS, 1))


def _attn_kernel(lam_init, q_ref, k_ref, v_ref, ga_ref, lq1_ref, lk1_ref, lq2_ref, lk2_ref, sw_ref,
                 o_ref, qt_s, vt_s, sa_s, sb_s, m_s, l_s, acc_s):
    qi = pl.program_id(2)
    tq = q_ref.shape[0]
    tk = vt_s.shape[2]
    ratio = tq // tk

    @pl.when(qi == 0)
    def _():
        for kk in range(vt_s.shape[0]):
            vt_s[kk] = v_ref[kk * tk:(kk + 1) * tk, :].T

    q = q_ref[...].astype(F32)
    lane = lax.broadcasted_iota(jnp.int32, q.shape, 1)
    qt_s[0] = jnp.where(lane < DA_QK, q, 0.0).T.astype(BF16)
    qt_s[1] = jnp.where(lane >= DA_QK, q, 0.0).T.astype(BF16)
    m_s[...] = jnp.full(m_s.shape, -jnp.inf, F32)
    l_s[...] = jnp.zeros(l_s.shape, F32)
    acc_s[...] = jnp.zeros(acc_s.shape, F32)

    def scores(kk, st_ref, q0=0):
        k = k_ref[pl.ds(pl.multiple_of(kk * tk, tk), tk), :]
        for c in range(2):
            st_ref[c, :, q0:] = jnp.dot(k, qt_s[c, :, q0:], preferred_element_type=F32)

    def process(kk, st_ref, diagonal_block=False, q0=0):
        vt = vt_s[kk]
        for c in range(2):
            st = st_ref[c, :, q0:]
            if diagonal_block:
                key = lax.broadcasted_iota(jnp.int32, st.shape, 0)
                qry = lax.broadcasted_iota(jnp.int32, st.shape, 1)
                st = jnp.where(key <= qry, st, -jnp.inf)
            m_prev = m_s[c, :, q0:]
            m_new = jnp.maximum(m_prev, jnp.max(st, axis=0, keepdims=True))
            alpha = jnp.exp2(m_prev - m_new)
            p = jnp.exp2(st - m_new)
            l_s[c, :, q0:] = alpha * l_s[c, :, q0:] + jnp.sum(p, axis=0, keepdims=True)
            acc_s[c, :, q0:] = (alpha * acc_s[c, :, q0:]
                                + jnp.dot(vt, p.astype(BF16), preferred_element_type=F32))
            m_s[c, :, q0:] = m_new

    def diagonal(first, cur, nxt):
        for d in range(ratio):
            if d + 1 < ratio:
                scores(first + d + 1, nxt, (d + 1) * tk)
            process(first + d, cur, True, d * tk)
            cur, nxt = nxt, cur

    n_below = qi * ratio
    scores(0, sa_s)

    def pair(i, carry):
        kk = 2 * i
        scores(kk + 1, sb_s)
        process(kk, sa_s)
        scores(kk + 2, sa_s)
        process(kk + 1, sb_s)
        return carry

    lax.fori_loop(0, n_below // 2, pair, 0)
    odd = lax.rem(n_below, 2) == 1

    @pl.when(odd)
    def _():
        scores(n_below, sb_s)
        process(n_below - 1, sa_s)
        diagonal(n_below, sb_s, sa_s)

    @pl.when(jnp.logical_not(odd))
    def _():
        diagonal(n_below, sa_s, sb_s)

    lam = (jnp.exp(jnp.sum(lq1_ref[...] * lk1_ref[...], axis=-1, keepdims=True))
           - jnp.exp(jnp.sum(lq2_ref[...] * lk2_ref[...], axis=-1, keepdims=True)) + lam_init)
    o = (acc_s[0] / l_s[0] - lam * (acc_s[1] / l_s[1])).T
    o = _rms(o) * sw_ref[...] * (1.0 - lam_init)
    o_ref[...] = (o * _sigmoid(ga_ref[...].astype(F32))).astype(o_ref.dtype)


def _attention(qk, plain, lam_vecs, subln_w, lam_init, B, S, tq, tk):
    T = qk.shape[0]
    nq = S // tq
    vec = pl.BlockSpec((1, DA_QK), lambda b, h, i: (0, 0))
    return pl.pallas_call(
        functools.partial(_attn_kernel, lam_init),
        grid=(B, HEADS, nq),
        in_specs=[pl.BlockSpec((tq, LANES), lambda b, h, i: (b * nq + i, h)),
                  pl.BlockSpec((S, LANES), lambda b, h, i: (b, HEADS + h)),
                  pl.BlockSpec((S, LANES), lambda b, h, i: (b, h)),
                  pl.BlockSpec((tq, LANES), lambda b, h, i: (b * nq + i, 3 * HEADS + h)),
                  vec, vec, vec, vec,
                  pl.BlockSpec((1, HEAD_V), lambda b, h, i: (0, 0))],
        out_specs=pl.BlockSpec((tq, LANES), lambda b, h, i: (b * nq + i, h)),
        out_shape=jax.ShapeDtypeStruct((T, D_MODEL), BF16),
        scratch_shapes=[pltpu.VMEM((2, LANES, tq), BF16),
                        pltpu.VMEM((S // tk, HEAD_V, tk), BF16),
                        pltpu.VMEM((2, tk, tq), F32),
                        pltpu.VMEM((2, tk, tq), F32),
                        pltpu.VMEM((2, 1, tq), F32),
                        pltpu.VMEM((2, 1, tq), F32),
                        pltpu.VMEM((2, HEAD_V, tq), F32)],
        compiler_params=_cparams(("parallel", "parallel", "arbitrary")),
        name="attn",
    )(qk, qk, plain, plain, *[v.reshape(1, DA_QK) for v in lam_vecs], subln_w.reshape(1, HEAD_V))


def _mlstm_kernel(qk_ref, v_ref, om_ref, gm_ref, gi_ref, gf_ref, git_ref, gft_ref, nw_ref,
                  o_ref, c_s, n_s, m_s):
    L = qk_ref.shape[0]

    @pl.when(pl.program_id(1) == 0)
    def _():
        c_s[...] = jnp.zeros(c_s.shape, F32)
        n_s[...] = jnp.zeros(n_s.shape, F32)
        m_s[...] = jnp.zeros(m_s.shape, F32)

    row = lax.broadcasted_iota(jnp.int32, (L, L), 0)
    col = lax.broadcasted_iota(jnp.int32, (L, L), 1)
    tri = (col <= row).astype(F32)
    causal_t = row <= col
    tri_t = causal_t.astype(F32)
    bcols = jnp.dot(tri, _log_sigmoid(gf_ref[...]), precision=HIGHEST, preferred_element_type=F32)
    brows = jnp.dot(_log_sigmoid(gft_ref[...]), tri_t, precision=HIGHEST, preferred_element_type=F32)
    ucols = gi_ref[...] - bcols
    git = git_ref[...]
    lane = lax.broadcasted_iota(jnp.int32, (1, LANES), 1)
    first_row = lax.broadcasted_iota(jnp.int32, (SUBLANES, 1), 0) == 0
    nt = (((1,), (1,)), ((), ()))

    for h in range(HEADS):
        p = h // 2
        hmask = ((lane >= (h % 2) * ML_QK) & (lane < (h % 2 + 1) * ML_QK)).astype(F32)
        qh = (qk_ref[:, p * LANES:(p + 1) * LANES].astype(F32) * hmask).astype(BF16)
        kp = qk_ref[:, (HEADS // 2 + p) * LANES:(HEADS // 2 + p + 1) * LANES]
        vt = v_ref[:, h * HEAD_V:(h + 1) * HEAD_V].T
        br = brows[h:h + 1, :]
        igr = git[h:h + 1, :]
        g_tot = br[:, L - 1:L]
        m_prev = m_s[h]
        ct_prev = c_s[h]
        n_prev = n_s[h]

        dm = jnp.where(causal_t, br + ucols[:, h:h + 1], -jnp.inf)
        m_inter = br + m_prev
        m_j = jnp.maximum(jnp.max(dm, axis=0, keepdims=True), m_inter)
        st = lax.dot_general(kp, qh, nt, preferred_element_type=F32)
        qkw = st * jnp.exp(dm - m_j)
        inter_w = jnp.exp(m_inter - m_j)
        num = (jnp.dot(vt, qkw.astype(BF16), preferred_element_type=F32)
               + inter_w * lax.dot_general(ct_prev.astype(BF16), qh, nt, preferred_element_type=F32))
        qn = lax.dot_general(n_prev.astype(BF16), qh, nt, preferred_element_type=F32)[0:1, :]
        den = jnp.sum(qkw, axis=0, keepdims=True) + inter_w * qn
        ht = num / jnp.maximum(jnp.abs(den), jnp.exp(-m_j))

        a = g_tot - br + igr
        m_loc = jnp.max(a, axis=-1, keepdims=True)
        w_loc = jnp.exp(a - m_loc)
        m_new = jnp.maximum(g_tot + m_prev, m_loc)
        dec = jnp.exp(g_tot + m_prev - m_new)
        inc = jnp.exp(m_loc - m_new)
        c_loc = jnp.dot((vt.astype(F32) * w_loc).astype(BF16), kp, preferred_element_type=F32)
        w8 = jnp.where(first_row, w_loc, 0.0)
        n_loc = jnp.dot(w8, kp.astype(F32), precision=HIGHEST, preferred_element_type=F32)
        c_s[h] = dec * ct_prev + inc * c_loc
        n_s[h] = dec * n_prev + inc * n_loc
        m_s[h] = m_new

        sl = slice(h * HEAD_V, (h + 1) * HEAD_V)
        yt = ht * lax.rsqrt(jnp.mean(ht * ht, axis=0, keepdims=True) + NORM_EPS)
        y = yt.T * nw_ref[:, sl]
        y = y * _sigmoid(om_ref[:, sl].astype(F32)) * _sigmoid(gm_ref[:, sl].astype(F32))
        o_ref[:, sl] = y.astype(o_ref.dtype)


def _mlstm(qkc, plain, gi, gf, git, gft, ml_norm_w, B, S):
    T, D = qkc.shape
    L = ML_CHUNK
    nc = S // L
    wide = lambda cb: pl.BlockSpec((L, D), lambda b, c: (b * nc + c, cb))
    return pl.pallas_call(
        _mlstm_kernel,
        grid=(B, nc),
        in_specs=[wide(0), wide(1), wide(2), wide(4),
                  pl.BlockSpec((L, HEADS), lambda b, c: (b * nc + c, 0)),
                  pl.BlockSpec((L, HEADS), lambda b, c: (b * nc + c, 0)),
                  pl.BlockSpec((HEADS, L), lambda b, c: (0, b * nc + c)),
                  pl.BlockSpec((HEADS, L), lambda b, c: (0, b * nc + c)),
                  pl.BlockSpec((1, D), lambda b, c: (0, 0))],
        out_specs=pl.BlockSpec((L, D), lambda b, c: (b * nc + c, 0)),
        out_shape=jax.ShapeDtypeStruct((T, D), BF16),
        scratch_shapes=[pltpu.VMEM((HEADS, LANES, HEAD_V), F32),
                        pltpu.VMEM((HEADS, SUBLANES, LANES), F32),
                        pltpu.VMEM((HEADS, 1, 1), F32)],
        compiler_params=_cparams(("parallel", "arbitrary")),
        name="mlstm",
    )(qkc, plain, plain, plain, gi, gf, git, gft, ml_norm_w.reshape(1, D))


def _outproj_kernel(ya_ref, ym_ref, x_ref, w_ref, gt_ref, nw_ref, sc_ref, sh_ref, x1_ref, h2t_ref):
    merged = (ya_ref[...].astype(F32) + ym_ref[...].astype(F32)).astype(BF16)
    x1 = x_ref[...] + gt_ref[0] * jnp.dot(merged, w_ref[...], preferred_element_type=F32)
    x1_ref[...] = x1
    h2 = _rms(x1) * nw_ref[...] * (1.0 + sc_ref[0]) + sh_ref[0]
    h2t_ref[...] = h2.T.astype(h2t_ref.dtype)


def _outproj(ya, ym, x2, w_out, norm2_w, mod3, B, tps, tm):
    T, D = x2.shape
    row = lambda k: pl.BlockSpec((1, 1, D), lambda i: (k * B + i // tps, 0, 0))
    tile = pl.BlockSpec((tm, D), lambda i: (i, 0))
    return pl.pallas_call(
        _outproj_kernel,
        grid=(T // tm,),
        in_specs=[tile, tile, tile, pl.BlockSpec((D, D), lambda i: (0, 0)), row(2),
                  pl.BlockSpec((1, D), lambda i: (0, 0)), row(4), row(3)],
        out_specs=[tile, pl.BlockSpec((D, tm), lambda i: (0, i))],
        out_shape=[jax.ShapeDtypeStruct((T, D), F32), jax.ShapeDtypeStruct((D, T), BF16)],
        compiler_params=_cparams(("parallel",)),
        name="outproj",
    )(ya, ym, x2, w_out, mod3, norm2_w.reshape(1, D), mod3, mod3)


def _topk_ranks(s):
    n = s.shape[0]
    it = lax.broadcasted_iota(jnp.int32, s.shape, 0)
    rank = jnp.full(s.shape, float(TOPK), F32)
    x = s
    vals = []
    for r in range(TOPK):
        m = jnp.max(x, axis=0, keepdims=True)
        first = jnp.min(jnp.where(x == m, it, n), axis=0, keepdims=True)
        hit = it == first
        x = jnp.where(hit, -jnp.inf, x)
        rank = jnp.where(hit, float(r), rank)
        vals.append(m)
    return rank, vals


MARK_BASE = 3.0e38
MARK_STEP = 1.0e36


def _topk_ranks_distinct(s):
    x = s
    vals = []
    for r in range(TOPK):
        m = jnp.max(x, axis=0, keepdims=True)
        x = jnp.where(x == m, -(MARK_BASE + r * MARK_STEP), x)
        vals.append(m)
    taken = x <= -MARK_BASE
    rank = jnp.where(taken, jnp.round((-x - MARK_BASE) * (1.0 / MARK_STEP)), float(TOPK))
    removed = jnp.sum(jnp.where(taken, 1.0, 0.0), axis=0, keepdims=True)
    return rank, vals, removed == float(TOPK)


def _cand_select(cand, top):
    it = lax.broadcasted_iota(jnp.int32, cand.shape, 0)
    sel = jnp.zeros(cand.shape, F32)
    z = jnp.zeros(top.shape, F32)
    x = cand
    for _ in range(TOPK):
        m = jnp.max(x, axis=0, keepdims=True)
        first = jnp.min(jnp.where(x == m, it, cand.shape[0]), axis=0, keepdims=True)
        hit = it == first
        x = jnp.where(hit, -jnp.inf, x)
        sel = jnp.where(hit, 1.0, sel)
        z = z + jnp.exp(m - top)
    return sel, z


def _cand_select_distinct(cand, top):
    z = jnp.zeros(top.shape, F32)
    x = cand
    for _ in range(TOPK):
        m = jnp.max(x, axis=0, keepdims=True)
        x = jnp.where(x == m, -MARK_BASE, x)
        z = z + jnp.exp(m - top)
    sel = jnp.where(x == -MARK_BASE, 1.0, 0.0)
    return sel, z, jnp.sum(sel, axis=0, keepdims=True) == float(TOPK)


def _selection(s1, s2, exact):
    if exact:
        (rank1, v1), (rank2, v2) = _topk_ranks(s1), _topk_ranks(s2)
    else:
        (rank1, v1, ok1), (rank2, v2, ok2) = _topk_ranks_distinct(s1), _topk_ranks_distinct(s2)
    tt = s1.shape[1]
    rows = [v1[r1] + v2[r2] for (r1, r2) in _CAND]
    rows += [jnp.full((1, tt), -jnp.inf, F32)] * (_CAND_ROWS - len(_CAND))
    cand = jnp.concatenate(rows, axis=0)
    top = v1[0] + v2[0]
    if exact:
        sel, z = _cand_select(cand, top)
        safe = None
    else:
        sel, z, ok3 = _cand_select_distinct(cand, top)
        safe = ok1 & ok2 & ok3
    a = jnp.zeros(s1.shape, F32)
    start = 0
    for r1 in range(TOPK):
        width = sum(1 for c in _CAND if c[0] == r1)
        cnt = jnp.sum(sel[start:start + width, :], axis=0, keepdims=True)
        a = jnp.where(rank1 == float(r1), cnt, a)
        start += width
    return (rank2, jnp.exp(s2 - v2[0]), a, jnp.exp(s1 - v1[0]) / z), safe


SEL_HEADS = 2


def _peer_sel_kernel(h2t_ref, wqt_ref, keys_ref, r2_ref, e2_ref, a_ref, c_ref):
    ht = h2t_ref[...]
    scores = []
    for hp in range(2 * SEL_HEADS):
        qt = jnp.dot(wqt_ref[hp * LANES:(hp + 1) * LANES, :], ht, preferred_element_type=F32)
        scores.append(jnp.dot(keys_ref[hp], qt.astype(BF16), preferred_element_type=F32))

    def write(hh, vals):
        for ref, val in zip((r2_ref, e2_ref, a_ref, c_ref), vals):
            for lt in range(val.shape[1] // LANES):
                ref[lt, hh * N_KEYS:(hh + 1) * N_KEYS, :] = val[:, lt * LANES:(lt + 1) * LANES].astype(ref.dtype)

    unsafe = 0.0
    for hh in range(SEL_HEADS):
        vals, safe = _selection(scores[2 * hh], scores[2 * hh + 1], exact=False)
        write(hh, vals)
        unsafe = unsafe + jnp.sum(jnp.where(safe, 0.0, 1.0))

    @pl.when(unsafe > 0.0)
    def _():
        for hh in range(SEL_HEADS):
            write(hh, _selection(scores[2 * hh], scores[2 * hh + 1], exact=True)[0])


def _peer_sel(h2t, wqt, keys, tt):
    D, T = h2t.shape
    out = pl.BlockSpec((tt // LANES, SEL_HEADS * N_KEYS, LANES), lambda i, h: (i, h, 0))
    shp = lambda dt: jax.ShapeDtypeStruct((T // LANES, HEADS * N_KEYS, LANES), dt)
    return pl.pallas_call(
        _peer_sel_kernel,
        grid=(T // tt, HEADS // SEL_HEADS),
        in_specs=[pl.BlockSpec((D, tt), lambda i, h: (0, i)),
                  pl.BlockSpec((SEL_HEADS * 2 * LANES, D), lambda i, h: (h, 0)),
                  pl.BlockSpec((SEL_HEADS * 2, N_KEYS, LANES), lambda i, h: (h, 0, 0))],
        out_specs=[out, out, out, out],
        out_shape=[shp(BF16), shp(BF16), shp(F32), shp(F32)],
        compiler_params=_cparams(("parallel", "parallel")),
        name="peer_sel",
    )(h2t, wqt, keys)


PEER_ROWS = 128
GATE_GROUP = 4
MXU_COLS = 256
ACT_ROWS = 256
OUT_ROWS = 512


def _gate_rows(row, rows):
    packed = jnp.broadcast_to(row, (2 * SUBLANES, LANES)).astype(BF16)
    return jnp.tile(packed, (rows // (2 * SUBLANES), 1))


def _peer_main_kernel(final, h2t_ref, u_ref, vt_ref, r2_ref, e2_ref, a_ref, c_ref, x1_ref, gt_ref, fw_ref,
                      o_ref, acc_ref, act_ref, p_ref):
    e = pl.program_id(1)
    eb, tt = act_ref.shape
    nj = eb // N_KEYS

    @pl.when(e == 0)
    def _():
        acc_ref[...] = jnp.zeros(acc_ref.shape, F32)

    n_half = tt // MXU_COLS
    tcols = lambda th: slice(th * MXU_COLS, (th + 1) * MXU_COLS)

    def act_piece(th, m):
        ms = slice(m * ACT_ROWS, (m + 1) * ACT_ROWS)
        act_ref[ms, tcols(th)] = jnp.dot(u_ref[ms, :], h2t_ref[:, tcols(th)], preferred_element_type=F32)

    def out_piece(th, r):
        rs = slice(r * OUT_ROWS, (r + 1) * OUT_ROWS)
        acc_ref[rs, tcols(th)] += jnp.dot(vt_ref[rs, :], p_ref[:, tcols(th)], preferred_element_type=F32)

    def gate_chunk(tl, bs):
        ls = slice(tl * LANES, (tl + 1) * LANES)
        for j0 in range(0, nj, GATE_GROUP):
            g = [None] * GATE_GROUP
            for h in range(HEADS):
                rs = slice(h * N_KEYS + bs * PEER_ROWS, h * N_KEYS + (bs + 1) * PEER_ROWS)
                r2c = r2_ref[tl, rs, :]
                e2c = e2_ref[tl, rs, :]
                for jj in range(GATE_GROUP):
                    idx = h * N_KEYS + e * nj + j0 + jj
                    arow = _gate_rows(a_ref[tl, pl.ds(idx, 1), :], PEER_ROWS)
                    crow = _gate_rows(c_ref[tl, pl.ds(idx, 1), :], PEER_ROWS)
                    w = jnp.where(r2c < arow, e2c, jnp.zeros_like(e2c)) * crow
                    g[jj] = w if g[jj] is None else g[jj] + w
            for jj in range(GATE_GROUP):
                es = slice((j0 + jj) * N_KEYS + bs * PEER_ROWS, (j0 + jj) * N_KEYS + (bs + 1) * PEER_ROWS)
                aj = act_ref[es, ls].astype(BF16)
                gelu = 0.5 * aj * (1.0 + lax.erf(aj * (2.0 ** -0.5)))
                p_ref[es, ls] = g[jj] * gelu

    n_act = eb // ACT_ROWS
    n_out = acc_ref.shape[0] // OUT_ROWS
    for m in range(n_act):
        act_piece(0, m)
    for th in range(n_half):
        mxu_work = [functools.partial(act_piece, th + 1, m) for m in range(n_act)] if th + 1 < n_half else []
        if th > 0:
            mxu_work += [functools.partial(out_piece, th - 1, r) for r in range(n_out)]
        chunks = [(th * (MXU_COLS // LANES) + lt, bs)
                  for lt in range(MXU_COLS // LANES) for bs in range(N_KEYS // PEER_ROWS)]
        per_chunk = -(-len(mxu_work) // len(chunks))
        for tl, bs in chunks:
            gate_chunk(tl, bs)
            for piece in mxu_work[:per_chunk]:
                piece()
            mxu_work = mxu_work[per_chunk:]
    for r in range(n_out):
        out_piece(n_half - 1, r)

    @pl.when(e == pl.num_programs(1) - 1)
    def _():
        x2 = x1_ref[...] + gt_ref[0] * acc_ref[...].T
        o_ref[...] = _rms(x2) * fw_ref[...] if final else x2


def _peer_main(h2t, u, vt, sel, x1, mod3, fw, final, B, tps, tt, eb):
    D, T = h2t.shape
    ne = u.shape[0]
    wide = pl.BlockSpec((tt // LANES, HEADS * N_KEYS, LANES), lambda i, e: (i, 0, 0))
    return pl.pallas_call(
        functools.partial(_peer_main_kernel, final),
        grid=(T // tt, ne // eb),
        in_specs=[pl.BlockSpec((D, tt), lambda i, e: (0, i)),
                  pl.BlockSpec((eb, D), lambda i, e: (e, 0)),
                  pl.BlockSpec((D, eb), lambda i, e: (0, e)),
                  wide, wide, wide, wide,
                  pl.BlockSpec((tt, D), lambda i, e: (i, 0)),
                  pl.BlockSpec((1, 1, D), lambda i, e: (5 * B + i // tps, 0, 0)),
                  pl.BlockSpec((1, D), lambda i, e: (0, 0))],
        out_specs=pl.BlockSpec((tt, D), lambda i, e: (i, 0)),
        out_shape=jax.ShapeDtypeStruct((T, D), F32),
        scratch_shapes=[pltpu.VMEM((D, tt), F32),
                        pltpu.VMEM((eb, tt), F32),
                        pltpu.VMEM((eb, tt), BF16)],
        compiler_params=_cparams(("parallel", "arbitrary")),
        name="peer_main",
    )(h2t, u, vt, *sel, x1, mod3, fw.reshape(1, D))


def _rope_tables(S):
    inv = ROPE_THETA ** (-jnp.arange(ROPE_HALF, dtype=F32) * 2.0 / ROPE_DIM)
    ang = jnp.arange(S, dtype=jnp.int32).astype(F32)[:, None] * inv[None, :]
    cos, sin = jnp.cos(ang), jnp.sin(ang)
    zeros = jnp.zeros((S, DA_QK - ROPE_DIM), F32)
    z8 = jnp.zeros((S, ROPE_HALF), F32)
    cosf = jnp.concatenate([cos, cos, zeros + 1.0] * 2, axis=1)
    sa = jnp.concatenate([-sin, z8, zeros] * 2, axis=1)
    sb = jnp.concatenate([z8, sin, zeros] * 2, axis=1)
    return cosf, sa, sb


def kernel(x, c, ada_w, ada_b, norm1_w, norm2_w, w_in, conv_w, conv_b, ml_i_bias, ml_f_bias, ml_norm_w, lam_q1, lam_k1, lam_q2, lam_k2, subln_w, w_out, peer_wq, peer_keys, peer_u, peer_v, final_norm_w):
    B, S, D = x.shape
    assert D == D_MODEL and S % PROJ_TM == 0 and S % ATTN_TQ == 0
    T = B * S
    depth = w_in.shape[0]
    assert depth > 0
    tm = ROW_TILE
    tps = S // tm
    tables = _rope_tables(S)
    x2 = x.reshape(T, D)
    for l in range(depth):
        mod3 = _mod(c, ada_w[l], ada_b[l])
        h = _norm(x2, norm1_w[l], mod3, B, tps, tm)

        wl = w_in[l]
        o = 0
        cols = {}
        for name, width in (("qa", D), ("ka", D), ("va", D), ("qm", D // 2), ("km", D // 2), ("vm", D),
                            ("om", D), ("ip", HEADS), ("fp", HEADS), ("ga", D), ("gm", D)):
            cols[name] = wl[:, o:o + width]
            o += width
        w_rope = jnp.concatenate([cols["qa"], cols["ka"]], axis=1).astype(BF16)
        w_conv = jnp.concatenate([cols["qm"], cols["km"]], axis=1).astype(BF16)
        w_plain = jnp.concatenate([cols[n] for n in ("va", "vm", "om", "ga", "gm")], axis=1).astype(BF16)

        qk = _rope_proj(h, w_rope, tables, S // PROJ_TM, PROJ_TM)
        plain = _proj(h, w_plain, PROJ_TM, D)
        qkc, gi, gf, git, gft = _conv_proj(h, w_conv, cols["ip"].astype(BF16), cols["fp"].astype(BF16),
                                           conv_w[l], conv_b[l], ml_i_bias[l], ml_f_bias[l], tps, tm)

        lam_init = 0.8 - 0.6 * math.exp(-0.3 * l)
        ya = _attention(qk, plain, (lam_q1[l], lam_k1[l], lam_q2[l], lam_k2[l]), subln_w[l], lam_init, B, S, ATTN_TQ, ATTN_TK)
        ym = _mlstm(qkc, plain, gi, gf, git, gft, ml_norm_w[l], B, S)
        x1, h2t = _outproj(ya, ym, x2, w_out[l].astype(BF16), norm2_w[l], mod3, B, tps, tm)

        wqt = peer_wq[l].T.astype(BF16)
        keys = peer_keys[l].reshape(2 * HEADS, N_KEYS, LANES).astype(BF16)
        sel = _peer_sel(h2t, wqt, keys, tm)
        x2 = _peer_main(h2t, peer_u[l].astype(BF16), peer_v[l].T.astype(BF16), sel, x1, mod3,
                        final_norm_w, l == depth - 1, B, tps, tm, PEER_EXPERT_BLOCK)
    return x2.reshape(B, S, D)
```

```python
import functools
import math

import jax
import jax.numpy as jnp
from jax import lax
from jax.experimental import pallas as pl
from jax.experimental.pallas import tpu as pltpu

F32 = jnp.float32
BF16 = jnp.bfloat16
HIGHEST = lax.Precision.HIGHEST

NORM_EPS = 1e-6
LOG2E = 1.4426950408889634
D_MODEL = 1024
HEADS = 8
HEAD_V = 128
DA_QK = 64
ROPE_DIM = 16
ROPE_HALF = 8
ROPE_THETA = 500000.0
ML_QK = 64
CONV_WIDTH = 4
ML_CHUNK = 128
ROW_TILE = 512
PEER_EXPERT_BLOCK = 2048
PROJ_TM = 1024
ATTN_TQ = 2048
ATTN_TK = 512
N_KEYS = 128
TOPK = 16
LANES = 128
SUBLANES = 8
VMEM_LIMIT = 56 * 1024 * 1024

_CAND = [(r1, r2) for r1 in range(TOPK) for r2 in range(TOPK) if (r1 + 1) * (r2 + 1) <= TOPK]
_CAND_ROWS = -(-len(_CAND) // SUBLANES) * SUBLANES


def _cparams(sem):
    return pltpu.CompilerParams(dimension_semantics=sem, vmem_limit_bytes=VMEM_LIMIT)


def _rms(x):
    return x * lax.rsqrt(jnp.mean(x * x, axis=-1, keepdims=True) + NORM_EPS)


def _sigmoid(x):
    return 1.0 / (1.0 + jnp.exp(-x))


def _log_sigmoid(x):
    return jnp.minimum(x, 0.0) - jnp.log(1.0 + jnp.exp(-jnp.abs(x)))


def _mod_kernel(c_ref, w_ref, b_ref, o_ref):
    c = c_ref[...]
    cond = c * _sigmoid(c)
    o_ref[0] = jnp.dot(cond, w_ref[...], precision=HIGHEST, preferred_element_type=F32) + b_ref[...]


def _mod(c, ada_w, ada_b):
    B, D = c.shape
    out = pl.pallas_call(
        _mod_kernel,
        grid=(6,),
        in_specs=[pl.BlockSpec((B, D), lambda j: (0, 0)),
                  pl.BlockSpec((D, D), lambda j: (0, j)),
                  pl.BlockSpec((1, D), lambda j: (0, j))],
        out_specs=pl.BlockSpec((1, B, D), lambda j: (j, 0, 0)),
        out_shape=jax.ShapeDtypeStruct((6, B, D), F32),
        compiler_params=_cparams(("parallel",)),
        name="mod",
    )(c, ada_w, ada_b.reshape(1, 6 * D))
    return out.reshape(6 * B, 1, D)


def _norm_kernel(x_ref, w_ref, sc_ref, sh_ref, o_ref):
    y = _rms(x_ref[...]) * w_ref[...]
    o_ref[...] = (y * (1.0 + sc_ref[0]) + sh_ref[0]).astype(o_ref.dtype)


def _norm(x2, w, mod3, B, tps, tm):
    T, D = x2.shape
    return pl.pallas_call(
        _norm_kernel,
        grid=(T // tm,),
        in_specs=[pl.BlockSpec((tm, D), lambda i: (i, 0)),
                  pl.BlockSpec((1, D), lambda i: (0, 0)),
                  pl.BlockSpec((1, 1, D), lambda i: (1 * B + i // tps, 0, 0)),
                  pl.BlockSpec((1, 1, D), lambda i: (0 * B + i // tps, 0, 0))],
        out_specs=pl.BlockSpec((tm, D), lambda i: (i, 0)),
        out_shape=jax.ShapeDtypeStruct((T, D), BF16),
        compiler_params=_cparams(("parallel",)),
        name="norm1",
    )(x2, w.reshape(1, D), mod3, mod3)


def _proj_kernel(h_ref, w_ref, o_ref):
    o_ref[...] = jnp.dot(h_ref[...], w_ref[...], preferred_element_type=F32).astype(o_ref.dtype)


def _proj(h, w, tm, tn):
    T, D = h.shape
    N = w.shape[1]
    return pl.pallas_call(
        _proj_kernel,
        grid=(N // tn, T // tm),
        in_specs=[pl.BlockSpec((tm, D), lambda j, i: (i, 0)),
                  pl.BlockSpec((D, tn), lambda j, i: (0, j))],
        out_specs=pl.BlockSpec((tm, tn), lambda j, i: (i, j)),
        out_shape=jax.ShapeDtypeStruct((T, N), BF16),
        compiler_params=_cparams(("parallel", "parallel")),
        name="proj_plain",
    )(h, w)


def _rope_proj_kernel(h_ref, w_ref, cos_ref, sa_ref, sb_ref, o_ref):
    acc = jnp.dot(h_ref[...], w_ref[...], preferred_element_type=F32)
    scale = jnp.where(pl.program_id(0) == 0, DA_QK ** -0.5 * LOG2E, 1.0).astype(F32)
    cosf, sa, sb = cos_ref[...], sa_ref[...], sb_ref[...]
    for hh in range(HEADS):
        blk = acc[:, hh * LANES:(hh + 1) * LANES]
        rot = (blk * cosf + pltpu.roll(blk, LANES - ROPE_HALF, 1) * sa
               + pltpu.roll(blk, ROPE_HALF, 1) * sb)
        o_ref[:, hh * LANES:(hh + 1) * LANES] = (rot * scale).astype(o_ref.dtype)


def _rope_proj(h, w, tables, tps, tm):
    T, D = h.shape
    cosf, sa, sb = tables
    tab_spec = pl.BlockSpec((tm, LANES), lambda j, i: (i % tps, 0))
    return pl.pallas_call(
        _rope_proj_kernel,
        grid=(2, T // tm),
        in_specs=[pl.BlockSpec((tm, D), lambda j, i: (i, 0)),
                  pl.BlockSpec((D, D), lambda j, i: (0, j)),
                  tab_spec, tab_spec, tab_spec],
        out_specs=pl.BlockSpec((tm, D), lambda j, i: (i, j)),
        out_shape=jax.ShapeDtypeStruct((T, 2 * D), BF16),
        compiler_params=_cparams(("parallel", "parallel")),
        name="proj_rope",
    )(h, w, cosf, sa, sb)


def _conv_proj_kernel(tps, h_ref, w_ref, wgt_ref, cw_ref, cb_ref, bi_ref, bf_ref, bit_ref, bft_ref,
                      o_ref, gi_ref, gf_ref, git_ref, gft_ref, buf):
    i = pl.program_id(0)
    tm = h_ref.shape[0]
    n = o_ref.shape[1]
    h = h_ref[...]
    full = jnp.dot(h, w_ref[...], preferred_element_type=F32)
    acc = full[:, :n]

    @pl.when(i % tps == 0)
    def _():
        buf[0:SUBLANES, :] = jnp.zeros((SUBLANES, n), F32)

    buf[SUBLANES:SUBLANES + tm, :] = acc
    cw = cw_ref[...]
    y = cb_ref[...]
    for j in range(CONV_WIDTH):
        off = SUBLANES - (CONV_WIDTH - 1) + j
        y = y + buf[off:off + tm, :] * cw[j:j + 1, :]
    buf[0:SUBLANES, :] = buf[tm:tm + SUBLANES, :]
    y = y * _sigmoid(y)
    lane = lax.broadcasted_iota(jnp.int32, (1, n), 1)
    kscale = jnp.where(lane >= HEADS * ML_QK, ML_QK ** -0.5, 1.0).astype(F32)
    o_ref[...] = (y * kscale).astype(o_ref.dtype)

    gates = full[:, n:n + LANES]
    gi_ref[...] = gates[:, :HEADS] + bi_ref[...]
    gf_ref[...] = gates[:, HEADS:2 * HEADS] + bf_ref[...]
    gates_t = lax.dot_general(wgt_ref[...], h, (((1,), (1,)), ((), ())), preferred_element_type=F32)
    git_ref[...] = gates_t[:HEADS] + bit_ref[...]
    gft_ref[...] = gates_t[HEADS:] + bft_ref[...]


def _conv_proj(h, w, wgi, wgf, conv_w, conv_b, bi, bf, tps, tm):
    T, D = h.shape
    N = w.shape[1]
    wg = jnp.concatenate([wgi, wgf], axis=1)
    w_ext = jnp.concatenate([w, jnp.pad(wg, ((0, 0), (0, LANES - 2 * HEADS)))], axis=1)
    full = lambda shape: pl.BlockSpec(shape, lambda i: tuple(0 for _ in shape))
    return pl.pallas_call(
        functools.partial(_conv_proj_kernel, tps),
        grid=(T // tm,),
        in_specs=[pl.BlockSpec((tm, D), lambda i: (i, 0)),
                  full((D, N + LANES)), full((2 * HEADS, D)),
                  full((CONV_WIDTH, N)), full((1, N)),
                  full((1, HEADS)), full((1, HEADS)), full((HEADS, 1)), full((HEADS, 1))],
        out_specs=[pl.BlockSpec((tm, N), lambda i: (i, 0)),
                   pl.BlockSpec((tm, HEADS), lambda i: (i, 0)),
                   pl.BlockSpec((tm, HEADS), lambda i: (i, 0)),
                   pl.BlockSpec((HEADS, tm), lambda i: (0, i)),
                   pl.BlockSpec((HEADS, tm), lambda i: (0, i))],
        out_shape=[jax.ShapeDtypeStruct((T, N), BF16),
                   jax.ShapeDtypeStruct((T, HEADS), F32),
                   jax.ShapeDtypeStruct((T, HEADS), F32),
                   jax.ShapeDtypeStruct((HEADS, T), F32),
                   jax.ShapeDtypeStruct((HEADS, T), F32)],
        scratch_shapes=[pltpu.VMEM((tm + 2 * SUBLANES, N), F32)],
        compiler_params=_cparams(("arbitrary",)),
        name="proj_conv",
    )(h, w_ext, wg.T, conv_w, conv_b.reshape(1, N),
      bi.reshape(1, HEADS), bf.reshape(1, HEADS), bi.reshape(HEADS, 1), bf.reshape(HEADS, 1))


def _attn_kernel(lam_init, q_ref, k_ref, v_ref, ga_ref, lq1_ref, lk1_ref, lq2_ref, lk2_ref, sw_ref,
                 o_ref, qt_s, vt_s, sa_s, sb_s, m_s, l_s, acc_s):
    qi = pl.program_id(2)
    tq = q_ref.shape[0]
    tk = vt_s.shape[2]
    ratio = tq // tk

    @pl.when(qi == 0)
    def _():
        for kk in range(vt_s.shape[0]):
            vt_s[kk] = v_ref[kk * tk:(kk + 1) * tk, :].T

    q = q_ref[...].astype(F32)
    lane = lax.broadcasted_iota(jnp.int32, q.shape, 1)
    qt_s[0] = jnp.where(lane < DA_QK, q, 0.0).T.astype(BF16)
    qt_s[1] = jnp.where(lane >= DA_QK, q, 0.0).T.astype(BF16)
    m_s[...] = jnp.full(m_s.shape, -jnp.inf, F32)
    l_s[...] = jnp.zeros(l_s.shape, F32)
    acc_s[...] = jnp.zeros(acc_s.shape, F32)

    def scores(kk, st_ref, q0=0):
        k = k_ref[pl.ds(pl.multiple_of(kk * tk, tk), tk), :]
        for c in range(2):
            st_ref[c, :, q0:] = jnp.dot(k, qt_s[c, :, q0:], preferred_element_type=F32)

    def process(kk, st_ref, diagonal_block=False, q0=0):
        vt = vt_s[kk]
        for c in range(2):
            st = st_ref[c, :, q0:]
            if diagonal_block:
                key = lax.broadcasted_iota(jnp.int32, st.shape, 0)
                qry = lax.broadcasted_iota(jnp.int32, st.shape, 1)
                st = jnp.where(key <= qry, st, -jnp.inf)
            m_prev = m_s[c, :, q0:]
            m_new = jnp.maximum(m_prev, jnp.max(st, axis=0, keepdims=True))
            alpha = jnp.exp2(m_prev - m_new)
            p = jnp.exp2(st - m_new)
            l_s[c, :, q0:] = alpha * l_s[c, :, q0:] + jnp.sum(p, axis=0, keepdims=True)
            acc_s[c, :, q0:] = (alpha * acc_s[c, :, q0:]
                                + jnp.dot(vt, p.astype(BF16), preferred_element_type=F32))
            m_s[c, :, q0:] = m_new

    def diagonal(first, cur, nxt):
        for d in range(ratio):
            if d + 1 < ratio:
                scores(first + d + 1, nxt, (d + 1) * tk)
            process(first + d, cur, True, d * tk)
            cur, nxt = nxt, cur

    n_below = qi * ratio
    scores(0, sa_s)

    def pair(i, carry):
        kk = 2 * i
        scores(kk + 1, sb_s)
        process(kk, sa_s)
        scores(kk + 2, sa_s)
        process(kk + 1, sb_s)
        return carry

    lax.fori_loop(0, n_below // 2, pair, 0)
    odd = lax.rem(n_below, 2) == 1

    @pl.when(odd)
    def _():
        scores(n_below, sb_s)
        process(n_below - 1, sa_s)
        diagonal(n_below, sb_s, sa_s)

    @pl.when(jnp.logical_not(odd))
    def _():
        diagonal(n_below, sa_s, sb_s)

    lam = (jnp.exp(jnp.sum(lq1_ref[...] * lk1_ref[...], axis=-1, keepdims=True))
           - jnp.exp(jnp.sum(lq2_ref[...] * lk2_ref[...], axis=-1, keepdims=True)) + lam_init)
    o = (acc_s[0] / l_s[0] - lam * (acc_s[1] / l_s[1])).T
    o = _rms(o) * sw_ref[...] * (1.0 - lam_init)
    o_ref[...] = (o * _sigmoid(ga_ref[...].astype(F32))).astype(o_ref.dtype)


def _attention(qk, plain, lam_vecs, subln_w, lam_init, B, S, tq, tk):
    T = qk.shape[0]
    nq = S // tq
    vec = pl.BlockSpec((1, DA_QK), lambda b, h, i: (0, 0))
    return pl.pallas_call(
        functools.partial(_attn_kernel, lam_init),
        grid=(B, HEADS, nq),
        in_specs=[pl.BlockSpec((tq, LANES), lambda b, h, i: (b * nq + i, h)),
                  pl.BlockSpec((S, LANES), lambda b, h, i: (b, HEADS + h)),
                  pl.BlockSpec((S, LANES), lambda b, h, i: (b, h)),
                  pl.BlockSpec((tq, LANES), lambda b, h, i: (b * nq + i, 3 * HEADS + h)),
                  vec, vec, vec, vec,
                  pl.BlockSpec((1, HEAD_V), lambda b, h, i: (0, 0))],
        out_specs=pl.BlockSpec((tq, LANES), lambda b, h, i: (b * nq + i, h)),
        out_shape=jax.ShapeDtypeStruct((T, D_MODEL), BF16),
        scratch_shapes=[pltpu.VMEM((2, LANES, tq), BF16),
                        pltpu.VMEM((S // tk, HEAD_V, tk), BF16),
                        pltpu.VMEM((2, tk, tq), F32),
                        pltpu.VMEM((2, tk, tq), F32),
                        pltpu.VMEM((2, 1, tq), F32),
                        pltpu.VMEM((2, 1, tq), F32),
                        pltpu.VMEM((2, HEAD_V, tq), F32)],
        compiler_params=_cparams(("parallel", "parallel", "arbitrary")),
        name="attn",
    )(qk, qk, plain, plain, *[v.reshape(1, DA_QK) for v in lam_vecs], subln_w.reshape(1, HEAD_V))


def _mlstm_kernel(qk_ref, v_ref, om_ref, gm_ref, gi_ref, gf_ref, git_ref, gft_ref, nw_ref,
                  o_ref, c_s, n_s, m_s):
    L = qk_ref.shape[0]

    @pl.when(pl.program_id(1) == 0)
    def _():
        c_s[...] = jnp.zeros(c_s.shape, F32)
        n_s[...] = jnp.zeros(n_s.shape, F32)
        m_s[...] = jnp.zeros(m_s.shape, F32)

    row = lax.broadcasted_iota(jnp.int32, (L, L), 0)
    col = lax.broadcasted_iota(jnp.int32, (L, L), 1)
    tri = (col <= row).astype(F32)
    causal_t = row <= col
    tri_t = causal_t.astype(F32)
    bcols = jnp.dot(tri, _log_sigmoid(gf_ref[...]), precision=HIGHEST, preferred_element_type=F32)
    brows = jnp.dot(_log_sigmoid(gft_ref[...]), tri_t, precision=HIGHEST, preferred_element_type=F32)
    ucols = gi_ref[...] - bcols
    git = git_ref[...]
    lane = lax.broadcasted_iota(jnp.int32, (1, LANES), 1)
    first_row = lax.broadcasted_iota(jnp.int32, (SUBLANES, 1), 0) == 0
    nt = (((1,), (1,)), ((), ()))

    for h in range(HEADS):
        p = h // 2
        hmask = ((lane >= (h % 2) * ML_QK) & (lane < (h % 2 + 1) * ML_QK)).astype(F32)
        qh = (qk_ref[:, p * LANES:(p + 1) * LANES].astype(F32) * hmask).astype(BF16)
        kp = qk_ref[:, (HEADS // 2 + p) * LANES:(HEADS // 2 + p + 1) * LANES]
        vt = v_ref[:, h * HEAD_V:(h + 1) * HEAD_V].T
        br = brows[h:h + 1, :]
        igr = git[h:h + 1, :]
        g_tot = br[:, L - 1:L]
        m_prev = m_s[h]
        ct_prev = c_s[h]
        n_prev = n_s[h]

        dm = jnp.where(causal_t, br + ucols[:, h:h + 1], -jnp.inf)
        m_inter = br + m_prev
        m_j = jnp.maximum(jnp.max(dm, axis=0, keepdims=True), m_inter)
        st = lax.dot_general(kp, qh, nt, preferred_element_type=F32)
        qkw = st * jnp.exp(dm - m_j)
        inter_w = jnp.exp(m_inter - m_j)
        num = (jnp.dot(vt, qkw.astype(BF16), preferred_element_type=F32)
               + inter_w * lax.dot_general(ct_prev.astype(BF16), qh, nt, preferred_element_type=F32))
        qn = lax.dot_general(n_prev.astype(BF16), qh, nt, preferred_element_type=F32)[0:1, :]
        den = jnp.sum(qkw, axis=0, keepdims=True) + inter_w * qn
        ht = num / jnp.maximum(jnp.abs(den), jnp.exp(-m_j))

        a = g_tot - br + igr
        m_loc = jnp.max(a, axis=-1, keepdims=True)
        w_loc = jnp.exp(a - m_loc)
        m_new = jnp.maximum(g_tot + m_prev, m_loc)
        dec = jnp.exp(g_tot + m_prev - m_new)
        inc = jnp.exp(m_loc - m_new)
        c_loc = jnp.dot((vt.astype(F32) * w_loc).astype(BF16), kp, preferred_element_type=F32)
        w8 = jnp.where(first_row, w_loc, 0.0)
        n_loc = jnp.dot(w8, kp.astype(F32), precision=HIGHEST, preferred_element_type=F32)
        c_s[h] = dec * ct_prev + inc * c_loc
        n_s[h] = dec * n_prev + inc * n_loc
        m_s[h] = m_new

        sl = slice(h * HEAD_V, (h + 1) * HEAD_V)
        yt = ht * lax.rsqrt(jnp.mean(ht * ht, axis=0, keepdims=True) + NORM_EPS)
        y = yt.T * nw_ref[:, sl]
        y = y * _sigmoid(om_ref[:, sl].astype(F32)) * _sigmoid(gm_ref[:, sl].astype(F32))
        o_ref[:, sl] = y.astype(o_ref.dtype)


def _mlstm(qkc, plain, gi, gf, git, gft, ml_norm_w, B, S):
    T, D = qkc.shape
    L = ML_CHUNK
    nc = S // L
    wide = lambda cb: pl.BlockSpec((L, D), lambda b, c: (b * nc + c, cb))
    return pl.pallas_call(
        _mlstm_kernel,
        grid=(B, nc),
        in_specs=[wide(0), wide(1), wide(2), wide(4),
                  pl.BlockSpec((L, HEADS), lambda b, c: (b * nc + c, 0)),
                  pl.BlockSpec((L, HEADS), lambda b, c: (b * nc + c, 0)),
                  pl.BlockSpec((HEADS, L), lambda b, c: (0, b * nc + c)),
                  pl.BlockSpec((HEADS, L), lambda b, c: (0, b * nc + c)),
                  pl.BlockSpec((1, D), lambda b, c: (0, 0))],
        out_specs=pl.BlockSpec((L, D), lambda b, c: (b * nc + c, 0)),
        out_shape=jax.ShapeDtypeStruct((T, D), BF16),
        scratch_shapes=[pltpu.VMEM((HEADS, LANES, HEAD_V), F32),
                        pltpu.VMEM((HEADS, SUBLANES, LANES), F32),
                        pltpu.VMEM((HEADS, 1, 1), F32)],
        compiler_params=_cparams(("parallel", "arbitrary")),
        name="mlstm",
    )(qkc, plain, plain, plain, gi, gf, git, gft, ml_norm_w.reshape(1, D))


def _outproj_kernel(ya_ref, ym_ref, x_ref, w_ref, gt_ref, nw_ref, sc_ref, sh_ref, x1_ref, h2t_ref):
    merged = (ya_ref[...].astype(F32) + ym_ref[...].astype(F32)).astype(BF16)
    x1 = x_ref[...] + gt_ref[0] * jnp.dot(merged, w_ref[...], preferred_element_type=F32)
    x1_ref[...] = x1
    h2 = _rms(x1) * nw_ref[...] * (1.0 + sc_ref[0]) + sh_ref[0]
    h2t_ref[...] = h2.T.astype(h2t_ref.dtype)


def _outproj(ya, ym, x2, w_out, norm2_w, mod3, B, tps, tm):
    T, D = x2.shape
    row = lambda k: pl.BlockSpec((1, 1, D), lambda i: (k * B + i // tps, 0, 0))
    tile = pl.BlockSpec((tm, D), lambda i: (i, 0))
    return pl.pallas_call(
        _outproj_kernel,
        grid=(T // tm,),
        in_specs=[tile, tile, tile, pl.BlockSpec((D, D), lambda i: (0, 0)), row(2),
                  pl.BlockSpec((1, D), lambda i: (0, 0)), row(4), row(3)],
        out_specs=[tile, pl.BlockSpec((D, tm), lambda i: (0, i))],
        out_shape=[jax.ShapeDtypeStruct((T, D), F32), jax.ShapeDtypeStruct((D, T), BF16)],
        compiler_params=_cparams(("parallel",)),
        name="outproj",
    )(ya, ym, x2, w_out, mod3, norm2_w.reshape(1, D), mod3, mod3)


def _topk_ranks(s):
    n = s.shape[0]
    it = lax.broadcasted_iota(jnp.int32, s.shape, 0)
    rank = jnp.full(s.shape, float(TOPK), F32)
    x = s
    vals = []
    for r in range(TOPK):
        m = jnp.max(x, axis=0, keepdims=True)
        first = jnp.min(jnp.where(x == m, it, n), axis=0, keepdims=True)
        hit = it == first
        x = jnp.where(hit, -jnp.inf, x)
        rank = jnp.where(hit, float(r), rank)
        vals.append(m)
    return rank, vals


MARK_BASE = 3.0e38
MARK_STEP = 1.0e36


def _topk_ranks_distinct(s):
    x = s
    vals = []
    for r in range(TOPK):
        m = jnp.max(x, axis=0, keepdims=True)
        x = jnp.where(x == m, -(MARK_BASE + r * MARK_STEP), x)
        vals.append(m)
    taken = x <= -MARK_BASE
    rank = jnp.where(taken, jnp.round((-x - MARK_BASE) * (1.0 / MARK_STEP)), float(TOPK))
    removed = jnp.sum(jnp.where(taken, 1.0, 0.0), axis=0, keepdims=True)
    return rank, vals, removed == float(TOPK)


def _cand_select(cand, top):
    it = lax.broadcasted_iota(jnp.int32, cand.shape, 0)
    sel = jnp.zeros(cand.shape, F32)
    z = jnp.zeros(top.shape, F32)
    x = cand
    for _ in range(TOPK):
        m = jnp.max(x, axis=0, keepdims=True)
        first = jnp.min(jnp.where(x == m, it, cand.shape[0]), axis=0, keepdims=True)
        hit = it == first
        x = jnp.where(hit, -jnp.inf, x)
        sel = jnp.where(hit, 1.0, sel)
        z = z + jnp.exp(m - top)
    return sel, z


def _cand_select_distinct(cand, top):
    z = jnp.zeros(top.shape, F32)
    x = cand
    for _ in range(TOPK):
        m = jnp.max(x, axis=0, keepdims=True)
        x = jnp.where(x == m, -MARK_BASE, x)
        z = z + jnp.exp(m - top)
    sel = jnp.where(x == -MARK_BASE, 1.0, 0.0)
    return sel, z, jnp.sum(sel, axis=0, keepdims=True) == float(TOPK)


def _selection(s1, s2, exact):
    if exact:
        (rank1, v1), (rank2, v2) = _topk_ranks(s1), _topk_ranks(s2)
    else:
        (rank1, v1, ok1), (rank2, v2, ok2) = _topk_ranks_distinct(s1), _topk_ranks_distinct(s2)
    tt = s1.shape[1]
    rows = [v1[r1] + v2[r2] for (r1, r2) in _CAND]
    rows += [jnp.full((1, tt), -jnp.inf, F32)] * (_CAND_ROWS - len(_CAND))
    cand = jnp.concatenate(rows, axis=0)
    top = v1[0] + v2[0]
    if exact:
        sel, z = _cand_select(cand, top)
        safe = None
    else:
        sel, z, ok3 = _cand_select_distinct(cand, top)
        safe = ok1 & ok2 & ok3
    a = jnp.zeros(s1.shape, F32)
    start = 0
    for r1 in range(TOPK):
        width = sum(1 for c in _CAND if c[0] == r1)
        cnt = jnp.sum(sel[start:start + width, :], axis=0, keepdims=True)
        a = jnp.where(rank1 == float(r1), cnt, a)
        start += width
    return (rank2, jnp.exp(s2 - v2[0]), a, jnp.exp(s1 - v1[0]) / z), safe


SEL_HEADS = 2


def _peer_sel_kernel(h2t_ref, wqt_ref, keys_ref, r2_ref, e2_ref, a_ref, c_ref):
    ht = h2t_ref[...]
    scores = []
    for hp in range(2 * SEL_HEADS):
        qt = jnp.dot(wqt_ref[hp * LANES:(hp + 1) * LANES, :], ht, preferred_element_type=F32)
        scores.append(jnp.dot(keys_ref[hp], qt.astype(BF16), preferred_element_type=F32))

    def write(hh, vals):
        for ref, val in zip((r2_ref, e2_ref, a_ref, c_ref), vals):
            for lt in range(val.shape[1] // LANES):
                ref[lt, hh * N_KEYS:(hh + 1) * N_KEYS, :] = val[:, lt * LANES:(lt + 1) * LANES].astype(ref.dtype)

    unsafe = 0.0
    for hh in range(SEL_HEADS):
        vals, safe = _selection(scores[2 * hh], scores[2 * hh + 1], exact=False)
        write(hh, vals)
        unsafe = unsafe + jnp.sum(jnp.where(safe, 0.0, 1.0))

    @pl.when(unsafe > 0.0)
    def _():
        for hh in range(SEL_HEADS):
            write(hh, _selection(scores[2 * hh], scores[2 * hh + 1], exact=True)[0])


def _peer_sel(h2t, wqt, keys, tt):
    D, T = h2t.shape
    out = pl.BlockSpec((tt // LANES, SEL_HEADS * N_KEYS, LANES), lambda i, h: (i, h, 0))
    shp = lambda dt: jax.ShapeDtypeStruct((T // LANES, HEADS * N_KEYS, LANES), dt)
    return pl.pallas_call(
        _peer_sel_kernel,
        grid=(T // tt, HEADS // SEL_HEADS),
        in_specs=[pl.BlockSpec((D, tt), lambda i, h: (0, i)),
                  pl.BlockSpec((SEL_HEADS * 2 * LANES, D), lambda i, h: (h, 0)),
                  pl.BlockSpec((SEL_HEADS * 2, N_KEYS, LANES), lambda i, h: (h, 0, 0))],
        out_specs=[out, out, out, out],
        out_shape=[shp(BF16), shp(BF16), shp(F32), shp(F32)],
        compiler_params=_cparams(("parallel", "parallel")),
        name="peer_sel",
    )(h2t, wqt, keys)


PEER_ROWS = 128
GATE_GROUP = 4
MXU_COLS = 256
ACT_ROWS = 256
OUT_ROWS = 512


def _gate_rows(row, rows):
    packed = jnp.broadcast_to(row, (2 * SUBLANES, LANES)).astype(BF16)
    return jnp.tile(packed, (rows // (2 * SUBLANES), 1))


def _peer_main_kernel(final, h2t_ref, u_ref, vt_ref, r2_ref, e2_ref, a_ref, c_ref, x1_ref, gt_ref, fw_ref,
                      o_ref, acc_ref, act_ref, p_ref):
    e = pl.program_id(1)
    eb, tt = act_ref.shape
    nj = eb // N_KEYS

    @pl.when(e == 0)
    def _():
        acc_ref[...] = jnp.zeros(acc_ref.shape, F32)

    n_half = tt // MXU_COLS
    tcols = lambda th: slice(th * MXU_COLS, (th + 1) * MXU_COLS)

    def act_piece(th, m):
        ms = slice(m * ACT_ROWS, (m + 1) * ACT_ROWS)
        act_ref[ms, tcols(th)] = jnp.dot(u_ref[ms, :], h2t_ref[:, tcols(th)], preferred_element_type=F32)

    def out_piece(th, r):
        rs = slice(r * OUT_ROWS, (r + 1) * OUT_ROWS)
        acc_ref[rs, tcols(th)] += jnp.dot(vt_ref[rs, :], p_ref[:, tcols(th)], preferred_element_type=F32)

    def gate_chunk(tl, bs):
        ls = slice(tl * LANES, (tl + 1) * LANES)
        for j0 in range(0, nj, GATE_GROUP):
            g = [None] * GATE_GROUP
            for h in range(HEADS):
                rs = slice(h * N_KEYS + bs * PEER_ROWS, h * N_KEYS + (bs + 1) * PEER_ROWS)
                r2c = r2_ref[tl, rs, :]
                e2c = e2_ref[tl, rs, :]
                for jj in range(GATE_GROUP):
                    idx = h * N_KEYS + e * nj + j0 + jj
                    arow = _gate_rows(a_ref[tl, pl.ds(idx, 1), :], PEER_ROWS)
                    crow = _gate_rows(c_ref[tl, pl.ds(idx, 1), :], PEER_ROWS)
                    w = jnp.where(r2c < arow, e2c, jnp.zeros_like(e2c)) * crow
                    g[jj] = w if g[jj] is None else g[jj] + w
            for jj in range(GATE_GROUP):
                es = slice((j0 + jj) * N_KEYS + bs * PEER_ROWS, (j0 + jj) * N_KEYS + (bs + 1) * PEER_ROWS)
                aj = act_ref[es, ls].astype(BF16)
                gelu = 0.5 * aj * (1.0 + lax.erf(aj * (2.0 ** -0.5)))
                p_ref[es, ls] = g[jj] * gelu

    n_act = eb // ACT_ROWS
    n_out = acc_ref.shape[0] // OUT_ROWS
    for m in range(n_act):
        act_piece(0, m)
    for th in range(n_half):
        mxu_work = [functools.partial(act_piece, th + 1, m) for m in range(n_act)] if th + 1 < n_half else []
        if th > 0:
            mxu_work += [functools.partial(out_piece, th - 1, r) for r in range(n_out)]
        chunks = [(th * (MXU_COLS // LANES) + lt, bs)
                  for lt in range(MXU_COLS // LANES) for bs in range(N_KEYS // PEER_ROWS)]
        per_chunk = -(-len(mxu_work) // len(chunks))
        for tl, bs in chunks:
            gate_chunk(tl, bs)
            for piece in mxu_work[:per_chunk]:
                piece()
            mxu_work = mxu_work[per_chunk:]
    for r in range(n_out):
        out_piece(n_half - 1, r)

    @pl.when(e == pl.num_programs(1) - 1)
    def _():
        x2 = x1_ref[...] + gt_ref[0] * acc_ref[...].T
        o_ref[...] = _rms(x2) * fw_ref[...] if final else x2


def _peer_main(h2t, u, vt, sel, x1, mod3, fw, final, B, tps, tt, eb):
    D, T = h2t.shape
    ne = u.shape[0]
    wide = pl.BlockSpec((tt // LANES, HEADS * N_KEYS, LANES), lambda i, e: (i, 0, 0))
    return pl.pallas_call(
        functools.partial(_peer_main_kernel, final),
        grid=(T // tt, ne // eb),
        in_specs=[pl.BlockSpec((D, tt), lambda i, e: (0, i)),
                  pl.BlockSpec((eb, D), lambda i, e: (e, 0)),
                  pl.BlockSpec((D, eb), lambda i, e: (0, e)),
                  wide, wide, wide, wide,
                  pl.BlockSpec((tt, D), lambda i, e: (i, 0)),
                  pl.BlockSpec((1, 1, D), lambda i, e: (5 * B + i // tps, 0, 0)),
                  pl.BlockSpec((1, D), lambda i, e: (0, 0))],
        out_specs=pl.BlockSpec((tt, D), lambda i, e: (i, 0)),
        out_shape=jax.ShapeDtypeStruct((T, D), F32),
        scratch_shapes=[pltpu.VMEM((D, tt), F32),
                        pltpu.VMEM((eb, tt), F32),
                        pltpu.VMEM((eb, tt), BF16)],
        compiler_params=_cparams(("parallel", "arbitrary")),
        name="peer_main",
    )(h2t, u, vt, *sel, x1, mod3, fw.reshape(1, D))


def _rope_tables(S):
    inv = ROPE_THETA ** (-jnp.arange(ROPE_HALF, dtype=F32) * 2.0 / ROPE_DIM)
    ang = jnp.arange(S, dtype=jnp.int32).astype(F32)[:, None] * inv[None, :]
    cos, sin = jnp.cos(ang), jnp.sin(ang)
    zeros = jnp.zeros((S, DA_QK - ROPE_DIM), F32)
    z8 = jnp.zeros((S, ROPE_HALF), F32)
    cosf = jnp.concatenate([cos, cos, zeros + 1.0] * 2, axis=1)
    sa = jnp.concatenate([-sin, z8, zeros] * 2, axis=1)
    sb = jnp.concatenate([z8, sin, zeros] * 2, axis=1)
    return cosf, sa, sb


def kernel(x, c, ada_w, ada_b, norm1_w, norm2_w, w_in, conv_w, conv_b, ml_i_bias, ml_f_bias, ml_norm_w, lam_q1, lam_k1, lam_q2, lam_k2, subln_w, w_out, peer_wq, peer_keys, peer_u, peer_v, final_norm_w):
    B, S, D = x.shape
    assert D == D_MODEL and S % PROJ_TM == 0 and S % ATTN_TQ == 0
    T = B * S
    depth = w_in.shape[0]
    assert depth > 0
    tm = ROW_TILE
    tps = S // tm
    tables = _rope_tables(S)
    x2 = x.reshape(T, D)
    for l in range(depth):
        mod3 = _mod(c, ada_w[l], ada_b[l])
        h = _norm(x2, norm1_w[l], mod3, B, tps, tm)

        wl = w_in[l]
        o = 0
        cols = {}
        for name, width in (("qa", D), ("ka", D), ("va", D), ("qm", D // 2), ("km", D // 2), ("vm", D),
                            ("om", D), ("ip", HEADS), ("fp", HEADS), ("ga", D), ("gm", D)):
            cols[name] = wl[:, o:o + width]
            o += width
        w_rope = jnp.concatenate([cols["qa"], cols["ka"]], axis=1).astype(BF16)
        w_conv = jnp.concatenate([cols["qm"], cols["km"]], axis=1).astype(BF16)
        w_plain = jnp.concatenate([cols[n] for n in ("va", "vm", "om", "ga", "gm")], axis=1).astype(BF16)

        qk = _rope_proj(h, w_rope, tables, S // PROJ_TM, PROJ_TM)
        plain = _proj(h, w_plain, PROJ_TM, D)
        qkc, gi, gf, git, gft = _conv_proj(h, w_conv, cols["ip"].astype(BF16), cols["fp"].astype(BF16),
                                           conv_w[l], conv_b[l], ml_i_bias[l], ml_f_bias[l], tps, tm)

        lam_init = 0.8 - 0.6 * math.exp(-0.3 * l)
        ya = _attention(qk, plain, (lam_q1[l], lam_k1[l], lam_q2[l], lam_k2[l]), subln_w[l], lam_init, B, S, ATTN_TQ, ATTN_TK)
        ym = _mlstm(qkc, plain, gi, gf, git, gft, ml_norm_w[l], B, S)
        x1, h2t = _outproj(ya, ym, x2, w_out[l].astype(BF16), norm2_w[l], mod3, B, tps, tm)

        wqt = peer_wq[l].T.astype(BF16)
        keys = peer_keys[l].reshape(2 * HEADS, N_KEYS, LANES).astype(BF16)
        sel = _peer_sel(h2t, wqt, keys, tm)
        x2 = _peer_main(h2t, peer_u[l].astype(BF16), peer_v[l].T.astype(BF16), sel, x1, mod3,
                        final_norm_w, l == depth - 1, B, tps, tm, PEER_EXPERT_BLOCK)
    return x2.reshape(B, S, D)
```

```python
import functools
import math

import jax
import jax.numpy as jnp
from jax import lax
from jax.experimental import pallas as pl
from jax.experimental.pallas import tpu as pltpu

F32 = jnp.float32
BF16 = jnp.bfloat16
HIGHEST = lax.Precision.HIGHEST

NORM_EPS = 1e-6
LOG2E = 1.4426950408889634
D_MODEL = 1024
HEADS = 8
HEAD_V = 128
DA_QK = 64
ROPE_DIM = 16
ROPE_HALF = 8
ROPE_THETA = 500000.0
ML_QK = 64
CONV_WIDTH = 4
ML_CHUNK = 128
ROW_TILE = 512
PEER_EXPERT_BLOCK = 2048
PROJ_TM = 1024
ATTN_TQ = 2048
ATTN_TK = 512
N_KEYS = 128
TOPK = 16
LANES = 128
SUBLANES = 8
VMEM_LIMIT = 56 * 1024 * 1024

_CAND = [(r1, r2) for r1 in range(TOPK) for r2 in range(TOPK) if (r1 + 1) * (r2 + 1) <= TOPK]
_CAND_ROWS = -(-len(_CAND) // SUBLANES) * SUBLANES


def _cparams(sem):
    return pltpu.CompilerParams(dimension_semantics=sem, vmem_limit_bytes=VMEM_LIMIT)


def _rms(x):
    return x * lax.rsqrt(jnp.mean(x * x, axis=-1, keepdims=True) + NORM_EPS)


def _sigmoid(x):
    return 1.0 / (1.0 + jnp.exp(-x))


def _log_sigmoid(x):
    return jnp.minimum(x, 0.0) - jnp.log(1.0 + jnp.exp(-jnp.abs(x)))


def _mod_kernel(c_ref, w_ref, b_ref, o_ref):
    c = c_ref[...]
    cond = c * _sigmoid(c)
    o_ref[0] = jnp.dot(cond, w_ref[...], precision=HIGHEST, preferred_element_type=F32) + b_ref[...]


def _mod(c, ada_w, ada_b):
    B, D = c.shape
    out = pl.pallas_call(
        _mod_kernel,
        grid=(6,),
        in_specs=[pl.BlockSpec((B, D), lambda j: (0, 0)),
                  pl.BlockSpec((D, D), lambda j: (0, j)),
                  pl.BlockSpec((1, D), lambda j: (0, j))],
        out_specs=pl.BlockSpec((1, B, D), lambda j: (j, 0, 0)),
        out_shape=jax.ShapeDtypeStruct((6, B, D), F32),
        compiler_params=_cparams(("parallel",)),
        name="mod",
    )(c, ada_w, ada_b.reshape(1, 6 * D))
    return out.reshape(6 * B, 1, D)


def _norm_kernel(x_ref, w_ref, sc_ref, sh_ref, o_ref):
    y = _rms(x_ref[...]) * w_ref[...]
    o_ref[...] = (y * (1.0 + sc_ref[0]) + sh_ref[0]).astype(o_ref.dtype)


def _norm(x2, w, mod3, B, tps, tm):
    T, D = x2.shape
    return pl.pallas_call(
        _norm_kernel,
        grid=(T // tm,),
        in_specs=[pl.BlockSpec((tm, D), lambda i: (i, 0)),
                  pl.BlockSpec((1, D), lambda i: (0, 0)),
                  pl.BlockSpec((1, 1, D), lambda i: (1 * B + i // tps, 0, 0)),
                  pl.BlockSpec((1, 1, D), lambda i: (0 * B + i // tps, 0, 0))],
        out_specs=pl.BlockSpec((tm, D), lambda i: (i, 0)),
        out_shape=jax.ShapeDtypeStruct((T, D), BF16),
        compiler_params=_cparams(("parallel",)),
        name="norm1",
    )(x2, w.reshape(1, D), mod3, mod3)


def _modnorm(x_ref, nw_ref, sc_ref, sh_ref):
    return (_rms(x_ref[...]) * nw_ref[...] * (1.0 + sc_ref[0]) + sh_ref[0]).astype(BF16)


def _norm_specs(D, B, tps, row_of):
    return [
        lambda tm: pl.BlockSpec((tm, D), lambda *g: (row_of(*g), 0)),
        pl.BlockSpec((1, D), lambda *g: (0, 0)),
        pl.BlockSpec((1, 1, D), lambda *g: (1 * B + row_of(*g) // tps, 0, 0)),
        pl.BlockSpec((1, 1, D), lambda *g: (0 * B + row_of(*g) // tps, 0, 0))]


def _proj_kernel(x_ref, nw_ref, sc_ref, sh_ref, w_ref, o_ref):
    h = _modnorm(x_ref, nw_ref, sc_ref, sh_ref)
    o_ref[...] = jnp.dot(h, w_ref[...], preferred_element_type=F32).astype(o_ref.dtype)


def _proj(x2, norm, w, tm, tn):
    T, D = x2.shape
    N = w.shape[1]
    nw, mod3, B, tps = norm
    xs, *rest = _norm_specs(D, B, tps, lambda j, i: i)
    return pl.pallas_call(
        _proj_kernel,
        grid=(N // tn, T // tm),
        in_specs=[xs(tm), *rest, pl.BlockSpec((D, tn), lambda j, i: (0, j))],
        out_specs=pl.BlockSpec((tm, tn), lambda j, i: (i, j)),
        out_shape=jax.ShapeDtypeStruct((T, N), BF16),
        compiler_params=_cparams(("parallel", "parallel")),
        name="proj_plain",
    )(x2, nw.reshape(1, D), mod3, mod3, w)


def _rope_proj_kernel(x_ref, nw_ref, sc_ref, sh_ref, w_ref, cos_ref, sa_ref, sb_ref, o_ref):
    acc = jnp.dot(_modnorm(x_ref, nw_ref, sc_ref, sh_ref), w_ref[...], preferred_element_type=F32)
    scale = jnp.where(pl.program_id(0) == 0, DA_QK ** -0.5 * LOG2E, 1.0).astype(F32)
    cosf, sa, sb = cos_ref[...], sa_ref[...], sb_ref[...]
    for hh in range(HEADS):
        blk = acc[:, hh * LANES:(hh + 1) * LANES]
        rot = (blk * cosf + pltpu.roll(blk, LANES - ROPE_HALF, 1) * sa
               + pltpu.roll(blk, ROPE_HALF, 1) * sb)
        o_ref[:, hh * LANES:(hh + 1) * LANES] = (rot * scale).astype(o_ref.dtype)


def _rope_proj(x2, norm, w, tables, tm):
    T, D = x2.shape
    cosf, sa, sb = tables
    nw, mod3, B, tps = norm
    xs, *rest = _norm_specs(D, B, tps, lambda j, i: i)
    tab_spec = pl.BlockSpec((tm, LANES), lambda j, i: (i % tps, 0))
    return pl.pallas_call(
        _rope_proj_kernel,
        grid=(2, T // tm),
        in_specs=[xs(tm), *rest,
                  pl.BlockSpec((D, D), lambda j, i: (0, j)),
                  tab_spec, tab_spec, tab_spec],
        out_specs=pl.BlockSpec((tm, D), lambda j, i: (i, j)),
        out_shape=jax.ShapeDtypeStruct((T, 2 * D), BF16),
        compiler_params=_cparams(("parallel", "parallel")),
        name="proj_rope",
    )(x2, nw.reshape(1, D), mod3, mod3, w, cosf, sa, sb)


def _conv_proj_kernel(tps, x_ref, nw_ref, sc_ref, sh_ref, w_ref, wgt_ref, cw_ref, cb_ref,
                      bi_ref, bf_ref, bit_ref, bft_ref,
                      o_ref, gi_ref, gf_ref, git_ref, gft_ref, buf):
    i = pl.program_id(0)
    tm = x_ref.shape[0]
    n = o_ref.shape[1]
    h = _modnorm(x_ref, nw_ref, sc_ref, sh_ref)
    full = jnp.dot(h, w_ref[...], preferred_element_type=F32)
    acc = full[:, :n]

    @pl.when(i % tps == 0)
    def _():
        buf[0:SUBLANES, :] = jnp.zeros((SUBLANES, n), F32)

    buf[SUBLANES:SUBLANES + tm, :] = acc
    cw = cw_ref[...]
    y = cb_ref[...]
    for j in range(CONV_WIDTH):
        off = SUBLANES - (CONV_WIDTH - 1) + j
        y = y + buf[off:off + tm, :] * cw[j:j + 1, :]
    buf[0:SUBLANES, :] = buf[tm:tm + SUBLANES, :]
    y = y * _sigmoid(y)
    lane = lax.broadcasted_iota(jnp.int32, (1, n), 1)
    kscale = jnp.where(lane >= HEADS * ML_QK, ML_QK ** -0.5, 1.0).astype(F32)
    o_ref[...] = (y * kscale).astype(o_ref.dtype)

    gates = full[:, n:n + LANES]
    gi_ref[...] = gates[:, :HEADS] + bi_ref[...]
    gf_ref[...] = gates[:, HEADS:2 * HEADS] + bf_ref[...]
    gates_t = lax.dot_general(wgt_ref[...], h, (((1,), (1,)), ((), ())), preferred_element_type=F32)
    git_ref[...] = gates_t[:HEADS] + bit_ref[...]
    gft_ref[...] = gates_t[HEADS:] + bft_ref[...]


def _conv_proj(x2, norm, w, wgi, wgf, conv_w, conv_b, bi, bf, tm):
    T, D = x2.shape
    N = w.shape[1]
    nw, mod3, B, tps = norm
    xs, *rest = _norm_specs(D, B, tps, lambda i: i)
    wg = jnp.concatenate([wgi, wgf], axis=1)
    w_ext = jnp.concatenate([w, jnp.pad(wg, ((0, 0), (0, LANES - 2 * HEADS)))], axis=1)
    full = lambda shape: pl.BlockSpec(shape, lambda i: tuple(0 for _ in shape))
    return pl.pallas_call(
        functools.partial(_conv_proj_kernel, tps),
        grid=(T // tm,),
        in_specs=[xs(tm), *rest,
                  full((D, N + LANES)), full((2 * HEADS, D)),
                  full((CONV_WIDTH, N)), full((1, N)),
                  full((1, HEADS)), full((1, HEADS)), full((HEADS, 1)), full((HEADS, 1))],
        out_specs=[pl.BlockSpec((tm, N), lambda i: (i, 0)),
                   pl.BlockSpec((tm, HEADS), lambda i: (i, 0)),
                   pl.BlockSpec((tm, HEADS), lambda i: (i, 0)),
                   pl.BlockSpec((HEADS, tm), lambda i: (0, i)),
                   pl.BlockSpec((HEADS, tm), lambda i: (0, i))],
        out_shape=[jax.ShapeDtypeStruct((T, N), BF16),
                   jax.ShapeDtypeStruct((T, HEADS), F32),
                   jax.ShapeDtypeStruct((T, HEADS), F32),
                   jax.ShapeDtypeStruct((HEADS, T), F32),
                   jax.ShapeDtypeStruct((HEADS, T), F32)],
        scratch_shapes=[pltpu.VMEM((tm + 2 * SUBLANES, N), F32)],
        compiler_params=_cparams(("arbitrary",)),
        name="proj_conv",
    )(x2, nw.reshape(1, D), mod3, mod3, w_ext, wg.T, conv_w, conv_b.reshape(1, N),
      bi.reshape(1, HEADS), bf.reshape(1, HEADS), bi.reshape(HEADS, 1), bf.reshape(HEADS, 1))


def _attn_kernel(lam_init, q_ref, k_ref, v_ref, ga_ref, lq1_ref, lk1_ref, lq2_ref, lk2_ref, sw_ref,
                 o_ref, qt_s, vt_s, sa_s, sb_s, m_s, l_s, acc_s):
    qi = pl.program_id(2)
    tq = q_ref.shape[0]
    tk = vt_s.shape[2]
    ratio = tq // tk

    @pl.when(qi == 0)
    def _():
        for kk in range(vt_s.shape[0]):
            vt_s[kk] = v_ref[kk * tk:(kk + 1) * tk, :].T

    q = q_ref[...].astype(F32)
    lane = lax.broadcasted_iota(jnp.int32, q.shape, 1)
    qt_s[0] = jnp.where(lane < DA_QK, q, 0.0).T.astype(BF16)
    qt_s[1] = jnp.where(lane >= DA_QK, q, 0.0).T.astype(BF16)
    m_s[...] = jnp.full(m_s.shape, -jnp.inf, F32)
    l_s[...] = jnp.zeros(l_s.shape, F32)
    acc_s[...] = jnp.zeros(acc_s.shape, F32)

    def scores(kk, st_ref, q0=0):
        k = k_ref[pl.ds(pl.multiple_of(kk * tk, tk), tk), :]
        for c in range(2):
            st_ref[c, :, q0:] = jnp.dot(k, qt_s[c, :, q0:], preferred_element_type=F32)

    def process(kk, st_ref, diagonal_block=False, q0=0):
        vt = vt_s[kk]
        for c in range(2):
            st = st_ref[c, :, q0:]
            if diagonal_block:
                key = lax.broadcasted_iota(jnp.int32, st.shape, 0)
                qry = lax.broadcasted_iota(jnp.int32, st.shape, 1)
                st = jnp.where(key <= qry, st, -jnp.inf)
            m_prev = m_s[c, :, q0:]
            m_new = jnp.maximum(m_prev, jnp.max(st, axis=0, keepdims=True))
            alpha = jnp.exp2(m_prev - m_new)
            p = jnp.exp2(st - m_new)
            l_s[c, :, q0:] = alpha * l_s[c, :, q0:] + jnp.sum(p, axis=0, keepdims=True)
            acc_s[c, :, q0:] = (alpha * acc_s[c, :, q0:]
                                + jnp.dot(vt, p.astype(BF16), preferred_element_type=F32))
            m_s[c, :, q0:] = m_new

    def diagonal(first, cur, nxt):
        for d in range(ratio):
            if d + 1 < ratio:
                scores(first + d + 1, nxt, (d + 1) * tk)
            process(first + d, cur, True, d * tk)
            cur, nxt = nxt, cur

    n_below = qi * ratio
    scores(0, sa_s)

    def pair(i, carry):
        kk = 2 * i
        scores(kk + 1, sb_s)
        process(kk, sa_s)
        scores(kk + 2, sa_s)
        process(kk + 1, sb_s)
        return carry

    lax.fori_loop(0, n_below // 2, pair, 0)
    odd = lax.rem(n_below, 2) == 1

    @pl.when(odd)
    def _():
        scores(n_below, sb_s)
        process(n_below - 1, sa_s)
        diagonal(n_below, sb_s, sa_s)

    @pl.when(jnp.logical_not(odd))
    def _():
        diagonal(n_below, sa_s, sb_s)

    lam = (jnp.exp(jnp.sum(lq1_ref[...] * lk1_ref[...], axis=-1, keepdims=True))
           - jnp.exp(jnp.sum(lq2_ref[...] * lk2_ref[...], axis=-1, keepdims=True)) + lam_init)
    o = (acc_s[0] / l_s[0] - lam * (acc_s[1] / l_s[1])).T
    o = _rms(o) * sw_ref[...] * (1.0 - lam_init)
    o_ref[...] = (o * _sigmoid(ga_ref[...].astype(F32))).astype(o_ref.dtype)


def _attention(qk, plain, lam_vecs, subln_w, lam_init, B, S, tq, tk):
    T = qk.shape[0]
    nq = S // tq
    vec = pl.BlockSpec((1, DA_QK), lambda b, h, i: (0, 0))
    return pl.pallas_call(
        functools.partial(_attn_kernel, lam_init),
        grid=(B, HEADS, nq),
        in_specs=[pl.BlockSpec((tq, LANES), lambda b, h, i: (b * nq + i, h)),
                  pl.BlockSpec((S, LANES), lambda b, h, i: (b, HEADS + h)),
                  pl.BlockSpec((S, LANES), lambda b, h, i: (b, h)),
                  pl.BlockSpec((tq, LANES), lambda b, h, i: (b * nq + i, 3 * HEADS + h)),
                  vec, vec, vec, vec,
                  pl.BlockSpec((1, HEAD_V), lambda b, h, i: (0, 0))],
        out_specs=pl.BlockSpec((tq, LANES), lambda b, h, i: (b * nq + i, h)),
        out_shape=jax.ShapeDtypeStruct((T, D_MODEL), BF16),
        scratch_shapes=[pltpu.VMEM((2, LANES, tq), BF16),
                        pltpu.VMEM((S // tk, HEAD_V, tk), BF16),
                        pltpu.VMEM((2, tk, tq), F32),
                        pltpu.VMEM((2, tk, tq), F32),
                        pltpu.VMEM((2, 1, tq), F32),
                        pltpu.VMEM((2, 1, tq), F32),
                        pltpu.VMEM((2, HEAD_V, tq), F32)],
        compiler_params=_cparams(("parallel", "parallel", "arbitrary")),
        name="attn",
    )(qk, qk, plain, plain, *[v.reshape(1, DA_QK) for v in lam_vecs], subln_w.reshape(1, HEAD_V))


def _mlstm_kernel(qk_ref, v_ref, om_ref, gm_ref, gi_ref, gf_ref, git_ref, gft_ref, nw_ref,
                  o_ref, c_s, n_s, m_s):
    L = qk_ref.shape[0]

    @pl.when(pl.program_id(1) == 0)
    def _():
        c_s[...] = jnp.zeros(c_s.shape, F32)
        n_s[...] = jnp.zeros(n_s.shape, F32)
        m_s[...] = jnp.zeros(m_s.shape, F32)

    row = lax.broadcasted_iota(jnp.int32, (L, L), 0)
    col = lax.broadcasted_iota(jnp.int32, (L, L), 1)
    tri = (col <= row).astype(F32)
    causal_t = row <= col
    tri_t = causal_t.astype(F32)
    bcols = jnp.dot(tri, _log_sigmoid(gf_ref[...]), precision=HIGHEST, preferred_element_type=F32)
    brows = jnp.dot(_log_sigmoid(gft_ref[...]), tri_t, precision=HIGHEST, preferred_element_type=F32)
    ucols = gi_ref[...] - bcols
    git = git_ref[...]
    lane = lax.broadcasted_iota(jnp.int32, (1, LANES), 1)
    first_row = lax.broadcasted_iota(jnp.int32, (SUBLANES, 1), 0) == 0
    nt = (((1,), (1,)), ((), ()))

    for h in range(HEADS):
        p = h // 2
        hmask = ((lane >= (h % 2) * ML_QK) & (lane < (h % 2 + 1) * ML_QK)).astype(F32)
        qh = (qk_ref[:, p * LANES:(p + 1) * LANES].astype(F32) * hmask).astype(BF16)
        kp = qk_ref[:, (HEADS // 2 + p) * LANES:(HEADS // 2 + p + 1) * LANES]
        vt = v_ref[:, h * HEAD_V:(h + 1) * HEAD_V].T
        br = brows[h:h + 1, :]
        igr = git[h:h + 1, :]
        g_tot = br[:, L - 1:L]
        m_prev = m_s[h]
        ct_prev = c_s[h]
        n_prev = n_s[h]

        dm = jnp.where(causal_t, br + ucols[:, h:h + 1], -jnp.inf)
        m_inter = br + m_prev
        m_j = jnp.maximum(jnp.max(dm, axis=0, keepdims=True), m_inter)
        st = lax.dot_general(kp, qh, nt, preferred_element_type=F32)
        qkw = st * jnp.exp(dm - m_j)
        inter_w = jnp.exp(m_inter - m_j)
        num = (jnp.dot(vt, qkw.astype(BF16), preferred_element_type=F32)
               + inter_w * lax.dot_general(ct_prev.astype(BF16), qh, nt, preferred_element_type=F32))
        qn = lax.dot_general(n_prev.astype(BF16), qh, nt, preferred_element_type=F32)[0:1, :]
        den = jnp.sum(qkw, axis=0, keepdims=True) + inter_w * qn
        ht = num / jnp.maximum(jnp.abs(den), jnp.exp(-m_j))

        a = g_tot - br + igr
        m_loc = jnp.max(a, axis=-1, keepdims=True)
        w_loc = jnp.exp(a - m_loc)
        m_new = jnp.maximum(g_tot + m_prev, m_loc)
        dec = jnp.exp(g_tot + m_prev - m_new)
        inc = jnp.exp(m_loc - m_new)
        c_loc = jnp.dot((vt.astype(F32) * w_loc).astype(BF16), kp, preferred_element_type=F32)
        w8 = jnp.where(first_row, w_loc, 0.0)
        n_loc = jnp.dot(w8, kp.astype(F32), precision=HIGHEST, preferred_element_type=F32)
        c_s[h] = dec * ct_prev + inc * c_loc
        n_s[h] = dec * n_prev + inc * n_loc
        m_s[h] = m_new

        sl = slice(h * HEAD_V, (h + 1) * HEAD_V)
        yt = ht * lax.rsqrt(jnp.mean(ht * ht, axis=0, keepdims=True) + NORM_EPS)
        y = yt.T * nw_ref[:, sl]
        y = y * _sigmoid(om_ref[:, sl].astype(F32)) * _sigmoid(gm_ref[:, sl].astype(F32))
        o_ref[:, sl] = y.astype(o_ref.dtype)


def _mlstm(qkc, plain, gi, gf, git, gft, ml_norm_w, B, S):
    T, D = qkc.shape
    L = ML_CHUNK
    nc = S // L
    wide = lambda cb: pl.BlockSpec((L, D), lambda b, c: (b * nc + c, cb))
    return pl.pallas_call(
        _mlstm_kernel,
        grid=(B, nc),
        in_specs=[wide(0), wide(1), wide(2), wide(4),
                  pl.BlockSpec((L, HEADS), lambda b, c: (b * nc + c, 0)),
                  pl.BlockSpec((L, HEADS), lambda b, c: (b * nc + c, 0)),
                  pl.BlockSpec((HEADS, L), lambda b, c: (0, b * nc + c)),
                  pl.BlockSpec((HEADS, L), lambda b, c: (0, b * nc + c)),
                  pl.BlockSpec((1, D), lambda b, c: (0, 0))],
        out_specs=pl.BlockSpec((L, D), lambda b, c: (b * nc + c, 0)),
        out_shape=jax.ShapeDtypeStruct((T, D), BF16),
        scratch_shapes=[pltpu.VMEM((HEADS, LANES, HEAD_V), F32),
                        pltpu.VMEM((HEADS, SUBLANES, LANES), F32),
                        pltpu.VMEM((HEADS, 1, 1), F32)],
        compiler_params=_cparams(("parallel", "arbitrary")),
        name="mlstm",
    )(qkc, plain, plain, plain, gi, gf, git, gft, ml_norm_w.reshape(1, D))


def _outproj_kernel(ya_ref, ym_ref, x_ref, w_ref, gt_ref, nw_ref, sc_ref, sh_ref, x1_ref, h2t_ref):
    merged = (ya_ref[...].astype(F32) + ym_ref[...].astype(F32)).astype(BF16)
    x1 = x_ref[...] + gt_ref[0] * jnp.dot(merged, w_ref[...], preferred_element_type=F32)
    x1_ref[...] = x1
    h2 = _rms(x1) * nw_ref[...] * (1.0 + sc_ref[0]) + sh_ref[0]
    h2t_ref[...] = h2.T.astype(h2t_ref.dtype)


def _outproj(ya, ym, x2, w_out, norm2_w, mod3, B, tps, tm):
    T, D = x2.shape
    row = lambda k: pl.BlockSpec((1, 1, D), lambda i: (k * B + i // tps, 0, 0))
    tile = pl.BlockSpec((tm, D), lambda i: (i, 0))
    return pl.pallas_call(
        _outproj_kernel,
        grid=(T // tm,),
        in_specs=[tile, tile, tile, pl.BlockSpec((D, D), lambda i: (0, 0)), row(2),
                  pl.BlockSpec((1, D), lambda i: (0, 0)), row(4), row(3)],
        out_specs=[tile, pl.BlockSpec((D, tm), lambda i: (0, i))],
        out_shape=[jax.ShapeDtypeStruct((T, D), F32), jax.ShapeDtypeStruct((D, T), BF16)],
        compiler_params=_cparams(("parallel",)),
        name="outproj",
    )(ya, ym, x2, w_out, mod3, norm2_w.reshape(1, D), mod3, mod3)


def _topk_ranks(s):
    n = s.shape[0]
    it = lax.broadcasted_iota(jnp.int32, s.shape, 0)
    rank = jnp.full(s.shape, float(TOPK), F32)
    x = s
    vals = []
    for r in range(TOPK):
        m = jnp.max(x, axis=0, keepdims=True)
        first = jnp.min(jnp.where(x == m, it, n), axis=0, keepdims=True)
        hit = it == first
        x = jnp.where(hit, -jnp.inf, x)
        rank = jnp.where(hit, float(r), rank)
        vals.append(m)
    return rank, vals


MARK_BASE = 3.0e38
MARK_STEP = 1.0e36


def _topk_ranks_distinct(s):
    x = s
    vals = []
    for r in range(TOPK):
        m = jnp.max(x, axis=0, keepdims=True)
        x = jnp.where(x == m, -(MARK_BASE + r * MARK_STEP), x)
        vals.append(m)
    taken = x <= -MARK_BASE
    rank = jnp.where(taken, jnp.round((-x - MARK_BASE) * (1.0 / MARK_STEP)), float(TOPK))
    removed = jnp.sum(jnp.where(taken, 1.0, 0.0), axis=0, keepdims=True)
    return rank, vals, removed == float(TOPK)


def _cand_select(cand, top):
    it = lax.broadcasted_iota(jnp.int32, cand.shape, 0)
    sel = jnp.zeros(cand.shape, F32)
    z = jnp.zeros(top.shape, F32)
    x = cand
    for _ in range(TOPK):
        m = jnp.max(x, axis=0, keepdims=True)
        first = jnp.min(jnp.where(x == m, it, cand.shape[0]), axis=0, keepdims=True)
        hit = it == first
        x = jnp.where(hit, -jnp.inf, x)
        sel = jnp.where(hit, 1.0, sel)
        z = z + jnp.exp(m - top)
    return sel, z


def _cand_select_distinct(cand, top):
    z = jnp.zeros(top.shape, F32)
    x = cand
    for _ in range(TOPK):
        m = jnp.max(x, axis=0, keepdims=True)
        x = jnp.where(x == m, -MARK_BASE, x)
        z = z + jnp.exp(m - top)
    sel = jnp.where(x == -MARK_BASE, 1.0, 0.0)
    return sel, z, jnp.sum(sel, axis=0, keepdims=True) == float(TOPK)


def _selection(s1, s2, exact):
    if exact:
        (rank1, v1), (rank2, v2) = _topk_ranks(s1), _topk_ranks(s2)
    else:
        (rank1, v1, ok1), (rank2, v2, ok2) = _topk_ranks_distinct(s1), _topk_ranks_distinct(s2)
    tt = s1.shape[1]
    rows = [v1[r1] + v2[r2] for (r1, r2) in _CAND]
    rows += [jnp.full((1, tt), -jnp.inf, F32)] * (_CAND_ROWS - len(_CAND))
    cand = jnp.concatenate(rows, axis=0)
    top = v1[0] + v2[0]
    if exact:
        sel, z = _cand_select(cand, top)
        safe = None
    else:
        sel, z, ok3 = _cand_select_distinct(cand, top)
        safe = ok1 & ok2 & ok3
    a = jnp.zeros(s1.shape, F32)
    start = 0
    for r1 in range(TOPK):
        width = sum(1 for c in _CAND if c[0] == r1)
        cnt = jnp.sum(sel[start:start + width, :], axis=0, keepdims=True)
        a = jnp.where(rank1 == float(r1), cnt, a)
        start += width
    return (rank2, jnp.exp(s2 - v2[0]), a, jnp.exp(s1 - v1[0]) / z), safe


SEL_HEADS = 2


def _peer_sel_kernel(h2t_ref, wqt_ref, keys_ref, r2_ref, e2_ref, a_ref, c_ref):
    ht = h2t_ref[...]
    scores = []
    for hp in range(2 * SEL_HEADS):
        qt = jnp.dot(wqt_ref[hp * LANES:(hp + 1) * LANES, :], ht, preferred_element_type=F32)
        scores.append(jnp.dot(keys_ref[hp], qt.astype(BF16), preferred_element_type=F32))

    def write(hh, vals):
        for ref, val in zip((r2_ref, e2_ref, a_ref, c_ref), vals):
            for lt in range(val.shape[1] // LANES):
                ref[lt, hh * N_KEYS:(hh + 1) * N_KEYS, :] = val[:, lt * LANES:(lt + 1) * LANES].astype(ref.dtype)

    unsafe = 0.0
    for hh in range(SEL_HEADS):
        vals, safe = _selection(scores[2 * hh], scores[2 * hh + 1], exact=False)
        write(hh, vals)
        unsafe = unsafe + jnp.sum(jnp.where(safe, 0.0, 1.0))

    @pl.when(unsafe > 0.0)
    def _():
        for hh in range(SEL_HEADS):
            write(hh, _selection(scores[2 * hh], scores[2 * hh + 1], exact=True)[0])


def _peer_sel(h2t, wqt, keys, tt):
    D, T = h2t.shape
    out = pl.BlockSpec((tt // LANES, SEL_HEADS * N_KEYS, LANES), lambda i, h: (i, h, 0))
    shp = lambda dt: jax.ShapeDtypeStruct((T // LANES, HEADS * N_KEYS, LANES), dt)
    return pl.pallas_call(
        _peer_sel_kernel,
        grid=(T // tt, HEADS // SEL_HEADS),
        in_specs=[pl.BlockSpec((D, tt), lambda i, h: (0, i)),
                  pl.BlockSpec((SEL_HEADS * 2 * LANES, D), lambda i, h: (h, 0)),
                  pl.BlockSpec((SEL_HEADS * 2, N_KEYS, LANES), lambda i, h: (h, 0, 0))],
        out_specs=[out, out, out, out],
        out_shape=[shp(BF16), shp(BF16), shp(F32), shp(F32)],
        compiler_params=_cparams(("parallel", "parallel")),
        name="peer_sel",
    )(h2t, wqt, keys)


PEER_ROWS = 128
GATE_GROUP = 4
MXU_COLS = 256
ACT_ROWS = 256
OUT_ROWS = 512


def _gate_rows(row, rows):
    packed = jnp.broadcast_to(row, (2 * SUBLANES, LANES)).astype(BF16)
    return jnp.tile(packed, (rows // (2 * SUBLANES), 1))


def _peer_main_kernel(final, h2t_ref, u_ref, vt_ref, r2_ref, e2_ref, a_ref, c_ref, x1_ref, gt_ref, fw_ref,
                      o_ref, acc_ref, act_ref, p_ref):
    e = pl.program_id(1)
    eb, tt = act_ref.shape
    nj = eb // N_KEYS

    @pl.when(e == 0)
    def _():
        acc_ref[...] = jnp.zeros(acc_ref.shape, F32)

    n_half = tt // MXU_COLS
    tcols = lambda th: slice(th * MXU_COLS, (th + 1) * MXU_COLS)

    def act_piece(th, m):
        ms = slice(m * ACT_ROWS, (m + 1) * ACT_ROWS)
        act_ref[ms, tcols(th)] = jnp.dot(u_ref[ms, :], h2t_ref[:, tcols(th)], preferred_element_type=F32)

    def out_piece(th, r):
        rs = slice(r * OUT_ROWS, (r + 1) * OUT_ROWS)
        acc_ref[rs, tcols(th)] += jnp.dot(vt_ref[rs, :], p_ref[:, tcols(th)], preferred_element_type=F32)

    def gate_chunk(tl, bs):
        ls = slice(tl * LANES, (tl + 1) * LANES)
        for j0 in range(0, nj, GATE_GROUP):
            g = [None] * GATE_GROUP
            for h in range(HEADS):
                rs = slice(h * N_KEYS + bs * PEER_ROWS, h * N_KEYS + (bs + 1) * PEER_ROWS)
                r2c = r2_ref[tl, rs, :]
                e2c = e2_ref[tl, rs, :]
                for jj in range(GATE_GROUP):
                    idx = h * N_KEYS + e * nj + j0 + jj
                    arow = _gate_rows(a_ref[tl, pl.ds(idx, 1), :], PEER_ROWS)
                    crow = _gate_rows(c_ref[tl, pl.ds(idx, 1), :], PEER_ROWS)
                    w = jnp.where(r2c < arow, e2c, jnp.zeros_like(e2c)) * crow
                    g[jj] = w if g[jj] is None else g[jj] + w
            for jj in range(GATE_GROUP):
                es = slice((j0 + jj) * N_KEYS + bs * PEER_ROWS, (j0 + jj) * N_KEYS + (bs + 1) * PEER_ROWS)
                aj = act_ref[es, ls].astype(BF16)
                gelu = 0.5 * aj * (1.0 + lax.erf(aj * (2.0 ** -0.5)))
                p_ref[es, ls] = g[jj] * gelu

    n_act = eb // ACT_ROWS
    n_out = acc_ref.shape[0] // OUT_ROWS
    for m in range(n_act):
        act_piece(0, m)
    for th in range(n_half):
        mxu_work = [functools.partial(act_piece, th + 1, m) for m in range(n_act)] if th + 1 < n_half else []
        if th > 0:
            mxu_work += [functools.partial(out_piece, th - 1, r) for r in range(n_out)]
        chunks = [(th * (MXU_COLS // LANES) + lt, bs)
                  for lt in range(MXU_COLS // LANES) for bs in range(N_KEYS // PEER_ROWS)]
        per_chunk = -(-len(mxu_work) // len(chunks))
        for tl, bs in chunks:
            gate_chunk(tl, bs)
            for piece in mxu_work[:per_chunk]:
                piece()
            mxu_work = mxu_work[per_chunk:]
    for r in range(n_out):
        out_piece(n_half - 1, r)

    @pl.when(e == pl.num_programs(1) - 1)
    def _():
        x2 = x1_ref[...] + gt_ref[0] * acc_ref[...].T
        o_ref[...] = _rms(x2) * fw_ref[...] if final else x2


def _peer_main(h2t, u, vt, sel, x1, mod3, fw, final, B, tps, tt, eb):
    D, T = h2t.shape
    ne = u.shape[0]
    wide = pl.BlockSpec((tt // LANES, HEADS * N_KEYS, LANES), lambda i, e: (i, 0, 0))
    return pl.pallas_call(
        functools.partial(_peer_main_kernel, final),
        grid=(T // tt, ne // eb),
        in_specs=[pl.BlockSpec((D, tt), lambda i, e: (0, i)),
                  pl.BlockSpec((eb, D), lambda i, e: (e, 0)),
                  pl.BlockSpec((D, eb), lambda i, e: (0, e)),
                  wide, wide, wide, wide,
                  pl.BlockSpec((tt, D), lambda i, e: (i, 0)),
                  pl.BlockSpec((1, 1, D), lambda i, e: (5 * B + i // tps, 0, 0)),
                  pl.BlockSpec((1, D), lambda i, e: (0, 0))],
        out_specs=pl.BlockSpec((tt, D), lambda i, e: (i, 0)),
        out_shape=jax.ShapeDtypeStruct((T, D), F32),
        scratch_shapes=[pltpu.VMEM((D, tt), F32),
                        pltpu.VMEM((eb, tt), F32),
                        pltpu.VMEM((eb, tt), BF16)],
        compiler_params=_cparams(("parallel", "arbitrary")),
        name="peer_main",
    )(h2t, u, vt, *sel, x1, mod3, fw.reshape(1, D))


def _rope_tables(S):
    inv = ROPE_THETA ** (-jnp.arange(ROPE_HALF, dtype=F32) * 2.0 / ROPE_DIM)
    ang = jnp.arange(S, dtype=jnp.int32).astype(F32)[:, None] * inv[None, :]
    cos, sin = jnp.cos(ang), jnp.sin(ang)
    zeros = jnp.zeros((S, DA_QK - ROPE_DIM), F32)
    z8 = jnp.zeros((S, ROPE_HALF), F32)
    cosf = jnp.concatenate([cos, cos, zeros + 1.0] * 2, axis=1)
    sa = jnp.concatenate([-sin, z8, zeros] * 2, axis=1)
    sb = jnp.concatenate([z8, sin, zeros] * 2, axis=1)
    return cosf, sa, sb


def kernel(x, c, ada_w, ada_b, norm1_w, norm2_w, w_in, conv_w, conv_b, ml_i_bias, ml_f_bias, ml_norm_w, lam_q1, lam_k1, lam_q2, lam_k2, subln_w, w_out, peer_wq, peer_keys, peer_u, peer_v, final_norm_w):
    B, S, D = x.shape
    assert D == D_MODEL and S % PROJ_TM == 0 and S % ATTN_TQ == 0
    T = B * S
    depth = w_in.shape[0]
    assert depth > 0
    tm = ROW_TILE
    tps = S // tm
    tables = _rope_tables(S)
    x2 = x.reshape(T, D)
    for l in range(depth):
        mod3 = _mod(c, ada_w[l], ada_b[l])

        wl = w_in[l]
        o = 0
        cols = {}
        for name, width in (("qa", D), ("ka", D), ("va", D), ("qm", D // 2), ("km", D // 2), ("vm", D),
                            ("om", D), ("ip", HEADS), ("fp", HEADS), ("ga", D), ("gm", D)):
            cols[name] = wl[:, o:o + width]
            o += width
        w_rope = jnp.concatenate([cols["qa"], cols["ka"]], axis=1).astype(BF16)
        w_conv = jnp.concatenate([cols["qm"], cols["km"]], axis=1).astype(BF16)
        w_plain = jnp.concatenate([cols[n] for n in ("va", "vm", "om", "ga", "gm")], axis=1).astype(BF16)

        wide = (norm1_w[l], mod3, B, S // PROJ_TM)
        qk = _rope_proj(x2, wide, w_rope, tables, PROJ_TM)
        plain = _proj(x2, wide, w_plain, PROJ_TM, D)
        qkc, gi, gf, git, gft = _conv_proj(x2, (norm1_w[l], mod3, B, tps), w_conv,
                                           cols["ip"].astype(BF16), cols["fp"].astype(BF16),
                                           conv_w[l], conv_b[l], ml_i_bias[l], ml_f_bias[l], tm)

        lam_init = 0.8 - 0.6 * math.exp(-0.3 * l)
        ya = _attention(qk, plain, (lam_q1[l], lam_k1[l], lam_q2[l], lam_k2[l]), subln_w[l], lam_init, B, S, ATTN_TQ, ATTN_TK)
        ym = _mlstm(qkc, plain, gi, gf, git, gft, ml_norm_w[l], B, S)
        x1, h2t = _outproj(ya, ym, x2, w_out[l].astype(BF16), norm2_w[l], mod3, B, tps, tm)

        wqt = peer_wq[l].T.astype(BF16)
        keys = peer_keys[l].reshape(2 * HEADS, N_KEYS, LANES).astype(BF16)
        sel = _peer_sel(h2t, wqt, keys, tm)
        x2 = _peer_main(h2t, peer_u[l].astype(BF16), peer_v[l].T.astype(BF16), sel, x1, mod3,
                        final_norm_w, l == depth - 1, B, tps, tm, PEER_EXPERT_BLOCK)
    return x2.reshape(B, S, D)
```
